```python
import math
import jax
import jax.numpy as jnp
from jax import lax
import numpy as np

D_MODEL = 1024
BATCH = 8
SEQ = 8192
DEPTH = 2

HEAD_DIM = 64
GRID_W = 64
N_MEM = 256
NA_HEADS = 6
NA_WIN_ROWS = 8
NA_WIN_COLS = 16
ML_HEADS = 4
ML_CHUNK = 64
GDN_HEADS = 6
GDN_CHUNK = 64
CONV_K = 5
N_DIR = 2
XA_HEADS = 4
XA_HEAD_DIM = D_MODEL // XA_HEADS
D_FF = 4 * D_MODEL
NA_WIDTH = NA_HEADS * HEAD_DIM
ML_WIDTH = ML_HEADS * HEAD_DIM
GDN_WIDTH = GDN_HEADS * HEAD_DIM
D_MIX = NA_WIDTH + ML_WIDTH + GDN_WIDTH
EPS = 1e-6
IN_SIZES = (NA_WIDTH, NA_WIDTH, NA_WIDTH,
            ML_WIDTH, ML_WIDTH, ML_WIDTH, ML_WIDTH, N_DIR * ML_HEADS, N_DIR * ML_HEADS,
            GDN_WIDTH, GDN_WIDTH, GDN_WIDTH, GDN_WIDTH, N_DIR * GDN_HEADS, N_DIR * GDN_HEADS)
IN_OFFSETS = tuple(int(o) for o in np.cumsum(IN_SIZES)[:-1])
P_IN = int(sum(IN_SIZES))

kernel_name = 'hybrid_na_mlstm_gdn_encoder'


def rmsnorm(x, w):
    xf = x.astype(jnp.float32)
    y = xf * lax.rsqrt(jnp.mean(xf * xf, axis=-1, keepdims=True) + EPS)
    return (y * w.astype(jnp.float32)).astype(x.dtype)


def head_rms(t):
    return t * lax.rsqrt(jnp.mean(t * t, axis=-1, keepdims=True) + EPS)


def l2norm(t):
    return t * lax.rsqrt(jnp.sum(t * t, axis=-1, keepdims=True) + EPS)


def neighbourhood_attention(q, k, v, rel_bias):
    b, s, h, dh = q.shape
    rows = s // GRID_W
    wr = min(NA_WIN_ROWS, rows)
    wc = NA_WIN_COLS
    qg = (q * dh ** -0.5).reshape(b, rows, GRID_W, h, dh)
    kg = k.reshape(b, rows, GRID_W, h, dh)
    vg = v.reshape(b, rows, GRID_W, h, dh)
    cols = jnp.arange(GRID_W)
    col_idx = jnp.clip(cols - wc // 2, 0, GRID_W - wc)[:, None] + jnp.arange(wc)[None, :]
    dc = col_idx - cols[:, None] + (NA_WIN_COLS - 1)

    def row_block(args):
        q_r, r = args
        r0 = jnp.clip(r - wr // 2, 0, rows - wr)
        k_win = lax.dynamic_slice_in_dim(kg, r0, wr, axis=1)[:, :, col_idx]
        v_win = lax.dynamic_slice_in_dim(vg, r0, wr, axis=1)[:, :, col_idx]
        dr = r0 + jnp.arange(wr) - r + (NA_WIN_ROWS - 1)
        bias = rel_bias[:, dr[None, :, None], dc[:, None, :]]
        logits = jnp.einsum('bqhd,brqchd->bhqrc', q_r, k_win).astype(jnp.float32) + bias.astype(jnp.float32)
        p = jax.nn.softmax(logits.reshape(b, h, GRID_W, wr * wc), axis=-1).reshape(logits.shape)
        return jnp.einsum('bhqrc,brqchd->bqhd', p.astype(v.dtype), v_win)

    out = lax.map(row_block, (jnp.moveaxis(qg, 1, 0), jnp.arange(rows)))
    return jnp.moveaxis(out, 0, 1).reshape(b, s, h * dh)


def mlstm_direction(q, k, v, ig, f_pre):
    b, h, s, d = q.shape
    L = ML_CHUNK
    nc = s // L
    q = q.reshape(b, h, nc, L, d)
    k = k.reshape(b, h, nc, L, d) * d ** -0.5
    v = v.reshape(b, h, nc, L, d)
    ig = ig.reshape(b, h, nc, L)
    bcum = jnp.cumsum(jax.nn.log_sigmoid(f_pre).reshape(b, h, nc, L), axis=-1)
    b_last = bcum[..., -1]
    w = b_last[..., None] - bcum + ig
    m_loc = jnp.max(w, axis=-1)
    wexp = jnp.exp(w - m_loc[..., None])
    c_loc = jnp.einsum('bhnl,bhnld,bhnle->nbhde', wexp, k, v)
    n_loc = jnp.einsum('bhnl,bhnld->nbhd', wexp, k)

    def step(carry, xs):
        c, n, m = carry
        c_l, n_l, m_l, bl = xs
        m_new = jnp.maximum(bl + m, m_l)
        a = jnp.exp(bl + m - m_new)
        g = jnp.exp(m_l - m_new)
        c_new = a[..., None, None] * c + g[..., None, None] * c_l
        n_new = a[..., None] * n + g[..., None] * n_l
        return (c_new, n_new, m_new), (c, n, m)

    init = (jnp.zeros((b, h, d, d), jnp.float32), jnp.zeros((b, h, d), jnp.float32), jnp.zeros((b, h), jnp.float32))
    _, (c_prev, n_prev, m_prev) = lax.scan(
        step, init, (c_loc, n_loc, jnp.moveaxis(m_loc, -1, 0), jnp.moveaxis(b_last, -1, 0)))
    c_prev = jnp.moveaxis(c_prev, 0, 2)
    n_prev = jnp.moveaxis(n_prev, 0, 2)
    m_prev = jnp.moveaxis(m_prev, 0, -1)
    causal = jnp.tril(jnp.ones((L, L), dtype=bool))
    dmat = jnp.where(causal, bcum[..., :, None] - bcum[..., None, :] + ig[..., None, :], -jnp.inf)
    inter = bcum + m_prev[..., None]
    m_t = jnp.maximum(jnp.max(dmat, axis=-1), inter)
    s_intra = jnp.einsum('bhnld,bhnsd->bhnls', q, k) * jnp.exp(dmat - m_t[..., None])
    a_inter = jnp.exp(inter - m_t)
    num = a_inter[..., None] * jnp.einsum('bhnld,bhnde->bhnle', q, c_prev) + jnp.einsum('bhnls,bhnse->bhnle', s_intra, v)
    den = a_inter * jnp.einsum('bhnld,bhnd->bhnl', q, n_prev) + jnp.sum(s_intra, axis=-1)
    out = num / jnp.maximum(jnp.abs(den), jnp.exp(-m_t))[..., None]
    return out.reshape(b, h, s, d)


def mlstm_mixer(q, k, v, o_pre, i_pre, f_pre, i_bias, f_bias, norm_w):
    b, s, _ = q.shape

    def heads(t):
        return t.reshape(b, s, ML_HEADS, HEAD_DIM).transpose(0, 2, 1, 3).astype(jnp.float32)

    qh, kh, vh = heads(q), heads(k), heads(v)
    ig = (i_pre.reshape(b, s, N_DIR, ML_HEADS).astype(jnp.float32) + i_bias.astype(jnp.float32)).transpose(2, 0, 3, 1)
    fg = (f_pre.reshape(b, s, N_DIR, ML_HEADS).astype(jnp.float32) + f_bias.astype(jnp.float32)).transpose(2, 0, 3, 1)
    h_fwd = mlstm_direction(qh, kh, vh, ig[0], fg[0])
    h_bwd = jnp.flip(mlstm_direction(jnp.flip(qh, 2), jnp.flip(kh, 2), jnp.flip(vh, 2),
                                     jnp.flip(ig[1], 2), jnp.flip(fg[1], 2)), 2)
    hs = head_rms((h_fwd + h_bwd).transpose(0, 2, 1, 3)).reshape(b, s, ML_WIDTH)
    out = hs * norm_w.astype(jnp.float32) * jax.nn.sigmoid(o_pre.astype(jnp.float32))
    return out.astype(q.dtype)


def centred_depthwise_conv(x, w):
    c = x.shape[-1]
    return lax.conv_general_dilated(
        x, w[:, None, :].astype(x.dtype), window_strides=(1,),
        padding=[(CONV_K // 2, CONV_K // 2)],
        dimension_numbers=('NWC', 'WIO', 'NWC'), feature_group_count=c)


def gdn_direction(q, k, v, g, beta):
    b, h, s, dk = q.shape
    dv = v.shape[-1]
    L = GDN_CHUNK
    nc = s // L
    q = (q * dk ** -0.5).reshape(b, h, nc, L, dk)
    k = k.reshape(b, h, nc, L, dk)
    v = v.reshape(b, h, nc, L, dv)
    beta = beta.reshape(b, h, nc, L)
    gc = jnp.cumsum(g.reshape(b, h, nc, L), axis=-1)
    incl = jnp.tril(jnp.ones((L, L), dtype=bool))
    strict = jnp.tril(jnp.ones((L, L), dtype=bool), -1)
    diff = gc[..., :, None] - gc[..., None, :]
    lmask = jnp.where(incl, jnp.exp(jnp.where(incl, diff, 0.0)), 0.0)
    kb = k * beta[..., None]
    a = jnp.where(strict, jnp.einsum('bhnld,bhnsd->bhnls', kb, k) * lmask, 0.0)
    eye = jnp.eye(L, dtype=jnp.float32)
    t = lax.linalg.triangular_solve(a + eye, jnp.broadcast_to(eye, a.shape),
                                    left_side=True, lower=True, unit_diagonal=True)
    u = jnp.einsum('bhnls,bhnse->bhnle', t, v * beta[..., None])
    wk = jnp.einsum('bhnls,bhnsd->bhnld', t, kb * jnp.exp(gc)[..., None])
    attn = jnp.where(incl, jnp.einsum('bhnld,bhnsd->bhnls', q, k) * lmask, 0.0)
    q_dec = q * jnp.exp(gc)[..., None]
    g_last = gc[..., -1]
    k_state = k * jnp.exp(g_last[..., None] - gc)[..., None]

    def step(state, xs):
        u_c, w_c, a_c, qd_c, ks_c, gl_c = xs
        v_new = u_c - jnp.einsum('bhld,bhde->bhle', w_c, state)
        o = jnp.einsum('bhld,bhde->bhle', qd_c, state) + jnp.einsum('bhls,bhse->bhle', a_c, v_new)
        state = state * jnp.exp(gl_c)[..., None, None] + jnp.einsum('bhld,bhle->bhde', ks_c, v_new)
        return state, o

    xs = tuple(jnp.moveaxis(t_, 2, 0) for t_ in (u, wk, attn, q_dec, k_state, g_last))
    _, o = lax.scan(step, jnp.zeros((b, h, dk, dv), jnp.float32), xs)
    return jnp.moveaxis(o, 0, 2).reshape(b, h, s, dv)


def gated_deltanet_mixer(q, k, v, z, beta_pre, alpha_pre, conv_w, a_log, dt_bias, norm_w):
    b, s, _ = q.shape
    qkv = jax.nn.silu(centred_depthwise_conv(jnp.concatenate([q, k, v], axis=-1), conv_w))
    qc, kc, vc = jnp.split(qkv, 3, axis=-1)

    def heads(t):
        return t.reshape(b, s, GDN_HEADS, HEAD_DIM).transpose(0, 2, 1, 3).astype(jnp.float32)

    qh, kh, vh = l2norm(heads(qc)), l2norm(heads(kc)), heads(vc)
    beta = jax.nn.sigmoid(beta_pre.reshape(b, s, N_DIR, GDN_HEADS).astype(jnp.float32)).transpose(2, 0, 3, 1)
    g = (-jnp.exp(a_log.astype(jnp.float32)) * jax.nn.softplus(
        alpha_pre.reshape(b, s, N_DIR, GDN_HEADS).astype(jnp.float32) + dt_bias.astype(jnp.float32))).transpose(2, 0, 3, 1)
    o_fwd = gdn_direction(qh, kh, vh, g[0], beta[0])
    o_bwd = jnp.flip(gdn_direction(jnp.flip(qh, 2), jnp.flip(kh, 2), jnp.flip(vh, 2),
                                   jnp.flip(g[1], 2), jnp.flip(beta[1], 2)), 2)
    o = head_rms((o_fwd + o_bwd).transpose(0, 2, 1, 3)).reshape(b, s, GDN_WIDTH)
    o = o * norm_w.astype(jnp.float32) * jax.nn.silu(z.astype(jnp.float32))
    return o.astype(z.dtype)


def memory_cross_attention(h, mem_n, w_q, w_kv, w_o):
    b, s, _ = h.shape
    m = mem_n.shape[1]
    q = (h @ w_q).reshape(b, s, XA_HEADS, XA_HEAD_DIM)
    kk, vv = jnp.split(mem_n @ w_kv, 2, axis=-1)
    kk = kk.reshape(b, m, XA_HEADS, XA_HEAD_DIM)
    vv = vv.reshape(b, m, XA_HEADS, XA_HEAD_DIM)
    logits = jnp.einsum('bshd,bmhd->bhsm', q, kk).astype(jnp.float32) * XA_HEAD_DIM ** -0.5
    p = jax.nn.softmax(logits, axis=-1)
    o = jnp.einsum('bhsm,bmhd->bshd', p.astype(vv.dtype), vv).reshape(b, s, XA_HEADS * XA_HEAD_DIM)
    return o @ w_o


def setup_inputs(seed: int = 0) -> dict:
    key = jax.random.key(seed)
    ks = jax.random.split(key, 24)
    f32 = jnp.float32

    def nrm(k, shape, scale):
        return jax.random.normal(k, shape, f32) * scale

    def gain(k, shape):
        return 1.0 + 0.02 * jax.random.normal(k, shape, f32)

    dt = jnp.exp(jax.random.uniform(ks[12], (DEPTH, N_DIR, GDN_HEADS), f32, math.log(1e-3), math.log(1e-1)))
    return {
        'x': nrm(ks[0], (BATCH, SEQ, D_MODEL), 1.0),
        'mem': nrm(ks[1], (BATCH, N_MEM, D_MODEL), 1.0),
        'norm_mix_w': gain(ks[2], (DEPTH, D_MODEL)),
        'w_in': nrm(ks[3], (DEPTH, D_MODEL, P_IN), D_MODEL ** -0.5),
        'na_rel_bias': nrm(ks[4], (DEPTH, NA_HEADS, 2 * NA_WIN_ROWS - 1, 2 * NA_WIN_COLS - 1), 0.02),
        'ml_i_bias': nrm(ks[5], (DEPTH, N_DIR, ML_HEADS), 0.1),
        'ml_f_bias': 3.0 + 3.0 * jax.random.uniform(ks[6], (DEPTH, N_DIR, ML_HEADS), f32),
        'ml_norm_w': gain(ks[7], (DEPTH, ML_WIDTH)),
        'gdn_conv_w': nrm(ks[8], (DEPTH, CONV_K, 3 * GDN_WIDTH), CONV_K ** -0.5),
        'gdn_a_log': jnp.log(jax.random.uniform(ks[9], (DEPTH, N_DIR, GDN_HEADS), f32, 1.0, 16.0)),
        'gdn_dt_bias': dt + jnp.log(-jnp.expm1(-dt)),
        'gdn_norm_w': gain(ks[10], (DEPTH, GDN_WIDTH)),
        'w_out': nrm(ks[11], (DEPTH, D_MIX, D_MODEL), D_MIX ** -0.5),
        'norm_xa_w': gain(ks[13], (DEPTH, D_MODEL)),
        'norm_mem_w': gain(ks[14], (DEPTH, D_MODEL)),
        'w_xq': nrm(ks[15], (DEPTH, D_MODEL, XA_HEADS * XA_HEAD_DIM), D_MODEL ** -0.5),
        'w_xkv': nrm(ks[16], (DEPTH, D_MODEL, 2 * XA_HEADS * XA_HEAD_DIM), D_MODEL ** -0.5),
        'w_xo': nrm(ks[17], (DEPTH, XA_HEADS * XA_HEAD_DIM, D_MODEL), (XA_HEADS * XA_HEAD_DIM) ** -0.5),
        'norm_ffn_w': gain(ks[18], (DEPTH, D_MODEL)),
        'w_ff1': nrm(ks[19], (DEPTH, D_MODEL, D_FF), D_MODEL ** -0.5),
        'w_ff2': nrm(ks[20], (DEPTH, D_FF, D_MODEL), D_FF ** -0.5),
        'norm_out_w': gain(ks[21], (D_MODEL,)),
    }


def reference(x, mem, norm_mix_w, w_in, na_rel_bias, ml_i_bias, ml_f_bias, ml_norm_w,
              gdn_conv_w, gdn_a_log, gdn_dt_bias, gdn_norm_w, w_out, norm_xa_w, norm_mem_w,
              w_xq, w_xkv, w_xo, norm_ffn_w, w_ff1, w_ff2, norm_out_w):
    b, s, _ = x.shape
    for l in range(DEPTH):
        h = rmsnorm(x, norm_mix_w[l])
        (na_q, na_k, na_v, ml_q, ml_k, ml_v, ml_o, ml_i, ml_f,
         gd_q, gd_k, gd_v, gd_z, gd_b, gd_a) = jnp.split(h @ w_in[l], IN_OFFSETS, axis=-1)
        y_na = neighbourhood_attention(na_q.reshape(b, s, NA_HEADS, HEAD_DIM),
                                       na_k.reshape(b, s, NA_HEADS, HEAD_DIM),
                                       na_v.reshape(b, s, NA_HEADS, HEAD_DIM), na_rel_bias[l])
        y_ml = mlstm_mixer(ml_q, ml_k, ml_v, ml_o, ml_i, ml_f, ml_i_bias[l], ml_f_bias[l], ml_norm_w[l])
        y_gd = gated_deltanet_mixer(gd_q, gd_k, gd_v, gd_z, gd_b, gd_a, gdn_conv_w[l],
                                    gdn_a_log[l], gdn_dt_bias[l], gdn_norm_w[l])
        x = x + jnp.concatenate([y_na, y_ml, y_gd], axis=-1) @ w_out[l]
        x = x + memory_cross_attention(rmsnorm(x, norm_xa_w[l]), rmsnorm(mem, norm_mem_w[l]),
                                       w_xq[l], w_xkv[l], w_xo[l])
        h = rmsnorm(x, norm_ffn_w[l])
        x = x + jnp.square(jax.nn.relu(h @ w_ff1[l])) @ w_ff2[l]
    return rmsnorm(x, norm_out_w)
```

```python
import functools

import numpy as np
import jax
import jax.numpy as jnp
from jax import lax
from jax.experimental import pallas as pl
from jax.experimental.pallas import tpu as pltpu

F32 = jnp.float32
BF16 = jnp.bfloat16

D_MODEL = 1024
HEAD_DIM = 64
GRID_W = 64
NA_HEADS = 6
NA_WIN_ROWS = 8
NA_WIN_COLS = 16
ML_HEADS = 4
GDN_HEADS = 6
CONV_K = 5
N_DIR = 2
XA_HEADS = 4
XA_HEAD_DIM = D_MODEL // XA_HEADS
D_FF = 4 * D_MODEL
NA_WIDTH = NA_HEADS * HEAD_DIM
ML_WIDTH = ML_HEADS * HEAD_DIM
GDN_WIDTH = GDN_HEADS * HEAD_DIM
EPS = 1e-6
CHUNK = 64
PAIR = 2 * HEAD_DIM
NEG = -1e30

IN_SIZES = (NA_WIDTH, NA_WIDTH, NA_WIDTH,
            ML_WIDTH, ML_WIDTH, ML_WIDTH, ML_WIDTH, N_DIR * ML_HEADS, N_DIR * ML_HEADS,
            GDN_WIDTH, GDN_WIDTH, GDN_WIDTH, GDN_WIDTH, N_DIR * GDN_HEADS, N_DIR * GDN_HEADS)

GATE_ML_I = 0
GATE_ML_F = GATE_ML_I + N_DIR * ML_HEADS
GATE_GD_B = GATE_ML_F + N_DIR * ML_HEADS
GATE_GD_A = GATE_GD_B + N_DIR * GDN_HEADS
GATE_USED = GATE_GD_A + N_DIR * GDN_HEADS
GATE_LANES = 128

V7X_VMEM_LIMIT = 56 * 1024 * 1024
MXU_COLS = 256

TM_ROWS = 512
NA_ROWS_PER_STEP = 8
SEQ_BLOCK = 1024


def _params(*sem):
    return pltpu.CompilerParams(dimension_semantics=sem, vmem_limit_bytes=V7X_VMEM_LIMIT)


def _dot(a, b):
    return jnp.dot(a, b, preferred_element_type=F32)


def _dot_nt(a, b):
    return lax.dot_general(a, b, (((1,), (1,)), ((), ())), preferred_element_type=F32)


def _dot_tn(a, b):
    return lax.dot_general(a, b, (((0,), (0,)), ((), ())), preferred_element_type=F32)


def _split3(x):
    x1 = x.astype(BF16)
    r1 = x - x1.astype(F32)
    x2 = r1.astype(BF16)
    x3 = (r1 - x2.astype(F32)).astype(BF16)
    return x1, x2, x3


def _exact_left(m01, x):
    x1, x2, x3 = _split3(x)
    return _dot(m01, x1) + _dot(m01, x2) + _dot(m01, x3)


def _exact_right(x, m01):
    x1, x2, x3 = _split3(x)
    return _dot(x1, m01) + _dot(x2, m01) + _dot(x3, m01)


def _rms(x, w):
    ms = jnp.mean(x * x, axis=-1, keepdims=True)
    return x * lax.rsqrt(ms + EPS) * w


def _softplus(x):
    return jnp.maximum(x, 0.0) + jnp.log1p(jnp.exp(-jnp.abs(x)))


def _sigmoid(x):
    return 1.0 / (1.0 + jnp.exp(-x))


def _segment_mean_matrix(width):
    r = lax.broadcasted_iota(jnp.int32, (width, width), 0) // HEAD_DIM
    c = lax.broadcasted_iota(jnp.int32, (width, width), 1) // HEAD_DIM
    return jnp.where(r == c, 1.0, 0.0).astype(BF16)


def _head_sumsq(t):
    return _exact_right(t * t, _segment_mean_matrix(t.shape[-1]))


def _pair_masks(rev):
    r = lax.broadcasted_iota(jnp.int32, (PAIR, PAIR), 0)
    c = lax.broadcasted_iota(jnp.int32, (PAIR, PAIR), 1)
    same = (r // HEAD_DIM) == (c // HEAD_DIM)
    t, s = r % HEAD_DIM, c % HEAD_DIM
    if rev:
        return same & (s >= t), same & (s > t)
    return same & (s <= t), same & (s < t)


def _chunk_tri(rev):
    t = lax.broadcasted_iota(jnp.int32, (CHUNK, CHUNK), 0)
    s = lax.broadcasted_iota(jnp.int32, (CHUNK, CHUNK), 1)
    return jnp.where((s >= t) if rev else (s <= t), 1.0, 0.0).astype(BF16)


def _stack_pair(x2):
    lane = lax.broadcasted_iota(jnp.int32, x2.shape, 1)
    zero = jnp.zeros_like(x2)
    return jnp.concatenate([jnp.where(lane < HEAD_DIM, x2, zero),
                            jnp.where(lane >= HEAD_DIM, x2, zero)], axis=0)


def _pair_col(tile, lane0):
    return jnp.concatenate([tile[:, lane0:lane0 + 1], tile[:, lane0 + 1:lane0 + 2]], axis=0)


def _pair_last(col, rev):
    i0 = 0 if rev else CHUNK - 1
    a = jnp.broadcast_to(col[i0:i0 + 1, :], (CHUNK, 1))
    b = jnp.broadcast_to(col[CHUNK + i0:CHUNK + i0 + 1, :], (CHUNK, 1))
    return jnp.concatenate([a, b], axis=0)


IN_SEGMENTS = (
    (NA_WIDTH, BF16), (NA_WIDTH, BF16), (NA_WIDTH, BF16),
    (ML_WIDTH, BF16), (ML_WIDTH, BF16), (ML_WIDTH, BF16), (ML_WIDTH, F32),
    (3 * GDN_WIDTH, F32), (GDN_WIDTH, F32), (GATE_LANES, F32))
IN_COLS = sum(w for w, _ in IN_SEGMENTS)
IN_DOT_COLS = 3 * MXU_COLS


def _in_proj_kernel(x_ref, nw_ref, w_ref, *out_refs):
    h = _rms(x_ref[...], nw_ref[...]).astype(BF16)
    starts = np.cumsum([0] + [w for w, _ in IN_SEGMENTS])
    for c0 in range(0, IN_COLS, IN_DOT_COLS):
        c1 = min(c0 + IN_DOT_COLS, IN_COLS)
        acc = _dot(h, w_ref[:, c0:c1])
        for o_ref, s0, s1 in zip(out_refs, starts[:-1], starts[1:]):
            a, b = max(c0, int(s0)), min(c1, int(s1))
            if a < b:
                o_ref[:, a - int(s0):b - int(s0)] = acc[:, a - c0:b - c0].astype(o_ref.dtype)


def _in_proj(x2d, norm_w, w_perm):
    t = x2d.shape[0]
    tm = min(TM_ROWS, t)
    outs = tuple(jax.ShapeDtypeStruct((t, w), dt) for w, dt in IN_SEGMENTS)
    return pl.pallas_call(
        _in_proj_kernel,
        out_shape=outs,
        grid=(t // tm,),
        in_specs=[pl.BlockSpec((tm, D_MODEL), lambda i: (i, 0)),
                  pl.BlockSpec((1, D_MODEL), lambda i: (0, 0)),
                  pl.BlockSpec((D_MODEL, IN_COLS), lambda i: (0, 0))],
        out_specs=tuple(pl.BlockSpec((tm, w), lambda i: (i, 0)) for w, _ in IN_SEGMENTS),
        compiler_params=_params("parallel"),
        name="in_proj",
    )(x2d, norm_w.reshape(1, D_MODEL), w_perm)


def _permute_w_in(w):
    parts = jnp.split(w, np.cumsum(IN_SIZES)[:-1], axis=-1)
    (na_q, na_k, na_v, ml_q, ml_k, ml_v, ml_o, ml_i, ml_f, gd_q, gd_k, gd_v, gd_z, gd_b, gd_a) = parts
    pad = jnp.zeros((w.shape[0], GATE_LANES - GATE_USED), w.dtype)
    return jnp.concatenate([na_q, na_k, na_v, ml_q, ml_k, ml_v, ml_o, gd_q, gd_k, gd_v, gd_z,
                            ml_i, ml_f, gd_b, gd_a, pad], axis=-1).astype(BF16)


def _na_bias_table(rel_bias):
    wr, wc = NA_WIN_ROWS, NA_WIN_COLS
    q = np.arange(GRID_W)
    c0 = np.clip(q - wc // 2, 0, GRID_W - wc)
    kc = np.arange(GRID_W)
    valid = (kc[None, :] >= c0[:, None]) & (kc[None, :] < c0[:, None] + wc)
    dc = np.clip(kc[None, :] - q[:, None] + (wc - 1), 0, 2 * wc - 2)
    var = np.arange(wr)
    j = np.arange(wr)
    dr = j[None, :] - var[:, None] + (wr - 1)
    tab = rel_bias.astype(F32)[:, dr[:, None, :, None], dc[None, :, None, :]]
    tab = jnp.where(valid[None, None, :, None, :], tab, NEG)
    tab = tab.reshape(NA_HEADS // 2, 2, wr, GRID_W, wr * GRID_W)
    return jnp.moveaxis(tab, 1, 2).reshape(NA_HEADS // 2, wr, 2 * GRID_W, wr * GRID_W)


def _na_kernel(q_ref, k_ref, v_ref, bias_ref, o_ref, *, rows, rows_per_step):
    j = pl.program_id(1)
    nkeys = NA_WIN_ROWS * GRID_W
    lane = lax.broadcasted_iota(jnp.int32, (GRID_W, PAIR), 1)

    def body(rr, carry):
        r = j * rows_per_step + rr
        r0 = jnp.clip(r - NA_WIN_ROWS // 2, 0, rows - NA_WIN_ROWS)
        variant = r - r0
        qoff = pl.multiple_of(rr * GRID_W, GRID_W)
        koff = pl.multiple_of(r0 * GRID_W, GRID_W)
        for p in range(NA_HEADS // 2):
            cols = slice(p * PAIR, (p + 1) * PAIR)
            q_pair = _stack_pair(q_ref[pl.ds(qoff, GRID_W), cols])
            k2 = k_ref[pl.ds(koff, nkeys), cols]
            v2 = v_ref[pl.ds(koff, nkeys), cols]
            s = _dot_nt(q_pair, k2) * (HEAD_DIM ** -0.5) + bias_ref[p, variant]
            m = jnp.max(s, axis=-1, keepdims=True)
            e = jnp.exp(s - m)
            l = jnp.sum(e, axis=-1, keepdims=True)
            o = _dot(e.astype(BF16), v2) / l
            o_ref[pl.ds(qoff, GRID_W), cols] = jnp.where(
                lane < HEAD_DIM, o[:GRID_W], o[GRID_W:]).astype(o_ref.dtype)
        return carry

    lax.fori_loop(0, rows_per_step, body, 0)


def _neighbourhood_attention(q, k, v, bias_tab):
    b, s, _ = q.shape
    rows = s // GRID_W
    assert rows >= NA_WIN_ROWS
    rps = min(NA_ROWS_PER_STEP, rows)
    tq = rps * GRID_W
    return pl.pallas_call(
        functools.partial(_na_kernel, rows=rows, rows_per_step=rps),
        out_shape=jax.ShapeDtypeStruct((b, s, NA_WIDTH), BF16),
        grid=(b, rows // rps),
        in_specs=[pl.BlockSpec((None, tq, NA_WIDTH), lambda bi, j: (bi, j, 0)),
                  pl.BlockSpec((None, s, NA_WIDTH), lambda bi, j: (bi, 0, 0)),
                  pl.BlockSpec((None, s, NA_WIDTH), lambda bi, j: (bi, 0, 0)),
                  pl.BlockSpec(bias_tab.shape, lambda bi, j: (0, 0, 0, 0))],
        out_specs=pl.BlockSpec((None, tq, NA_WIDTH), lambda bi, j: (bi, j, 0)),
        compiler_params=_params("parallel", "arbitrary"),
        name="neighbourhood_attention",
    )(q, k, v, bias_tab)


def _gate_rows(gates, lane0, heads):
    b, s, _ = gates.shape
    g = gates[:, :, lane0:lane0 + N_DIR * heads].reshape(b, s // CHUNK, CHUNK, N_DIR, heads // 2, 2)
    return jnp.transpose(g, (0, 3, 4, 1, 5, 2)).reshape(b, N_DIR, heads // 2, s // CHUNK, PAIR)


def _param_rows(p, heads):
    return jnp.repeat(p.astype(F32).reshape(N_DIR, heads // 2, 2), HEAD_DIM, axis=-1).reshape(
        N_DIR, heads // 2, 1, PAIR)


def _lane_vector(entries):
    v = jnp.zeros((GATE_LANES,), F32)
    for lane0, vals in entries:
        v = lax.dynamic_update_slice(v, vals.astype(F32).reshape(-1), (lane0,))
    return v.reshape(1, GATE_LANES)


def _pair_cumsum_matrix(rev):
    valid, _ = _pair_masks(rev)
    r = lax.broadcasted_iota(jnp.int32, (PAIR, PAIR), 0)
    c = lax.broadcasted_iota(jnp.int32, (PAIR, PAIR), 1)
    same = (r // HEAD_DIM) == (c // HEAD_DIM)
    sp, s = r % HEAD_DIM, c % HEAD_DIM
    del valid
    return jnp.where(same & ((sp >= s) if rev else (sp <= s)), 1.0, 0.0).astype(BF16)


ML_PAIRS = ML_HEADS // 2


def _mlstm_kernel(qf_ref, kf_ref, vf_ref, gf_ref, if_ref, ff_ref,
                  qb_ref, kb_ref, vb_ref, gb_ref, ib_ref, fb_ref,
                  bias_ref, ibr_ref, fbr_ref,
                  hf_ref, hb_ref,
                  c_scr, m_scr, br_scr, ir_scr, *, chunks):
    j = pl.program_id(1)

    @pl.when(j == 0)
    def _():
        c_scr[...] = jnp.zeros_like(c_scr)
        m_scr[...] = jnp.zeros_like(m_scr)

    bias = bias_ref[...]
    for d, (i_ref, f_ref) in enumerate(((if_ref, ff_ref), (ib_ref, fb_ref))):
        ucum = _pair_cumsum_matrix(rev=bool(d))
        for p in range(ML_PAIRS):
            logf = -_softplus(-(f_ref[p] + fbr_ref[d, p]))
            bcum = _exact_right(logf, ucum)
            ig = i_ref[p] + ibr_ref[d, p]
            for c in range(chunks):
                br_scr[d * ML_PAIRS + p, c] = bcum[c:c + 1, :]
                ir_scr[d * ML_PAIRS + p, c] = ig[c:c + 1, :]

    lane = lax.broadcasted_iota(jnp.int32, (CHUNK, PAIR), 1)
    one_hi = jnp.where(lane == HEAD_DIM, 1.0, 0.0).astype(BF16)
    one_lo = jnp.where(lane == 0, 1.0, 0.0).astype(BF16)

    def direction(d, c, q_ref, k_ref, v_ref, g_ref, h_ref):
        rev = bool(d)
        valid, _ = _pair_masks(rev)
        r0 = pl.multiple_of(c * CHUNK, CHUNK)
        gt = g_ref[pl.ds(r0, CHUNK), :] + bias
        logf = -_softplus(-gt)
        bcum_all = _exact_left(_chunk_tri(rev), logf)
        for p in range(ML_PAIRS):
            u = d * ML_PAIRS + p
            cols = slice(p * PAIR, (p + 1) * PAIR)
            ic = _pair_col(gt, GATE_ML_I + d * ML_HEADS + 2 * p)
            bc = _pair_col(bcum_all, GATE_ML_F + d * ML_HEADS + 2 * p)
            br = br_scr[u, c]
            ir = ir_scr[u, c]
            q_pair = _stack_pair(q_ref[pl.ds(r0, CHUNK), cols])
            k_pair = _stack_pair(k_ref[pl.ds(r0, CHUNK), cols])
            v2 = v_ref[pl.ds(r0, CHUNK), cols]
            v_ext = jnp.concatenate([jnp.where(lane < HEAD_DIM, v2, one_hi),
                                     jnp.where(lane >= HEAD_DIM, v2, one_lo)], axis=0)
            m_prev = m_scr[u]
            c_prev = c_scr[u]

            dm = jnp.where(valid, bc - br + ir, NEG)
            inter = bc + m_prev
            m_t = jnp.maximum(jnp.max(dm, axis=-1, keepdims=True), inter)
            gram = _dot_nt(q_pair, k_pair) * (HEAD_DIM ** -0.5)
            s_intra = gram * jnp.exp(dm - m_t)
            a_inter = jnp.exp(inter - m_t)
            res = a_inter * _dot(q_pair, c_prev.astype(BF16)) + _dot(s_intra.astype(BF16), v_ext)
            den0 = jnp.maximum(jnp.abs(res[:CHUNK, HEAD_DIM:HEAD_DIM + 1]), jnp.exp(-m_t[:CHUNK]))
            den1 = jnp.maximum(jnp.abs(res[CHUNK:, 0:1]), jnp.exp(-m_t[CHUNK:]))
            h_ref[pl.ds(r0, CHUNK), cols] = jnp.where(lane < HEAD_DIM, res[:CHUNK] / den0, res[CHUNK:] / den1)

            b_last = _pair_last(bc, rev)
            w = b_last - bc + ic
            m_loc = jnp.concatenate(
                [jnp.broadcast_to(jnp.max(w[:CHUNK], axis=0, keepdims=True), (CHUNK, 1)),
                 jnp.broadcast_to(jnp.max(w[CHUNK:], axis=0, keepdims=True), (CHUNK, 1))], axis=0)
            m_new = jnp.maximum(b_last + m_prev, m_loc)
            a = jnp.exp(b_last + m_prev - m_new)
            g = jnp.exp(m_loc - m_new)
            kw = (k_pair.astype(F32) * ((HEAD_DIM ** -0.5) * jnp.exp(w - m_loc))).astype(BF16)
            c_scr[u] = a * c_prev + g * _dot_tn(kw, v_ext)
            m_scr[u] = m_new

    def body(i, carry):
        direction(0, i, qf_ref, kf_ref, vf_ref, gf_ref, hf_ref)
        direction(1, chunks - 1 - i, qb_ref, kb_ref, vb_ref, gb_ref, hb_ref)
        return carry

    lax.fori_loop(0, chunks, body, 0)


def _mlstm(q, k, v, gates, i_bias, f_bias):
    b, s, _ = q.shape
    tb = min(SEQ_BLOCK, s)
    nb = s // tb
    chunks = tb // CHUNK
    i_rows = _gate_rows(gates, GATE_ML_I, ML_HEADS)
    f_rows = _gate_rows(gates, GATE_ML_F, ML_HEADS)
    bias_vec = _lane_vector([(GATE_ML_I, i_bias), (GATE_ML_F, f_bias)])
    ibr = _param_rows(i_bias, ML_HEADS)
    fbr = _param_rows(f_bias, ML_HEADS)

    def fwd(bi, j):
        return (bi, j, 0)

    def bwd(bi, j):
        return (bi, nb - 1 - j, 0)

    seq = lambda w, im: pl.BlockSpec((None, tb, w), im)
    rows_f = pl.BlockSpec((None, None, ML_PAIRS, chunks, PAIR), lambda bi, j: (bi, 0, 0, j, 0))
    rows_b = pl.BlockSpec((None, None, ML_PAIRS, chunks, PAIR), lambda bi, j: (bi, 1, 0, nb - 1 - j, 0))
    full = lambda a: pl.BlockSpec(a.shape, lambda bi, j: (0,) * a.ndim)
    units = N_DIR * ML_PAIRS
    out = jax.ShapeDtypeStruct((b, s, ML_WIDTH), F32)
    return pl.pallas_call(
        functools.partial(_mlstm_kernel, chunks=chunks),
        out_shape=(out, out),
        grid=(b, nb),
        in_specs=[seq(ML_WIDTH, fwd), seq(ML_WIDTH, fwd), seq(ML_WIDTH, fwd), seq(GATE_LANES, fwd), rows_f, rows_f,
                  seq(ML_WIDTH, bwd), seq(ML_WIDTH, bwd), seq(ML_WIDTH, bwd), seq(GATE_LANES, bwd), rows_b, rows_b,
                  full(bias_vec), full(ibr), full(fbr)],
        out_specs=(seq(ML_WIDTH, fwd), seq(ML_WIDTH, bwd)),
        scratch_shapes=[pltpu.VMEM((units, PAIR, PAIR), F32),
                        pltpu.VMEM((units, PAIR, 1), F32),
                        pltpu.VMEM((units, chunks, 1, PAIR), F32),
                        pltpu.VMEM((units, chunks, 1, PAIR), F32)],
        compiler_params=_params("parallel", "arbitrary"),
        name="mlstm",
    )(q, k, v, gates, i_rows, f_rows, q, k, v, gates, i_rows, f_rows, bias_vec, ibr, fbr)


GD_PAIRS = GDN_HEADS // 2
CONV_HALO = 8


def _gdn_prep_kernel(x_ref, prev_ref, next_ref, w_ref, q_ref, k_ref, v_ref, ext_scr, *, tb):
    j = pl.program_id(1)
    nb = pl.num_programs(1)
    ext_scr[0:CONV_HALO, :] = jnp.where(j > 0, prev_ref[...], 0.0)
    ext_scr[CONV_HALO:CONV_HALO + tb, :] = x_ref[...]
    ext_scr[CONV_HALO + tb:, :] = jnp.where(j < nb - 1, next_ref[...], 0.0)
    base = CONV_HALO - CONV_K // 2
    y = jnp.zeros((tb, 3 * GDN_WIDTH), F32)
    for t in range(CONV_K):
        y = y + ext_scr[base + t:base + t + tb, :] * w_ref[t:t + 1, :]
    y = y * _sigmoid(y)
    qc, kc = y[:, :GDN_WIDTH], y[:, GDN_WIDTH:2 * GDN_WIDTH]
    q_ref[...] = qc * lax.rsqrt(_head_sumsq(qc) + EPS)
    k_ref[...] = kc * lax.rsqrt(_head_sumsq(kc) + EPS)
    v_ref[...] = y[:, 2 * GDN_WIDTH:]


def _gdn_prep(qkv, conv_w):
    b, s, w = qkv.shape
    tb = min(SEQ_BLOCK, s)
    nb = s // tb
    hb = tb // CONV_HALO
    last = s // CONV_HALO - 1
    out = jax.ShapeDtypeStruct((b, s, GDN_WIDTH), F32)
    return pl.pallas_call(
        functools.partial(_gdn_prep_kernel, tb=tb),
        out_shape=(out, out, out),
        grid=(b, nb),
        in_specs=[pl.BlockSpec((None, tb, w), lambda bi, j: (bi, j, 0)),
                  pl.BlockSpec((None, CONV_HALO, w), lambda bi, j: (bi, jnp.maximum(j * hb - 1, 0), 0)),
                  pl.BlockSpec((None, CONV_HALO, w), lambda bi, j: (bi, jnp.minimum((j + 1) * hb, last), 0)),
                  pl.BlockSpec((CONV_K, w), lambda bi, j: (0, 0))],
        out_specs=tuple(pl.BlockSpec((None, tb, GDN_WIDTH), lambda bi, j: (bi, j, 0)) for _ in range(3)),
        scratch_shapes=[pltpu.VMEM((tb + 2 * CONV_HALO, w), F32)],
        compiler_params=_params("parallel", "arbitrary"),
        name="gdn_prep",
    )(qkv, qkv, qkv, conv_w.astype(F32))


def _gdn_kernel(qf_ref, kf_ref, vf_ref, gf_ref, af_ref,
                qb_ref, kb_ref, vb_ref, gb_ref, ab_ref,
                bias_ref, alog_ref, alr_ref, dtr_ref,
                of_ref, ob_ref,
                s_scr, gr_scr, *, chunks):
    j = pl.program_id(1)

    @pl.when(j == 0)
    def _():
        s_scr[...] = jnp.zeros_like(s_scr)

    bias = bias_ref[...]
    decay = -jnp.exp(alog_ref[...])
    for d, a_ref in enumerate((af_ref, ab_ref)):
        ucum = _pair_cumsum_matrix(rev=bool(d))
        for p in range(GD_PAIRS):
            g = -jnp.exp(alr_ref[d, p]) * _softplus(a_ref[p] + dtr_ref[d, p])
            gc = _exact_right(g, ucum)
            for c in range(chunks):
                gr_scr[d * GD_PAIRS + p, c] = gc[c:c + 1, :]

    r = lax.broadcasted_iota(jnp.int32, (PAIR, PAIR), 0)
    cidx = lax.broadcasted_iota(jnp.int32, (PAIR, PAIR), 1)
    eye = jnp.where(r == cidx, 1.0, 0.0).astype(F32)

    def direction(d, c, q_ref, k_ref, v_ref, g_ref, o_ref):
        rev = bool(d)
        valid, strict = _pair_masks(rev)
        r0 = pl.multiple_of(c * CHUNK, CHUNK)
        gt = g_ref[pl.ds(r0, CHUNK), :]
        beta_all = _sigmoid(gt)
        g_all = decay * _softplus(gt + bias)
        gc_all = _exact_left(_chunk_tri(rev), g_all)
        for p in range(GD_PAIRS):
            u = d * GD_PAIRS + p
            cols = slice(p * PAIR, (p + 1) * PAIR)
            beta = _pair_col(beta_all, GATE_GD_B + d * GDN_HEADS + 2 * p)
            gc = _pair_col(gc_all, GATE_GD_A + d * GDN_HEADS + 2 * p)
            gr = gr_scr[u, c]
            q_pair = _stack_pair(q_ref[pl.ds(r0, CHUNK), cols])
            k_pair = _stack_pair(k_ref[pl.ds(r0, CHUNK), cols])
            v_pair = _stack_pair(v_ref[pl.ds(r0, CHUNK), cols])
            k_b = k_pair.astype(BF16)

            grams = _dot_nt(jnp.concatenate([q_pair.astype(BF16), k_b], axis=0), k_b)
            decay_ts = jnp.exp(jnp.where(valid, gc - gr, NEG))
            attn = grams[:PAIR] * (HEAD_DIM ** -0.5) * decay_ts
            neg_a = jnp.where(strict, -(grams[PAIR:] * beta * decay_ts), 0.0)

            t_inv = eye + neg_a
            n_b = neg_a.astype(BF16)
            pw = _dot(n_b, n_b)
            for _ in range(4):
                pw_b = pw.astype(BF16)
                both = _dot(jnp.concatenate([t_inv.astype(BF16), pw_b], axis=0), pw_b)
                t_inv = t_inv + both[:PAIR]
                pw = both[PAIR:]
            t_inv = t_inv + _dot(t_inv.astype(BF16), pw.astype(BF16))

            egc = jnp.exp(gc)
            rhs = jnp.concatenate([v_pair * beta, k_pair * (beta * egc)], axis=1).astype(BF16)
            uw = _dot(t_inv.astype(BF16), rhs)
            u_c, w_c = uw[:, :PAIR], uw[:, PAIR:]
            g_last = _pair_last(gc, rev)
            q_dec = q_pair * ((HEAD_DIM ** -0.5) * egc)
            k_state = k_pair * jnp.exp(g_last - gc)

            state = s_scr[u]
            ws = _dot(jnp.concatenate([w_c, q_dec], axis=0).astype(BF16), state.astype(BF16))
            v_new = u_c - ws[:PAIR]
            o = ws[PAIR:] + _dot(attn.astype(BF16), v_new.astype(BF16))
            s_scr[u] = state * jnp.exp(g_last) + _dot_tn(k_state.astype(BF16), v_new.astype(BF16))
            o_ref[pl.ds(r0, CHUNK), cols] = o[:CHUNK] + o[CHUNK:]

    def body(i, carry):
        direction(0, i, qf_ref, kf_ref, vf_ref, gf_ref, of_ref)
        direction(1, chunks - 1 - i, qb_ref, kb_ref, vb_ref, gb_ref, ob_ref)
        return carry

    lax.fori_loop(0, chunks, body, 0)


def _gdn(q, k, v, gates, a_log, dt_bias):
    b, s, _ = q.shape
    tb = min(SEQ_BLOCK, s)
    nb = s // tb
    chunks = tb // CHUNK
    a_rows = _gate_rows(gates, GATE_GD_A, GDN_HEADS)
    bias_vec = _lane_vector([(GATE_GD_A, dt_bias)])
    alog_vec = _lane_vector([(GATE_GD_A, a_log)])
    alr = _param_rows(a_log, GDN_HEADS)
    dtr = _param_rows(dt_bias, GDN_HEADS)

    def fwd(bi, j):
        return (bi, j, 0)

    def bwd(bi, j):
        return (bi, nb - 1 - j, 0)

    seq = lambda w, im: pl.BlockSpec((None, tb, w), im)
    rows_f = pl.BlockSpec((None, None, GD_PAIRS, chunks, PAIR), lambda bi, j: (bi, 0, 0, j, 0))
    rows_b = pl.BlockSpec((None, None, GD_PAIRS, chunks, PAIR), lambda bi, j: (bi, 1, 0, nb - 1 - j, 0))
    full = lambda a: pl.BlockSpec(a.shape, lambda bi, j: (0,) * a.ndim)
    units = N_DIR * GD_PAIRS
    out = jax.ShapeDtypeStruct((b, s, GDN_WIDTH), F32)
    return pl.pallas_call(
        functools.partial(_gdn_kernel, chunks=chunks),
        out_shape=(out, out),
        grid=(b, nb),
        in_specs=[seq(GDN_WIDTH, fwd), seq(GDN_WIDTH, fwd), seq(GDN_WIDTH, fwd), seq(GATE_LANES, fwd), rows_f,
                  seq(GDN_WIDTH, bwd), seq(GDN_WIDTH, bwd), seq(GDN_WIDTH, bwd), seq(GATE_LANES, bwd), rows_b,
                  full(bias_vec), full(alog_vec), full(alr), full(dtr)],
        out_specs=(seq(GDN_WIDTH, fwd), seq(GDN_WIDTH, bwd)),
        scratch_shapes=[pltpu.VMEM((units, PAIR, PAIR), F32),
                        pltpu.VMEM((units, chunks, 1, PAIR), F32)],
        compiler_params=_params("parallel", "arbitrary"),
        name="gated_deltanet",
    )(q, k, v, gates, a_rows, q, k, v, gates, a_rows, bias_vec, alog_vec, alr, dtr)


def _out_proj_kernel(x_ref, na_ref, hf_ref, hb_ref, mo_ref, mw_ref, of_ref, ob_ref, gz_ref, gw_ref,
                     w_ref, o_ref):
    hs = hf_ref[...] + hb_ref[...]
    y_ml = hs * lax.rsqrt(_head_sumsq(hs) * (1.0 / HEAD_DIM) + EPS) * mw_ref[...] * _sigmoid(mo_ref[...])
    os_ = of_ref[...] + ob_ref[...]
    z = gz_ref[...]
    y_gd = os_ * lax.rsqrt(_head_sumsq(os_) * (1.0 / HEAD_DIM) + EPS) * gw_ref[...] * (z * _sigmoid(z))
    acc = x_ref[...] + _dot(na_ref[...], w_ref[0:NA_WIDTH, :])
    acc = acc + _dot(y_ml.astype(BF16), w_ref[NA_WIDTH:NA_WIDTH + ML_WIDTH, :])
    acc = acc + _dot(y_gd.astype(BF16), w_ref[NA_WIDTH + ML_WIDTH:, :])
    o_ref[...] = acc


def _out_proj(x2d, y_na, hf, hb, ml_o, ml_norm_w, of, ob, gd_z, gdn_norm_w, w_out):
    t = x2d.shape[0]
    tm = min(TM_ROWS, t)
    row = lambda w: pl.BlockSpec((tm, w), lambda i: (i, 0))
    const = lambda r, c: pl.BlockSpec((r, c), lambda i: (0, 0))
    return pl.pallas_call(
        _out_proj_kernel,
        out_shape=jax.ShapeDtypeStruct((t, D_MODEL), F32),
        grid=(t // tm,),
        in_specs=[row(D_MODEL), row(NA_WIDTH), row(ML_WIDTH), row(ML_WIDTH), row(ML_WIDTH), const(1, ML_WIDTH),
                  row(GDN_WIDTH), row(GDN_WIDTH), row(GDN_WIDTH), const(1, GDN_WIDTH),
                  const(D_MODEL, D_MODEL)],
        out_specs=row(D_MODEL),
        compiler_params=_params("parallel"),
        name="out_proj",
    )(x2d, y_na, hf, hb, ml_o, ml_norm_w.reshape(1, ML_WIDTH).astype(F32), of, ob, gd_z,
      gdn_norm_w.reshape(1, GDN_WIDTH).astype(F32), w_out.astype(BF16))


def _mem_kv_kernel(m_ref, nw_ref, w_ref, k_ref, v_ref):
    h = _rms(m_ref[...], nw_ref[...]).astype(BF16)
    k_ref[...] = _dot(h, w_ref[:, :D_MODEL]).astype(BF16)
    v_ref[...] = _dot(h, w_ref[:, D_MODEL:]).astype(BF16)


def _mem_kv(mem2d, norm_w, w_kv):
    t = mem2d.shape[0]
    tm = min(TM_ROWS, t)
    out = jax.ShapeDtypeStruct((t, D_MODEL), BF16)
    return pl.pallas_call(
        _mem_kv_kernel,
        out_shape=(out, out),
        grid=(t // tm,),
        in_specs=[pl.BlockSpec((tm, D_MODEL), lambda i: (i, 0)),
                  pl.BlockSpec((1, D_MODEL), lambda i: (0, 0)),
                  pl.BlockSpec((D_MODEL, 2 * D_MODEL), lambda i: (0, 0))],
        out_specs=(pl.BlockSpec((tm, D_MODEL), lambda i: (i, 0)), pl.BlockSpec((tm, D_MODEL), lambda i: (i, 0))),
        compiler_params=_params("parallel"),
        name="mem_kv",
    )(mem2d, norm_w.reshape(1, D_MODEL).astype(F32), w_kv.astype(BF16))


def _xattn_kernel(x_ref, nw_ref, wq_ref, k_ref, v_ref, wo_ref, o_ref):
    x = x_ref[...]
    q = _dot(_rms(x, nw_ref[...]).astype(BF16), wq_ref[...]).astype(BF16)
    acc = x
    for h in range(XA_HEADS):
        cols = slice(h * XA_HEAD_DIM, (h + 1) * XA_HEAD_DIM)
        s = _dot_nt(q[:, cols], k_ref[:, cols]) * (XA_HEAD_DIM ** -0.5)
        m = jnp.max(s, axis=-1, keepdims=True)
        e = jnp.exp(s - m)
        l = jnp.sum(e, axis=-1, keepdims=True)
        o_h = _dot(e.astype(BF16), v_ref[:, cols]) / l
        acc = acc + _dot(o_h.astype(BF16), wo_ref[cols, :])
    o_ref[...] = acc


def _xattn(x3d, norm_w, w_q, k, v, w_o):
    b, s, _ = x3d.shape
    tm = min(TM_ROWS, s)
    n_mem = k.shape[1]
    return pl.pallas_call(
        _xattn_kernel,
        out_shape=jax.ShapeDtypeStruct(x3d.shape, F32),
        grid=(b, s // tm),
        in_specs=[pl.BlockSpec((None, tm, D_MODEL), lambda bi, i: (bi, i, 0)),
                  pl.BlockSpec((1, D_MODEL), lambda bi, i: (0, 0)),
                  pl.BlockSpec((D_MODEL, D_MODEL), lambda bi, i: (0, 0)),
                  pl.BlockSpec((None, n_mem, D_MODEL), lambda bi, i: (bi, 0, 0)),
                  pl.BlockSpec((None, n_mem, D_MODEL), lambda bi, i: (bi, 0, 0)),
                  pl.BlockSpec((D_MODEL, D_MODEL), lambda bi, i: (0, 0))],
        out_specs=pl.BlockSpec((None, tm, D_MODEL), lambda bi, i: (bi, i, 0)),
        compiler_params=_params("parallel", "parallel"),
        name="cross_attention",
    )(x3d, norm_w.reshape(1, D_MODEL).astype(F32), w_q.astype(BF16), k, v, w_o.astype(BF16))


FF_CHUNK = 512


def _ffn_kernel(x_ref, nw_ref, w1_ref, w2_ref, fw_ref, o_ref, *, final_norm):
    x = x_ref[...]
    h = _rms(x, nw_ref[...]).astype(BF16)
    acc = x
    for c0 in range(0, D_FF, FF_CHUNK):
        a = jnp.maximum(_dot(h, w1_ref[:, c0:c0 + FF_CHUNK]), 0.0)
        acc = acc + _dot((a * a).astype(BF16), w2_ref[c0:c0 + FF_CHUNK, :])
    o_ref[...] = _rms(acc, fw_ref[...]) if final_norm else acc


def _ffn(x2d, norm_w, w1, w2, final_w, final_norm):
    t = x2d.shape[0]
    tm = min(TM_ROWS, t)
    return pl.pallas_call(
        functools.partial(_ffn_kernel, final_norm=final_norm),
        out_shape=jax.ShapeDtypeStruct((t, D_MODEL), F32),
        grid=(t // tm,),
        in_specs=[pl.BlockSpec((tm, D_MODEL), lambda i: (i, 0)),
                  pl.BlockSpec((1, D_MODEL), lambda i: (0, 0)),
                  pl.BlockSpec((D_MODEL, D_FF), lambda i: (0, 0), pipeline_mode=pl.Buffered(1)),
                  pl.BlockSpec((D_FF, D_MODEL), lambda i: (0, 0), pipeline_mode=pl.Buffered(1)),
                  pl.BlockSpec((1, D_MODEL), lambda i: (0, 0))],
        out_specs=pl.BlockSpec((tm, D_MODEL), lambda i: (i, 0)),
        compiler_params=_params("parallel"),
        name="ffn",
    )(x2d, norm_w.reshape(1, D_MODEL).astype(F32), w1.astype(BF16), w2.astype(BF16),
      final_w.reshape(1, D_MODEL).astype(F32))


def kernel(x, mem, norm_mix_w, w_in, na_rel_bias, ml_i_bias, ml_f_bias, ml_norm_w, gdn_conv_w, gdn_a_log,
           gdn_dt_bias, gdn_norm_w, w_out, norm_xa_w, norm_mem_w, w_xq, w_xkv, w_xo, norm_ffn_w, w_ff1,
           w_ff2, norm_out_w):
    b, s, d = x.shape
    depth = w_in.shape[0]
    x2d = x.reshape(b * s, d).astype(F32)
    mem2d = mem.reshape(-1, d).astype(F32)
    for l in range(depth):
        (na_q, na_k, na_v, ml_q, ml_k, ml_v, ml_o, gd_qkv, gd_z, gates) = _in_proj(
            x2d, norm_mix_w[l].astype(F32), _permute_w_in(w_in[l]))
        seq = lambda a: a.reshape(b, s, a.shape[-1])
        gates3 = seq(gates)
        y_na = _neighbourhood_attention(seq(na_q), seq(na_k), seq(na_v), _na_bias_table(na_rel_bias[l]))
        hf, hb = _mlstm(seq(ml_q), seq(ml_k), seq(ml_v), gates3, ml_i_bias[l], ml_f_bias[l])
        gq, gk, gv = _gdn_prep(seq(gd_qkv), gdn_conv_w[l])
        of, ob = _gdn(gq, gk, gv, gates3, gdn_a_log[l], gdn_dt_bias[l])
        flat = lambda a: a.reshape(b * s, a.shape[-1])
        x2d = _out_proj(x2d, flat(y_na), flat(hf), flat(hb), ml_o, ml_norm_w[l], flat(of), flat(ob), gd_z,
                        gdn_norm_w[l], w_out[l])
        mk, mv = _mem_kv(mem2d, norm_mem_w[l], w_xkv[l])
        n_mem = mem.shape[1]
        x2d = _xattn(x2d.reshape(b, s, d), norm_xa_w[l], w_xq[l], mk.reshape(b, n_mem, d),
                     mv.reshape(b, n_mem, d), w_xo[l]).reshape(b * s, d)
        x2d = _ffn(x2d, norm_ffn_w[l], w_ff1[l], w_ff2[l], norm_out_w, final_norm=(l == depth - 1))
    return x2d.reshape(b, s, d).astype(x.dtype)
```

```python
import functools

import numpy as np
import jax
import jax.numpy as jnp
from jax import lax
from jax.experimental import pallas as pl
from jax.experimental.pallas import tpu as pltpu

F32 = jnp.float32
BF16 = jnp.bfloat16

D_MODEL = 1024
HEAD_DIM = 64
GRID_W = 64
NA_HEADS = 6
NA_WIN_ROWS = 8
NA_WIN_COLS = 16
ML_HEADS = 4
GDN_HEADS = 6
CONV_K = 5
N_DIR = 2
XA_HEADS = 4
XA_HEAD_DIM = D_MODEL // XA_HEADS
D_FF = 4 * D_MODEL
NA_WIDTH = NA_HEADS * HEAD_DIM
ML_WIDTH = ML_HEADS * HEAD_DIM
GDN_WIDTH = GDN_HEADS * HEAD_DIM
EPS = 1e-6
CHUNK = 64
PAIR = 2 * HEAD_DIM
NEG = -1e30

IN_SIZES = (NA_WIDTH, NA_WIDTH, NA_WIDTH,
            ML_WIDTH, ML_WIDTH, ML_WIDTH, ML_WIDTH, N_DIR * ML_HEADS, N_DIR * ML_HEADS,
            GDN_WIDTH, GDN_WIDTH, GDN_WIDTH, GDN_WIDTH, N_DIR * GDN_HEADS, N_DIR * GDN_HEADS)

GATE_ML_I = 0
GATE_ML_F = GATE_ML_I + N_DIR * ML_HEADS
GATE_GD_B = GATE_ML_F + N_DIR * ML_HEADS
GATE_GD_A = GATE_GD_B + N_DIR * GDN_HEADS
GATE_USED = GATE_GD_A + N_DIR * GDN_HEADS
GATE_LANES = 128

V7X_VMEM_LIMIT = 56 * 1024 * 1024
MXU_COLS = 256

TM_ROWS = 512
NA_ROWS_PER_STEP = 8
SEQ_BLOCK = 1024
GDN_SEQ_BLOCK = 512


def _params(*sem):
    return pltpu.CompilerParams(dimension_semantics=sem, vmem_limit_bytes=V7X_VMEM_LIMIT)


def _dot(a, b):
    return jnp.dot(a, b, preferred_element_type=F32)


def _dot_nt(a, b):
    return lax.dot_general(a, b, (((1,), (1,)), ((), ())), preferred_element_type=F32)


def _dot_tn(a, b):
    return lax.dot_general(a, b, (((0,), (0,)), ((), ())), preferred_element_type=F32)


def _split3(x):
    x1 = x.astype(BF16)
    r1 = x - x1.astype(F32)
    x2 = r1.astype(BF16)
    x3 = (r1 - x2.astype(F32)).astype(BF16)
    return x1, x2, x3


def _exact_left(m01, x):
    x1, x2, x3 = _split3(x)
    return _dot(m01, x1) + _dot(m01, x2) + _dot(m01, x3)


def _exact_right(x, m01):
    x1, x2, x3 = _split3(x)
    return _dot(x1, m01) + _dot(x2, m01) + _dot(x3, m01)


def _rms(x, w):
    ms = jnp.mean(x * x, axis=-1, keepdims=True)
    return x * lax.rsqrt(ms + EPS) * w


def _softplus(x):
    return jnp.maximum(x, 0.0) + jnp.log1p(jnp.exp(-jnp.abs(x)))


def _sigmoid(x):
    return 1.0 / (1.0 + jnp.exp(-x))


def _segment_mean_matrix(width):
    r = lax.broadcasted_iota(jnp.int32, (width, width), 0) // HEAD_DIM
    c = lax.broadcasted_iota(jnp.int32, (width, width), 1) // HEAD_DIM
    return jnp.where(r == c, 1.0, 0.0).astype(BF16)


def _head_sumsq(t):
    return _exact_right(t * t, _segment_mean_matrix(t.shape[-1]))


def _pair_masks(rev):
    r = lax.broadcasted_iota(jnp.int32, (PAIR, PAIR), 0)
    c = lax.broadcasted_iota(jnp.int32, (PAIR, PAIR), 1)
    same = (r // HEAD_DIM) == (c // HEAD_DIM)
    t, s = r % HEAD_DIM, c % HEAD_DIM
    if rev:
        return same & (s >= t), same & (s > t)
    return same & (s <= t), same & (s < t)


def _chunk_tri(rev):
    t = lax.broadcasted_iota(jnp.int32, (CHUNK, CHUNK), 0)
    s = lax.broadcasted_iota(jnp.int32, (CHUNK, CHUNK), 1)
    return jnp.where((s >= t) if rev else (s <= t), 1.0, 0.0).astype(BF16)


def _stack_pair(x2):
    lane = lax.broadcasted_iota(jnp.int32, x2.shape, 1)
    zero = jnp.zeros_like(x2)
    return jnp.concatenate([jnp.where(lane < HEAD_DIM, x2, zero),
                            jnp.where(lane >= HEAD_DIM, x2, zero)], axis=0)


def _pair_col(tile, lane0):
    return jnp.concatenate([tile[:, lane0:lane0 + 1], tile[:, lane0 + 1:lane0 + 2]], axis=0)


def _pair_last(col, rev):
    i0 = 0 if rev else CHUNK - 1
    a = jnp.broadcast_to(col[i0:i0 + 1, :], (CHUNK, 1))
    b = jnp.broadcast_to(col[CHUNK + i0:CHUNK + i0 + 1, :], (CHUNK, 1))
    return jnp.concatenate([a, b], axis=0)


IN_SEGMENTS = (
    (NA_WIDTH, BF16), (NA_WIDTH, BF16), (NA_WIDTH, BF16),
    (ML_WIDTH, BF16), (ML_WIDTH, BF16), (ML_WIDTH, BF16), (ML_WIDTH, F32),
    (3 * GDN_WIDTH, F32), (GDN_WIDTH, F32), (GATE_LANES, F32))
IN_COLS = sum(w for w, _ in IN_SEGMENTS)
IN_DOT_COLS = 3 * MXU_COLS


def _in_proj_kernel(x_ref, nw_ref, w_ref, *out_refs):
    h = _rms(x_ref[...], nw_ref[...]).astype(BF16)
    starts = np.cumsum([0] + [w for w, _ in IN_SEGMENTS])
    for c0 in range(0, IN_COLS, IN_DOT_COLS):
        c1 = min(c0 + IN_DOT_COLS, IN_COLS)
        acc = _dot(h, w_ref[:, c0:c1])
        for o_ref, s0, s1 in zip(out_refs, starts[:-1], starts[1:]):
            a, b = max(c0, int(s0)), min(c1, int(s1))
            if a < b:
                o_ref[:, a - int(s0):b - int(s0)] = acc[:, a - c0:b - c0].astype(o_ref.dtype)


def _in_proj(x2d, norm_w, w_perm):
    t = x2d.shape[0]
    tm = min(TM_ROWS, t)
    outs = tuple(jax.ShapeDtypeStruct((t, w), dt) for w, dt in IN_SEGMENTS)
    return pl.pallas_call(
        _in_proj_kernel,
        out_shape=outs,
        grid=(t // tm,),
        in_specs=[pl.BlockSpec((tm, D_MODEL), lambda i: (i, 0)),
                  pl.BlockSpec((1, D_MODEL), lambda i: (0, 0)),
                  pl.BlockSpec((D_MODEL, IN_COLS), lambda i: (0, 0))],
        out_specs=tuple(pl.BlockSpec((tm, w), lambda i: (i, 0)) for w, _ in IN_SEGMENTS),
        compiler_params=_params("parallel"),
        name="in_proj",
    )(x2d, norm_w.reshape(1, D_MODEL), w_perm)


def _permute_w_in(w):
    parts = jnp.split(w, np.cumsum(IN_SIZES)[:-1], axis=-1)
    (na_q, na_k, na_v, ml_q, ml_k, ml_v, ml_o, ml_i, ml_f, gd_q, gd_k, gd_v, gd_z, gd_b, gd_a) = parts
    pad = jnp.zeros((w.shape[0], GATE_LANES - GATE_USED), w.dtype)
    return jnp.concatenate([na_q, na_k, na_v, ml_q, ml_k, ml_v, ml_o, gd_q, gd_k, gd_v, gd_z,
                            ml_i, ml_f, gd_b, gd_a, pad], axis=-1).astype(BF16)


NA_BIAS_ROWS = 96


def _na_bias_kernel(rb_ref, o_ref):
    n = GRID_W * GRID_W
    dc = lax.broadcasted_iota(jnp.int32, (GATE_LANES, n), 0)
    col = lax.broadcasted_iota(jnp.int32, (GATE_LANES, n), 1)
    q, kc = col // GRID_W, col % GRID_W
    c0 = jnp.clip(q - NA_WIN_COLS // 2, 0, GRID_W - NA_WIN_COLS)
    valid = (kc >= c0) & (kc < c0 + NA_WIN_COLS)
    onehot = jnp.where(valid & (kc - q + (NA_WIN_COLS - 1) == dc), 1.0, 0.0).astype(BF16)
    o_ref[...] = jnp.where(valid[0:1, :], _exact_right(rb_ref[...], onehot), NEG)


def _na_bias_table(rel_bias):
    wr = NA_WIN_ROWS
    nh, ndr, ndc = rel_bias.shape
    rb = jnp.zeros((NA_BIAS_ROWS, GATE_LANES), F32).at[:nh * ndr, :ndc].set(
        rel_bias.astype(F32).reshape(nh * ndr, ndc))
    band = pl.pallas_call(
        _na_bias_kernel,
        out_shape=jax.ShapeDtypeStruct((NA_BIAS_ROWS, GRID_W * GRID_W), F32),
        name="na_bias_expand",
    )(rb)
    band = band[:nh * ndr].reshape(nh, ndr, GRID_W, GRID_W)
    tab = jnp.stack([band[:, wr - 1 - var:2 * wr - 1 - var] for var in range(wr)], axis=1)
    tab = jnp.transpose(tab, (0, 1, 3, 2, 4))
    tab = tab.reshape(NA_HEADS // 2, 2, wr, GRID_W, wr * GRID_W)
    return jnp.moveaxis(tab, 1, 2).reshape(NA_HEADS // 2, wr, 2 * GRID_W, wr * GRID_W)


def _na_kernel(q_ref, k_ref, v_ref, bias_ref, o_ref, *, rows, rows_per_step):
    j = pl.program_id(1)
    nkeys = NA_WIN_ROWS * GRID_W
    lane = lax.broadcasted_iota(jnp.int32, (GRID_W, PAIR), 1)

    def body(rr, carry):
        r = j * rows_per_step + rr
        r0 = jnp.clip(r - NA_WIN_ROWS // 2, 0, rows - NA_WIN_ROWS)
        variant = r - r0
        qoff = pl.multiple_of(rr * GRID_W, GRID_W)
        koff = pl.multiple_of(r0 * GRID_W, GRID_W)
        for p in range(NA_HEADS // 2):
            cols = slice(p * PAIR, (p + 1) * PAIR)
            q_pair = _stack_pair(q_ref[pl.ds(qoff, GRID_W), cols])
            k2 = k_ref[pl.ds(koff, nkeys), cols]
            v2 = v_ref[pl.ds(koff, nkeys), cols]
            s = _dot_nt(q_pair, k2) * (HEAD_DIM ** -0.5) + bias_ref[p, variant]
            m = jnp.max(s, axis=-1, keepdims=True)
            e = jnp.exp(s - m)
            l = jnp.sum(e, axis=-1, keepdims=True)
            o = _dot(e.astype(BF16), v2) / l
            o_ref[pl.ds(qoff, GRID_W), cols] = jnp.where(
                lane < HEAD_DIM, o[:GRID_W], o[GRID_W:]).astype(o_ref.dtype)
        return carry

    lax.fori_loop(0, rows_per_step, body, 0)


def _neighbourhood_attention(q, k, v, bias_tab):
    b, s, _ = q.shape
    rows = s // GRID_W
    assert rows >= NA_WIN_ROWS
    rps = min(NA_ROWS_PER_STEP, rows)
    tq = rps * GRID_W
    return pl.pallas_call(
        functools.partial(_na_kernel, rows=rows, rows_per_step=rps),
        out_shape=jax.ShapeDtypeStruct((b, s, NA_WIDTH), BF16),
        grid=(b, rows // rps),
        in_specs=[pl.BlockSpec((None, tq, NA_WIDTH), lambda bi, j: (bi, j, 0)),
                  pl.BlockSpec((None, s, NA_WIDTH), lambda bi, j: (bi, 0, 0)),
                  pl.BlockSpec((None, s, NA_WIDTH), lambda bi, j: (bi, 0, 0)),
                  pl.BlockSpec(bias_tab.shape, lambda bi, j: (0, 0, 0, 0))],
        out_specs=pl.BlockSpec((None, tq, NA_WIDTH), lambda bi, j: (bi, j, 0)),
        compiler_params=_params("parallel", "arbitrary"),
        name="neighbourhood_attention",
    )(q, k, v, bias_tab)


def _gate_rows(gates, lane0, heads):
    b, s, _ = gates.shape
    g = gates[:, :, lane0:lane0 + N_DIR * heads].reshape(b, s // CHUNK, CHUNK, N_DIR, heads // 2, 2)
    return jnp.transpose(g, (0, 3, 4, 1, 5, 2)).reshape(b, N_DIR, heads // 2, s // CHUNK, PAIR)


def _param_rows(p, heads):
    return jnp.repeat(p.astype(F32).reshape(N_DIR, heads // 2, 2), HEAD_DIM, axis=-1).reshape(
        N_DIR, heads // 2, 1, PAIR)


def _lane_vector(entries):
    v = jnp.zeros((GATE_LANES,), F32)
    for lane0, vals in entries:
        v = lax.dynamic_update_slice(v, vals.astype(F32).reshape(-1), (lane0,))
    return v.reshape(1, GATE_LANES)


def _pair_cumsum_matrix(rev):
    valid, _ = _pair_masks(rev)
    r = lax.broadcasted_iota(jnp.int32, (PAIR, PAIR), 0)
    c = lax.broadcasted_iota(jnp.int32, (PAIR, PAIR), 1)
    same = (r // HEAD_DIM) == (c // HEAD_DIM)
    sp, s = r % HEAD_DIM, c % HEAD_DIM
    del valid
    return jnp.where(same & ((sp >= s) if rev else (sp <= s)), 1.0, 0.0).astype(BF16)


ML_PAIRS = ML_HEADS // 2


def _mlstm_kernel(qf_ref, kf_ref, vf_ref, gf_ref, if_ref, ff_ref,
                  qb_ref, kb_ref, vb_ref, gb_ref, ib_ref, fb_ref,
                  bias_ref, ibr_ref, fbr_ref,
                  hf_ref, hb_ref,
                  c_scr, m_scr, br_scr, ir_scr, *, chunks):
    j = pl.program_id(1)

    @pl.when(j == 0)
    def _():
        c_scr[...] = jnp.zeros_like(c_scr)
        m_scr[...] = jnp.zeros_like(m_scr)

    bias = bias_ref[...]
    for d, (i_ref, f_ref) in enumerate(((if_ref, ff_ref), (ib_ref, fb_ref))):
        ucum = _pair_cumsum_matrix(rev=bool(d))
        for p in range(ML_PAIRS):
            logf = -_softplus(-(f_ref[p] + fbr_ref[d, p]))
            bcum = _exact_right(logf, ucum)
            ig = i_ref[p] + ibr_ref[d, p]
            for c in range(chunks):
                br_scr[d * ML_PAIRS + p, c] = bcum[c:c + 1, :]
                ir_scr[d * ML_PAIRS + p, c] = ig[c:c + 1, :]

    lane = lax.broadcasted_iota(jnp.int32, (CHUNK, PAIR), 1)
    one_hi = jnp.where(lane == HEAD_DIM, 1.0, 0.0).astype(BF16)
    one_lo = jnp.where(lane == 0, 1.0, 0.0).astype(BF16)

    def direction(d, c, q_ref, k_ref, v_ref, g_ref, h_ref):
        rev = bool(d)
        valid, _ = _pair_masks(rev)
        r0 = pl.multiple_of(c * CHUNK, CHUNK)
        gt = g_ref[pl.ds(r0, CHUNK), :] + bias
        logf = -_softplus(-gt)
        bcum_all = _exact_left(_chunk_tri(rev), logf)
        for p in range(ML_PAIRS):
            u = d * ML_PAIRS + p
            cols = slice(p * PAIR, (p + 1) * PAIR)
            ic = _pair_col(gt, GATE_ML_I + d * ML_HEADS + 2 * p)
            bc = _pair_col(bcum_all, GATE_ML_F + d * ML_HEADS + 2 * p)
            br = br_scr[u, c]
            ir = ir_scr[u, c]
            q_pair = _stack_pair(q_ref[pl.ds(r0, CHUNK), cols])
            k_pair = _stack_pair(k_ref[pl.ds(r0, CHUNK), cols])
            v2 = v_ref[pl.ds(r0, CHUNK), cols]
            v_ext = jnp.concatenate([jnp.where(lane < HEAD_DIM, v2, one_hi),
                                     jnp.where(lane >= HEAD_DIM, v2, one_lo)], axis=0)
            m_prev = m_scr[u]
            c_prev = c_scr[u]

            dm = jnp.where(valid, bc - br + ir, NEG)
            inter = bc + m_prev
            m_t = jnp.maximum(jnp.max(dm, axis=-1, keepdims=True), inter)
            gram = _dot_nt(q_pair, k_pair) * (HEAD_DIM ** -0.5)
            s_intra = gram * jnp.exp(dm - m_t)
            a_inter = jnp.exp(inter - m_t)
            qc = a_inter * _dot(q_pair, c_prev.astype(BF16))
            res = qc + _dot(s_intra.astype(BF16), v_ext)
            intra = jnp.sum(s_intra, axis=-1, keepdims=True)
            den0 = jnp.maximum(jnp.abs(qc[:CHUNK, HEAD_DIM:HEAD_DIM + 1] + intra[:CHUNK]), jnp.exp(-m_t[:CHUNK]))
            den1 = jnp.maximum(jnp.abs(qc[CHUNK:, 0:1] + intra[CHUNK:]), jnp.exp(-m_t[CHUNK:]))
            h_ref[pl.ds(r0, CHUNK), cols] = jnp.where(lane < HEAD_DIM, res[:CHUNK] / den0, res[CHUNK:] / den1)

            b_last = _pair_last(bc, rev)
            w = b_last - bc + ic
            m_loc = jnp.concatenate(
                [jnp.broadcast_to(jnp.max(w[:CHUNK], axis=0, keepdims=True), (CHUNK, 1)),
                 jnp.broadcast_to(jnp.max(w[CHUNK:], axis=0, keepdims=True), (CHUNK, 1))], axis=0)
            m_new = jnp.maximum(b_last + m_prev, m_loc)
            a = jnp.exp(b_last + m_prev - m_new)
            g = jnp.exp(m_loc - m_new)
            kw = (k_pair.astype(F32) * ((HEAD_DIM ** -0.5) * jnp.exp(w - m_loc))).astype(BF16)
            c_scr[u] = a * c_prev + g * _dot_tn(kw, v_ext)
            m_scr[u] = m_new

    def body(i, carry):
        direction(0, i, qf_ref, kf_ref, vf_ref, gf_ref, hf_ref)
        direction(1, chunks - 1 - i, qb_ref, kb_ref, vb_ref, gb_ref, hb_ref)
        return carry

    lax.fori_loop(0, chunks, body, 0)


def _mlstm(q, k, v, gates, i_bias, f_bias):
    b, s, _ = q.shape
    tb = min(SEQ_BLOCK, s)
    nb = s // tb
    chunks = tb // CHUNK
    i_rows = _gate_rows(gates, GATE_ML_I, ML_HEADS)
    f_rows = _gate_rows(gates, GATE_ML_F, ML_HEADS)
    bias_vec = _lane_vector([(GATE_ML_I, i_bias), (GATE_ML_F, f_bias)])
    ibr = _param_rows(i_bias, ML_HEADS)
    fbr = _param_rows(f_bias, ML_HEADS)

    def fwd(bi, j):
        return (bi, j, 0)

    def bwd(bi, j):
        return (bi, nb - 1 - j, 0)

    seq = lambda w, im: pl.BlockSpec((None, tb, w), im)
    rows_f = pl.BlockSpec((None, None, ML_PAIRS, chunks, PAIR), lambda bi, j: (bi, 0, 0, j, 0))
    rows_b = pl.BlockSpec((None, None, ML_PAIRS, chunks, PAIR), lambda bi, j: (bi, 1, 0, nb - 1 - j, 0))
    full = lambda a: pl.BlockSpec(a.shape, lambda bi, j: (0,) * a.ndim)
    units = N_DIR * ML_PAIRS
    out = jax.ShapeDtypeStruct((b, s, ML_WIDTH), F32)
    return pl.pallas_call(
        functools.partial(_mlstm_kernel, chunks=chunks),
        out_shape=(out, out),
        grid=(b, nb),
        in_specs=[seq(ML_WIDTH, fwd), seq(ML_WIDTH, fwd), seq(ML_WIDTH, fwd), seq(GATE_LANES, fwd), rows_f, rows_f,
                  seq(ML_WIDTH, bwd), seq(ML_WIDTH, bwd), seq(ML_WIDTH, bwd), seq(GATE_LANES, bwd), rows_b, rows_b,
                  full(bias_vec), full(ibr), full(fbr)],
        out_specs=(seq(ML_WIDTH, fwd), seq(ML_WIDTH, bwd)),
        scratch_shapes=[pltpu.VMEM((units, PAIR, PAIR), F32),
                        pltpu.VMEM((units, PAIR, 1), F32),
                        pltpu.VMEM((units, chunks, 1, PAIR), F32),
                        pltpu.VMEM((units, chunks, 1, PAIR), F32)],
        compiler_params=_params("parallel", "arbitrary"),
        name="mlstm",
    )(q, k, v, gates, i_rows, f_rows, q, k, v, gates, i_rows, f_rows, bias_vec, ibr, fbr)


GD_PAIRS = GDN_HEADS // 2
CONV_HALO = 8


def _gdn_prep_kernel(x_ref, prev_ref, next_ref, w_ref, q_ref, k_ref, v_ref, ext_scr, *, tb):
    j = pl.program_id(1)
    nb = pl.num_programs(1)
    ext_scr[0:CONV_HALO, :] = jnp.where(j > 0, prev_ref[...], 0.0)
    ext_scr[CONV_HALO:CONV_HALO + tb, :] = x_ref[...]
    ext_scr[CONV_HALO + tb:, :] = jnp.where(j < nb - 1, next_ref[...], 0.0)
    base = CONV_HALO - CONV_K // 2
    y = jnp.zeros((tb, 3 * GDN_WIDTH), F32)
    for t in range(CONV_K):
        y = y + ext_scr[base + t:base + t + tb, :] * w_ref[t:t + 1, :]
    y = y * _sigmoid(y)
    qc, kc = y[:, :GDN_WIDTH], y[:, GDN_WIDTH:2 * GDN_WIDTH]
    q_ref[...] = qc * lax.rsqrt(_head_sumsq(qc) + EPS)
    k_ref[...] = kc * lax.rsqrt(_head_sumsq(kc) + EPS)
    v_ref[...] = y[:, 2 * GDN_WIDTH:]


def _gdn_prep(qkv, conv_w):
    b, s, w = qkv.shape
    tb = min(SEQ_BLOCK, s)
    nb = s // tb
    hb = tb // CONV_HALO
    last = s // CONV_HALO - 1
    out = jax.ShapeDtypeStruct((b, s, GDN_WIDTH), F32)
    return pl.pallas_call(
        functools.partial(_gdn_prep_kernel, tb=tb),
        out_shape=(out, out, out),
        grid=(b, nb),
        in_specs=[pl.BlockSpec((None, tb, w), lambda bi, j: (bi, j, 0)),
                  pl.BlockSpec((None, CONV_HALO, w), lambda bi, j: (bi, jnp.maximum(j * hb - 1, 0), 0)),
                  pl.BlockSpec((None, CONV_HALO, w), lambda bi, j: (bi, jnp.minimum((j + 1) * hb, last), 0)),
                  pl.BlockSpec((CONV_K, w), lambda bi, j: (0, 0))],
        out_specs=tuple(pl.BlockSpec((None, tb, GDN_WIDTH), lambda bi, j: (bi, j, 0)) for _ in range(3)),
        scratch_shapes=[pltpu.VMEM((tb + 2 * CONV_HALO, w), F32)],
        compiler_params=_params("parallel", "arbitrary"),
        name="gdn_prep",
    )(qkv, qkv, qkv, conv_w.astype(F32))


GD_CHUNKS_PER_TRIP = 2
GD_INV_BLOCK = 16


def _gdn_kernel(qf_ref, kf_ref, vf_ref, gf_ref, af_ref,
                qb_ref, kb_ref, vb_ref, gb_ref, ab_ref,
                bias_ref, alog_ref, alr_ref, dtr_ref,
                of_ref, ob_ref,
                s_scr, gr_scr, eg_scr, u_scr, wq_scr, at_scr, ks_scr, *, chunks):
    j = pl.program_id(1)

    @pl.when(j == 0)
    def _():
        s_scr[...] = jnp.zeros_like(s_scr)

    bias = bias_ref[...]
    decay = -jnp.exp(alog_ref[...])
    lane_row = lax.broadcasted_iota(jnp.int32, (chunks, PAIR), 1)
    for d, a_ref in enumerate((af_ref, ab_ref)):
        ucum = _pair_cumsum_matrix(rev=bool(d))
        last = 0 if d else CHUNK - 1
        for p in range(GD_PAIRS):
            g = -jnp.exp(alr_ref[d, p]) * _softplus(a_ref[p] + dtr_ref[d, p])
            gc = _exact_right(g, ucum)
            eg = jnp.exp(jnp.where(lane_row < HEAD_DIM, gc[:, last:last + 1],
                                   gc[:, HEAD_DIM + last:HEAD_DIM + last + 1]))
            for c in range(chunks):
                gr_scr[d * GD_PAIRS + p, c] = gc[c:c + 1, :]
                eg_scr[d * GD_PAIRS + p, c] = eg[c:c + 1, :]

    r = lax.broadcasted_iota(jnp.int32, (PAIR, PAIR), 0)
    cidx = lax.broadcasted_iota(jnp.int32, (PAIR, PAIR), 1)
    eye = jnp.where(r == cidx, 1.0, 0.0).astype(F32)
    same16 = (r // GD_INV_BLOCK) == (cidx // GD_INV_BLOCK)
    dirs = ((qf_ref, kf_ref, vf_ref, gf_ref, of_ref), (qb_ref, kb_ref, vb_ref, gb_ref, ob_ref))
    units = [(d, p) for d in range(N_DIR) for p in range(GD_PAIRS)]

    def precompute(t, carry):
        masks = [_pair_masks(False), _pair_masks(True)]
        tris = [_chunk_tri(False), _chunk_tri(True)]
        items = []
        for cc in range(GD_CHUNKS_PER_TRIP):
            c = t * GD_CHUNKS_PER_TRIP + cc
            r0 = pl.multiple_of(c * CHUNK, CHUNK)
            for d in range(N_DIR):
                gt = dirs[d][3][pl.ds(r0, CHUNK), :]
                beta_all = _sigmoid(gt)
                gc_all = _exact_left(tris[d], decay * _softplus(gt + bias))
                for p in range(GD_PAIRS):
                    items.append(dict(d=d, p=p, c=c, r0=r0, u=d * GD_PAIRS + p,
                                      cols=slice(p * PAIR, (p + 1) * PAIR),
                                      beta=_pair_col(beta_all, GATE_GD_B + d * GDN_HEADS + 2 * p),
                                      gc=_pair_col(gc_all, GATE_GD_A + d * GDN_HEADS + 2 * p)))
        for it in items:
            q_ref, k_ref = dirs[it["d"]][0], dirs[it["d"]][1]
            q_b = _stack_pair(q_ref[pl.ds(it["r0"], CHUNK), it["cols"]]).astype(BF16)
            k_b = _stack_pair(k_ref[pl.ds(it["r0"], CHUNK), it["cols"]]).astype(BF16)
            it["grams"] = _dot_nt(jnp.concatenate([q_b, k_b], axis=0), k_b)
        for it in items:
            valid, strict = masks[it["d"]]
            decay_ts = jnp.exp(jnp.where(valid, it["gc"] - gr_scr[it["u"], it["c"]], NEG))
            grams = it.pop("grams")
            at_scr[it["u"], it["c"]] = (grams[:PAIR] * (HEAD_DIM ** -0.5) * decay_ts).astype(BF16)
            neg_a = jnp.where(strict, -(grams[PAIR:] * it["beta"] * decay_ts), 0.0)
            nd = jnp.where(same16, neg_a, 0.0)
            it["nl"] = (neg_a - nd).astype(BF16)
            it["t_inv"] = eye + nd
            n_b = nd.astype(BF16)
            it["pw"] = _dot(n_b, n_b)
        for _ in range(2):
            for it in items:
                pw_b = it["pw"].astype(BF16)
                both = _dot(jnp.concatenate([it["t_inv"].astype(BF16), pw_b], axis=0), pw_b)
                it["t_inv"] = it["t_inv"] + both[:PAIR]
                it["pw"] = both[PAIR:]
        for it in items:
            it["t_inv"] = it["t_inv"] + _dot(it["t_inv"].astype(BF16), it.pop("pw").astype(BF16))
        for it in items:
            it["x_b"] = it["t_inv"].astype(BF16)
            it["m"] = _dot(it["x_b"], it.pop("nl"))
        for it in items:
            m_b = it.pop("m").astype(BF16)
            both = _dot(m_b, jnp.concatenate([m_b, it.pop("x_b")], axis=1))
            it["m2"] = both[:, :PAIR]
            it["t_inv"] = it["t_inv"] + both[:, PAIR:]
        for it in items:
            it["t_inv"] = it["t_inv"] + _dot(it.pop("m2").astype(BF16), it["t_inv"].astype(BF16))
        for it in items:
            q_ref, k_ref, v_ref = dirs[it["d"]][:3]
            rows = pl.ds(it["r0"], CHUNK)
            q_pair = _stack_pair(q_ref[rows, it["cols"]])
            k_pair = _stack_pair(k_ref[rows, it["cols"]])
            v_pair = _stack_pair(v_ref[rows, it["cols"]])
            beta, gc = it["beta"], it["gc"]
            egc = jnp.exp(gc)
            rhs = jnp.concatenate([v_pair * beta, k_pair * (beta * egc)], axis=1).astype(BF16)
            uw = _dot(it["t_inv"].astype(BF16), rhs)
            g_last = _pair_last(gc, bool(it["d"]))
            q_dec = q_pair * ((HEAD_DIM ** -0.5) * egc)
            u_scr[it["u"], it["c"]] = uw[:, :PAIR]
            wq_scr[it["u"], it["c"]] = jnp.concatenate([uw[:, PAIR:], q_dec], axis=0).astype(BF16)
            ks_scr[it["u"], it["c"]] = (k_pair * jnp.exp(g_last - gc)).astype(BF16)
        return carry

    lax.fori_loop(0, chunks // GD_CHUNKS_PER_TRIP, precompute, 0)

    def scan(i, carry):
        cs = (i, chunks - 1 - i)
        state = [s_scr[u] for u in range(len(units))]
        ws = [_dot(wq_scr[u, cs[d]], state[u].astype(BF16)) for u, (d, p) in enumerate(units)]
        v_new = [(u_scr[u, cs[d]] - ws[u][:PAIR]).astype(BF16) for u, (d, p) in enumerate(units)]
        for u, (d, p) in enumerate(units):
            s_scr[u] = state[u] * eg_scr[u, cs[d]] + _dot_tn(ks_scr[u, cs[d]], v_new[u])
        for u, (d, p) in enumerate(units):
            o = ws[u][PAIR:] + _dot(at_scr[u, cs[d]], v_new[u])
            r0 = pl.multiple_of(cs[d] * CHUNK, CHUNK)
            dirs[d][4][pl.ds(r0, CHUNK), p * PAIR:(p + 1) * PAIR] = o[:CHUNK] + o[CHUNK:]
        return carry

    lax.fori_loop(0, chunks, scan, 0)


def _gdn(q, k, v, gates, a_log, dt_bias):
    b, s, _ = q.shape
    tb = min(GDN_SEQ_BLOCK, s)
    nb = s // tb
    chunks = tb // CHUNK
    a_rows = _gate_rows(gates, GATE_GD_A, GDN_HEADS)
    bias_vec = _lane_vector([(GATE_GD_A, dt_bias)])
    alog_vec = _lane_vector([(GATE_GD_A, a_log)])
    alr = _param_rows(a_log, GDN_HEADS)
    dtr = _param_rows(dt_bias, GDN_HEADS)

    def fwd(bi, j):
        return (bi, j, 0)

    def bwd(bi, j):
        return (bi, nb - 1 - j, 0)

    seq = lambda w, im: pl.BlockSpec((None, tb, w), im)
    rows_f = pl.BlockSpec((None, None, GD_PAIRS, chunks, PAIR), lambda bi, j: (bi, 0, 0, j, 0))
    rows_b = pl.BlockSpec((None, None, GD_PAIRS, chunks, PAIR), lambda bi, j: (bi, 1, 0, nb - 1 - j, 0))
    full = lambda a: pl.BlockSpec(a.shape, lambda bi, j: (0,) * a.ndim)
    units = N_DIR * GD_PAIRS
    out = jax.ShapeDtypeStruct((b, s, GDN_WIDTH), F32)
    return pl.pallas_call(
        functools.partial(_gdn_kernel, chunks=chunks),
        out_shape=(out, out),
        grid=(b, nb),
        in_specs=[seq(GDN_WIDTH, fwd), seq(GDN_WIDTH, fwd), seq(GDN_WIDTH, fwd), seq(GATE_LANES, fwd), rows_f,
                  seq(GDN_WIDTH, bwd), seq(GDN_WIDTH, bwd), seq(GDN_WIDTH, bwd), seq(GATE_LANES, bwd), rows_b,
                  full(bias_vec), full(alog_vec), full(alr), full(dtr)],
        out_specs=(seq(GDN_WIDTH, fwd), seq(GDN_WIDTH, bwd)),
        scratch_shapes=[pltpu.VMEM((units, PAIR, PAIR), F32),
                        pltpu.VMEM((units, chunks, 1, PAIR), F32),
                        pltpu.VMEM((units, chunks, 1, PAIR), F32),
                        pltpu.VMEM((units, chunks, PAIR, PAIR), F32),
                        pltpu.VMEM((units, chunks, 2 * PAIR, PAIR), BF16),
                        pltpu.VMEM((units, chunks, PAIR, PAIR), BF16),
                        pltpu.VMEM((units, chunks, PAIR, PAIR), BF16)],
        compiler_params=_params("parallel", "arbitrary"),
        name="gated_deltanet",
    )(q, k, v, gates, a_rows, q, k, v, gates, a_rows, bias_vec, alog_vec, alr, dtr)


def _out_proj_kernel(x_ref, na_ref, hf_ref, hb_ref, mo_ref, mw_ref, of_ref, ob_ref, gz_ref, gw_ref,
                     w_ref, o_ref):
    hs = hf_ref[...] + hb_ref[...]
    y_ml = hs * lax.rsqrt(_head_sumsq(hs) * (1.0 / HEAD_DIM) + EPS) * mw_ref[...] * _sigmoid(mo_ref[...])
    os_ = of_ref[...] + ob_ref[...]
    z = gz_ref[...]
    y_gd = os_ * lax.rsqrt(_head_sumsq(os_) * (1.0 / HEAD_DIM) + EPS) * gw_ref[...] * (z * _sigmoid(z))
    acc = x_ref[...] + _dot(na_ref[...], w_ref[0:NA_WIDTH, :])
    acc = acc + _dot(y_ml.astype(BF16), w_ref[NA_WIDTH:NA_WIDTH + ML_WIDTH, :])
    acc = acc + _dot(y_gd.astype(BF16), w_ref[NA_WIDTH + ML_WIDTH:, :])
    o_ref[...] = acc


def _out_proj(x2d, y_na, hf, hb, ml_o, ml_norm_w, of, ob, gd_z, gdn_norm_w, w_out):
    t = x2d.shape[0]
    tm = min(TM_ROWS, t)
    row = lambda w: pl.BlockSpec((tm, w), lambda i: (i, 0))
    const = lambda r, c: pl.BlockSpec((r, c), lambda i: (0, 0))
    return pl.pallas_call(
        _out_proj_kernel,
        out_shape=jax.ShapeDtypeStruct((t, D_MODEL), F32),
        grid=(t // tm,),
        in_specs=[row(D_MODEL), row(NA_WIDTH), row(ML_WIDTH), row(ML_WIDTH), row(ML_WIDTH), const(1, ML_WIDTH),
                  row(GDN_WIDTH), row(GDN_WIDTH), row(GDN_WIDTH), const(1, GDN_WIDTH),
                  const(D_MODEL, D_MODEL)],
        out_specs=row(D_MODEL),
        compiler_params=_params("parallel"),
        name="out_proj",
    )(x2d, y_na, hf, hb, ml_o, ml_norm_w.reshape(1, ML_WIDTH).astype(F32), of, ob, gd_z,
      gdn_norm_w.reshape(1, GDN_WIDTH).astype(F32), w_out.astype(BF16))


def _mem_kv_kernel(m_ref, nw_ref, w_ref, k_ref, v_ref):
    h = _rms(m_ref[...], nw_ref[...]).astype(BF16)
    k_ref[...] = _dot(h, w_ref[:, :D_MODEL]).astype(BF16)
    v_ref[...] = _dot(h, w_ref[:, D_MODEL:]).astype(BF16)


def _mem_kv(mem2d, norm_w, w_kv):
    t = mem2d.shape[0]
    tm = min(TM_ROWS, t)
    out = jax.ShapeDtypeStruct((t, D_MODEL), BF16)
    return pl.pallas_call(
        _mem_kv_kernel,
        out_shape=(out, out),
        grid=(t // tm,),
        in_specs=[pl.BlockSpec((tm, D_MODEL), lambda i: (i, 0)),
                  pl.BlockSpec((1, D_MODEL), lambda i: (0, 0)),
                  pl.BlockSpec((D_MODEL, 2 * D_MODEL), lambda i: (0, 0))],
        out_specs=(pl.BlockSpec((tm, D_MODEL), lambda i: (i, 0)), pl.BlockSpec((tm, D_MODEL), lambda i: (i, 0))),
        compiler_params=_params("parallel"),
        name="mem_kv",
    )(mem2d, norm_w.reshape(1, D_MODEL).astype(F32), w_kv.astype(BF16))


def _xattn_kernel(x_ref, nw_ref, wq_ref, k_ref, v_ref, wo_ref, o_ref):
    x = x_ref[...]
    q = _dot(_rms(x, nw_ref[...]).astype(BF16), wq_ref[...]).astype(BF16)
    acc = x
    for h in range(XA_HEADS):
        cols = slice(h * XA_HEAD_DIM, (h + 1) * XA_HEAD_DIM)
        s = _dot_nt(q[:, cols], k_ref[:, cols]) * (XA_HEAD_DIM ** -0.5)
        m = jnp.max(s, axis=-1, keepdims=True)
        e = jnp.exp(s - m)
        l = jnp.sum(e, axis=-1, keepdims=True)
        o_h = _dot(e.astype(BF16), v_ref[:, cols]) / l
        acc = acc + _dot(o_h.astype(BF16), wo_ref[cols, :])
    o_ref[...] = acc


def _xattn(x3d, norm_w, w_q, k, v, w_o):
    b, s, _ = x3d.shape
    tm = min(TM_ROWS, s)
    n_mem = k.shape[1]
    return pl.pallas_call(
        _xattn_kernel,
        out_shape=jax.ShapeDtypeStruct(x3d.shape, F32),
        grid=(b, s // tm),
        in_specs=[pl.BlockSpec((None, tm, D_MODEL), lambda bi, i: (bi, i, 0)),
                  pl.BlockSpec((1, D_MODEL), lambda bi, i: (0, 0)),
                  pl.BlockSpec((D_MODEL, D_MODEL), lambda bi, i: (0, 0)),
                  pl.BlockSpec((None, n_mem, D_MODEL), lambda bi, i: (bi, 0, 0)),
                  pl.BlockSpec((None, n_mem, D_MODEL), lambda bi, i: (bi, 0, 0)),
                  pl.BlockSpec((D_MODEL, D_MODEL), lambda bi, i: (0, 0))],
        out_specs=pl.BlockSpec((None, tm, D_MODEL), lambda bi, i: (bi, i, 0)),
        compiler_params=_params("parallel", "parallel"),
        name="cross_attention",
    )(x3d, norm_w.reshape(1, D_MODEL).astype(F32), w_q.astype(BF16), k, v, w_o.astype(BF16))


FF_CHUNK = 512


def _ffn_kernel(x_ref, nw_ref, w1_ref, w2_ref, fw_ref, o_ref, *, final_norm):
    x = x_ref[...]
    h = _rms(x, nw_ref[...]).astype(BF16)
    acc = x
    for c0 in range(0, D_FF, FF_CHUNK):
        a = jnp.maximum(_dot(h, w1_ref[:, c0:c0 + FF_CHUNK]), 0.0)
        acc = acc + _dot((a * a).astype(BF16), w2_ref[c0:c0 + FF_CHUNK, :])
    o_ref[...] = _rms(acc, fw_ref[...]) if final_norm else acc


def _ffn(x2d, norm_w, w1, w2, final_w, final_norm):
    t = x2d.shape[0]
    tm = min(TM_ROWS, t)
    return pl.pallas_call(
        functools.partial(_ffn_kernel, final_norm=final_norm),
        out_shape=jax.ShapeDtypeStruct((t, D_MODEL), F32),
        grid=(t // tm,),
        in_specs=[pl.BlockSpec((tm, D_MODEL), lambda i: (i, 0)),
                  pl.BlockSpec((1, D_MODEL), lambda i: (0, 0)),
                  pl.BlockSpec((D_MODEL, D_FF), lambda i: (0, 0), pipeline_mode=pl.Buffered(1)),
                  pl.BlockSpec((D_FF, D_MODEL), lambda i: (0, 0), pipeline_mode=pl.Buffered(1)),
                  pl.BlockSpec((1, D_MODEL), lambda i: (0, 0))],
        out_specs=pl.BlockSpec((tm, D_MODEL), lambda i: (i, 0)),
        compiler_params=_params("parallel"),
        name="ffn",
    )(x2d, norm_w.reshape(1, D_MODEL).astype(F32), w1.astype(BF16), w2.astype(BF16),
      final_w.reshape(1, D_MODEL).astype(F32))


def kernel(x, mem, norm_mix_w, w_in, na_rel_bias, ml_i_bias, ml_f_bias, ml_norm_w, gdn_conv_w, gdn_a_log,
           gdn_dt_bias, gdn_norm_w, w_out, norm_xa_w, norm_mem_w, w_xq, w_xkv, w_xo, norm_ffn_w, w_ff1,
           w_ff2, norm_out_w):
    b, s, d = x.shape
    depth = w_in.shape[0]
    x2d = x.reshape(b * s, d).astype(F32)
    mem2d = mem.reshape(-1, d).astype(F32)
    for l in range(depth):
        (na_q, na_k, na_v, ml_q, ml_k, ml_v, ml_o, gd_qkv, gd_z, gates) = _in_proj(
            x2d, norm_mix_w[l].astype(F32), _permute_w_in(w_in[l]))
        seq = lambda a: a.reshape(b, s, a.shape[-1])
        gates3 = seq(gates)
        y_na = _neighbourhood_attention(seq(na_q), seq(na_k), seq(na_v), _na_bias_table(na_rel_bias[l]))
        hf, hb = _mlstm(seq(ml_q), seq(ml_k), seq(ml_v), gates3, ml_i_bias[l], ml_f_bias[l])
        gq, gk, gv = _gdn_prep(seq(gd_qkv), gdn_conv_w[l])
        of, ob = _gdn(gq, gk, gv, gates3, gdn_a_log[l], gdn_dt_bias[l])
        flat = lambda a: a.reshape(b * s, a.shape[-1])
        x2d = _out_proj(x2d, flat(y_na), flat(hf), flat(hb), ml_o, ml_norm_w[l], flat(of), flat(ob), gd_z,
                        gdn_norm_w[l], w_out[l])
        mk, mv = _mem_kv(mem2d, norm_mem_w[l], w_xkv[l])
        n_mem = mem.shape[1]
        x2d = _xattn(x2d.reshape(b, s, d), norm_xa_w[l], w_xq[l], mk.reshape(b, n_mem, d),
                     mv.reshape(b, n_mem, d), w_xo[l]).reshape(b * s, d)
        x2d = _ffn(x2d, norm_ffn_w[l], w_ff1[l], w_ff2[l], norm_out_w, final_norm=(l == depth - 1))
    return x2d.reshape(b, s, d).astype(x.dtype)
```

```python
import functools

import numpy as np
import jax
import jax.numpy as jnp
from jax import lax
from jax.experimental import pallas as pl
from jax.experimental.pallas import tpu as pltpu

F32 = jnp.float32
BF16 = jnp.bfloat16

D_MODEL = 1024
HEAD_DIM = 64
GRID_W = 64
NA_HEADS = 6
NA_WIN_ROWS = 8
NA_WIN_COLS = 16
ML_HEADS = 4
GDN_HEADS = 6
CONV_K = 5
N_DIR = 2
XA_HEADS = 4
XA_HEAD_DIM = D_MODEL // XA_HEADS
D_FF = 4 * D_MODEL
NA_WIDTH = NA_HEADS * HEAD_DIM
ML_WIDTH = ML_HEADS * HEAD_DIM
GDN_WIDTH = GDN_HEADS * HEAD_DIM
EPS = 1e-6
CHUNK = 64
PAIR = 2 * HEAD_DIM
NEG = -1e30

IN_SIZES = (NA_WIDTH, NA_WIDTH, NA_WIDTH,
            ML_WIDTH, ML_WIDTH, ML_WIDTH, ML_WIDTH, N_DIR * ML_HEADS, N_DIR * ML_HEADS,
            GDN_WIDTH, GDN_WIDTH, GDN_WIDTH, GDN_WIDTH, N_DIR * GDN_HEADS, N_DIR * GDN_HEADS)

GATE_ML_I = 0
GATE_ML_F = GATE_ML_I + N_DIR * ML_HEADS
GATE_GD_B = GATE_ML_F + N_DIR * ML_HEADS
GATE_GD_A = GATE_GD_B + N_DIR * GDN_HEADS
GATE_USED = GATE_GD_A + N_DIR * GDN_HEADS
GATE_LANES = 128

V7X_VMEM_LIMIT = 56 * 1024 * 1024
MXU_COLS = 256

TM_ROWS = 512
NA_ROWS_PER_STEP = 8
SEQ_BLOCK = 1024
GDN_SEQ_BLOCK = 512


def _params(*sem):
    return pltpu.CompilerParams(dimension_semantics=sem, vmem_limit_bytes=V7X_VMEM_LIMIT)


def _dot(a, b):
    return jnp.dot(a, b, preferred_element_type=F32)


def _dot_nt(a, b):
    return lax.dot_general(a, b, (((1,), (1,)), ((), ())), preferred_element_type=F32)


def _dot_tn(a, b):
    return lax.dot_general(a, b, (((0,), (0,)), ((), ())), preferred_element_type=F32)


def _split3(x):
    x1 = x.astype(BF16)
    r1 = x - x1.astype(F32)
    x2 = r1.astype(BF16)
    x3 = (r1 - x2.astype(F32)).astype(BF16)
    return x1, x2, x3


def _exact_left(m01, x):
    x1, x2, x3 = _split3(x)
    return _dot(m01, x1) + _dot(m01, x2) + _dot(m01, x3)


def _exact_right(x, m01):
    x1, x2, x3 = _split3(x)
    return _dot(x1, m01) + _dot(x2, m01) + _dot(x3, m01)


def _rms(x, w):
    ms = jnp.mean(x * x, axis=-1, keepdims=True)
    return x * lax.rsqrt(ms + EPS) * w


def _softplus(x):
    return jnp.maximum(x, 0.0) + jnp.log1p(jnp.exp(-jnp.abs(x)))


def _sigmoid(x):
    return 1.0 / (1.0 + jnp.exp(-x))


def _segment_mean_matrix(width):
    r = lax.broadcasted_iota(jnp.int32, (width, width), 0) // HEAD_DIM
    c = lax.broadcasted_iota(jnp.int32, (width, width), 1) // HEAD_DIM
    return jnp.where(r == c, 1.0, 0.0).astype(BF16)


def _head_sumsq(t):
    return _exact_right(t * t, _segment_mean_matrix(t.shape[-1]))


def _pair_masks(rev):
    r = lax.broadcasted_iota(jnp.int32, (PAIR, PAIR), 0)
    c = lax.broadcasted_iota(jnp.int32, (PAIR, PAIR), 1)
    same = (r // HEAD_DIM) == (c // HEAD_DIM)
    t, s = r % HEAD_DIM, c % HEAD_DIM
    if rev:
        return same & (s >= t), same & (s > t)
    return same & (s <= t), same & (s < t)


def _chunk_tri(rev):
    t = lax.broadcasted_iota(jnp.int32, (CHUNK, CHUNK), 0)
    s = lax.broadcasted_iota(jnp.int32, (CHUNK, CHUNK), 1)
    return jnp.where((s >= t) if rev else (s <= t), 1.0, 0.0).astype(BF16)


def _stack_pair(x2):
    lane = lax.broadcasted_iota(jnp.int32, x2.shape, 1)
    zero = jnp.zeros_like(x2)
    return jnp.concatenate([jnp.where(lane < HEAD_DIM, x2, zero),
                            jnp.where(lane >= HEAD_DIM, x2, zero)], axis=0)


def _pair_col(tile, lane0):
    return jnp.concatenate([tile[:, lane0:lane0 + 1], tile[:, lane0 + 1:lane0 + 2]], axis=0)


def _pair_last(col, rev):
    i0 = 0 if rev else CHUNK - 1
    a = jnp.broadcast_to(col[i0:i0 + 1, :], (CHUNK, 1))
    b = jnp.broadcast_to(col[CHUNK + i0:CHUNK + i0 + 1, :], (CHUNK, 1))
    return jnp.concatenate([a, b], axis=0)


IN_SEGMENTS = (
    (NA_WIDTH, BF16), (NA_WIDTH, BF16), (NA_WIDTH, BF16),
    (ML_WIDTH, BF16), (ML_WIDTH, BF16), (ML_WIDTH, BF16), (ML_WIDTH, F32),
    (3 * GDN_WIDTH, F32), (GDN_WIDTH, F32), (GATE_LANES, F32))
IN_COLS = sum(w for w, _ in IN_SEGMENTS)
IN_DOT_COLS = 3 * MXU_COLS


def _in_proj_kernel(x_ref, nw_ref, w_ref, *out_refs):
    h = _rms(x_ref[...], nw_ref[...]).astype(BF16)
    starts = np.cumsum([0] + [w for w, _ in IN_SEGMENTS])
    for c0 in range(0, IN_COLS, IN_DOT_COLS):
        c1 = min(c0 + IN_DOT_COLS, IN_COLS)
        acc = _dot(h, w_ref[:, c0:c1])
        for o_ref, s0, s1 in zip(out_refs, starts[:-1], starts[1:]):
            a, b = max(c0, int(s0)), min(c1, int(s1))
            if a < b:
                o_ref[:, a - int(s0):b - int(s0)] = acc[:, a - c0:b - c0].astype(o_ref.dtype)


def _in_proj(x2d, norm_w, w_perm):
    t = x2d.shape[0]
    tm = min(TM_ROWS, t)
    outs = tuple(jax.ShapeDtypeStruct((t, w), dt) for w, dt in IN_SEGMENTS)
    return pl.pallas_call(
        _in_proj_kernel,
        out_shape=outs,
        grid=(t // tm,),
        in_specs=[pl.BlockSpec((tm, D_MODEL), lambda i: (i, 0)),
                  pl.BlockSpec((1, D_MODEL), lambda i: (0, 0)),
                  pl.BlockSpec((D_MODEL, IN_COLS), lambda i: (0, 0))],
        out_specs=tuple(pl.BlockSpec((tm, w), lambda i: (i, 0)) for w, _ in IN_SEGMENTS),
        compiler_params=_params("parallel"),
        name="in_proj",
    )(x2d, norm_w.reshape(1, D_MODEL), w_perm)


def _permute_w_in(w):
    parts = jnp.split(w, np.cumsum(IN_SIZES)[:-1], axis=-1)
    (na_q, na_k, na_v, ml_q, ml_k, ml_v, ml_o, ml_i, ml_f, gd_q, gd_k, gd_v, gd_z, gd_b, gd_a) = parts
    pad = jnp.zeros((w.shape[0], GATE_LANES - GATE_USED), w.dtype)
    return jnp.concatenate([na_q, na_k, na_v, ml_q, ml_k, ml_v, ml_o, gd_q, gd_k, gd_v, gd_z,
                            ml_i, ml_f, gd_b, gd_a, pad], axis=-1).astype(BF16)


NA_BIAS_ROWS = 96


def _na_bias_kernel(rb_ref, o_ref):
    n = GRID_W * GRID_W
    dc = lax.broadcasted_iota(jnp.int32, (GATE_LANES, n), 0)
    col = lax.broadcasted_iota(jnp.int32, (GATE_LANES, n), 1)
    q, kc = col // GRID_W, col % GRID_W
    c0 = jnp.clip(q - NA_WIN_COLS // 2, 0, GRID_W - NA_WIN_COLS)
    valid = (kc >= c0) & (kc < c0 + NA_WIN_COLS)
    onehot = jnp.where(valid & (kc - q + (NA_WIN_COLS - 1) == dc), 1.0, 0.0).astype(BF16)
    o_ref[...] = jnp.where(valid[0:1, :], _exact_right(rb_ref[...], onehot), NEG)


def _na_bias_table(rel_bias):
    wr = NA_WIN_ROWS
    nh, ndr, ndc = rel_bias.shape
    rb = jnp.zeros((NA_BIAS_ROWS, GATE_LANES), F32).at[:nh * ndr, :ndc].set(
        rel_bias.astype(F32).reshape(nh * ndr, ndc))
    band = pl.pallas_call(
        _na_bias_kernel,
        out_shape=jax.ShapeDtypeStruct((NA_BIAS_ROWS, GRID_W * GRID_W), F32),
        name="na_bias_expand",
    )(rb)
    band = band[:nh * ndr].reshape(nh, ndr, GRID_W, GRID_W)
    tab = jnp.stack([band[:, wr - 1 - var:2 * wr - 1 - var] for var in range(wr)], axis=1)
    tab = jnp.transpose(tab, (0, 1, 3, 2, 4))
    tab = tab.reshape(NA_HEADS // 2, 2, wr, GRID_W, wr * GRID_W)
    return jnp.moveaxis(tab, 1, 2).reshape(NA_HEADS // 2, wr, 2 * GRID_W, wr * GRID_W)


NA_ROWS_PER_TRIP = 2


def _na_kernel(q_ref, k_ref, v_ref, bias_ref, o_ref, *, rows, rows_per_step):
    j = pl.program_id(1)
    nkeys = NA_WIN_ROWS * GRID_W
    lane = lax.broadcasted_iota(jnp.int32, (GRID_W, PAIR), 1)

    def body(t, carry):
        items = []
        for rr_ in range(NA_ROWS_PER_TRIP):
            rr = t * NA_ROWS_PER_TRIP + rr_
            r = j * rows_per_step + rr
            r0 = jnp.clip(r - NA_WIN_ROWS // 2, 0, rows - NA_WIN_ROWS)
            qoff = pl.multiple_of(rr * GRID_W, GRID_W)
            koff = pl.multiple_of(r0 * GRID_W, GRID_W)
            for p in range(NA_HEADS // 2):
                items.append(dict(p=p, variant=r - r0, qoff=qoff, koff=koff,
                                  cols=slice(p * PAIR, (p + 1) * PAIR)))
        for it in items:
            q_pair = _stack_pair(q_ref[pl.ds(it["qoff"], GRID_W), it["cols"]])
            q_pair = (q_pair.astype(F32) * (HEAD_DIM ** -0.5)).astype(BF16)
            it["s"] = _dot_nt(q_pair, k_ref[pl.ds(it["koff"], nkeys), it["cols"]])
        for it in items:
            s = it.pop("s") + bias_ref[it["p"], it["variant"]]
            e = jnp.exp(s - jnp.max(s, axis=-1, keepdims=True))
            it["l"] = jnp.sum(e, axis=-1, keepdims=True)
            it["o"] = _dot(e.astype(BF16), v_ref[pl.ds(it["koff"], nkeys), it["cols"]])
        for it in items:
            o = it.pop("o") * (1.0 / it.pop("l"))
            o_ref[pl.ds(it["qoff"], GRID_W), it["cols"]] = jnp.where(
                lane < HEAD_DIM, o[:GRID_W], o[GRID_W:]).astype(o_ref.dtype)
        return carry

    lax.fori_loop(0, rows_per_step // NA_ROWS_PER_TRIP, body, 0)


def _neighbourhood_attention(q, k, v, bias_tab):
    b, s, _ = q.shape
    rows = s // GRID_W
    assert rows >= NA_WIN_ROWS
    rps = min(NA_ROWS_PER_STEP, rows)
    tq = rps * GRID_W
    return pl.pallas_call(
        functools.partial(_na_kernel, rows=rows, rows_per_step=rps),
        out_shape=jax.ShapeDtypeStruct((b, s, NA_WIDTH), BF16),
        grid=(b, rows // rps),
        in_specs=[pl.BlockSpec((None, tq, NA_WIDTH), lambda bi, j: (bi, j, 0)),
                  pl.BlockSpec((None, s, NA_WIDTH), lambda bi, j: (bi, 0, 0)),
                  pl.BlockSpec((None, s, NA_WIDTH), lambda bi, j: (bi, 0, 0)),
                  pl.BlockSpec(bias_tab.shape, lambda bi, j: (0, 0, 0, 0))],
        out_specs=pl.BlockSpec((None, tq, NA_WIDTH), lambda bi, j: (bi, j, 0)),
        compiler_params=_params("parallel", "arbitrary"),
        name="neighbourhood_attention",
    )(q, k, v, bias_tab)


def _gate_rows(gates, lane0, heads):
    b, s, _ = gates.shape
    g = gates[:, :, lane0:lane0 + N_DIR * heads].reshape(b, s // CHUNK, CHUNK, N_DIR, heads // 2, 2)
    return jnp.transpose(g, (0, 3, 4, 1, 5, 2)).reshape(b, N_DIR, heads // 2, s // CHUNK, PAIR)


def _param_rows(p, heads):
    return jnp.repeat(p.astype(F32).reshape(N_DIR, heads // 2, 2), HEAD_DIM, axis=-1).reshape(
        N_DIR, heads // 2, 1, PAIR)


def _lane_vector(entries):
    v = jnp.zeros((GATE_LANES,), F32)
    for lane0, vals in entries:
        v = lax.dynamic_update_slice(v, vals.astype(F32).reshape(-1), (lane0,))
    return v.reshape(1, GATE_LANES)


def _pair_cumsum_matrix(rev):
    valid, _ = _pair_masks(rev)
    r = lax.broadcasted_iota(jnp.int32, (PAIR, PAIR), 0)
    c = lax.broadcasted_iota(jnp.int32, (PAIR, PAIR), 1)
    same = (r // HEAD_DIM) == (c // HEAD_DIM)
    sp, s = r % HEAD_DIM, c % HEAD_DIM
    del valid
    return jnp.where(same & ((sp >= s) if rev else (sp <= s)), 1.0, 0.0).astype(BF16)


ML_PAIRS = ML_HEADS // 2
ML_UNITS = N_DIR * ML_PAIRS
ML_CHUNKS_PER_TRIP = 2
OS_ROWS = 16


def _outer_sum_operands(col_term, row_term):
    ones = jnp.ones((3, PAIR), BF16)
    zeros = jnp.zeros((OS_ROWS - 6, PAIR), BF16)
    a = jnp.concatenate(list(_split3(col_term)) + [ones, zeros], axis=0)
    b = jnp.concatenate([ones] + list(_split3(row_term)) + [zeros], axis=0)
    return a, b


def _mlstm_kernel(qf_ref, kf_ref, vf_ref, if_ref, ff_ref,
                  qb_ref, kb_ref, vb_ref, ib_ref, fb_ref,
                  ibr_ref, fbr_ref,
                  hf_ref, hb_ref,
                  c_scr, m_scr, cs_scr, rt_scr, ai_scr, em_scr, we_scr, a_scr, g_scr, vt_scr, ut_scr, *, chunks):
    j = pl.program_id(1)

    @pl.when(j == 0)
    def _():
        c_scr[...] = jnp.zeros_like(c_scr)
        m_scr[...] = jnp.zeros_like(m_scr)

    lane = lax.broadcasted_iota(jnp.int32, (chunks, PAIR), 1)
    pos = lane % HEAD_DIM
    lo = lane < HEAD_DIM
    for d, (i_ref, f_ref) in enumerate(((if_ref, ff_ref), (ib_ref, fb_ref))):
        rev = bool(d)
        ucum = _pair_cumsum_matrix(rev)
        last = 0 if rev else CHUNK - 1
        for p in range(ML_PAIRS):
            u = d * ML_PAIRS + p
            logf = -_softplus(-(f_ref[p] + fbr_ref[d, p]))
            bcum = _exact_right(logf, ucum)
            cs = i_ref[p] + ibr_ref[d, p] - bcum
            bl = jnp.where(lo, bcum[:, last:last + 1], bcum[:, HEAD_DIM + last:HEAD_DIM + last + 1])
            cm = cs
            for k in (1, 2, 4, 8, 16, 32):
                if rev:
                    cm = jnp.where(pos < HEAD_DIM - k, jnp.maximum(cm, pltpu.roll(cm, PAIR - k, axis=1)), cm)
                else:
                    cm = jnp.where(pos >= k, jnp.maximum(cm, pltpu.roll(cm, k, axis=1)), cm)
            w = bl + cs
            m_loc = jnp.where(lo, jnp.max(jnp.where(lo, w, NEG), axis=-1, keepdims=True),
                              jnp.max(jnp.where(lo, NEG, w), axis=-1, keepdims=True))
            m = m_scr[u]
            m_prev, a_rows, g_rows = [None] * chunks, [None] * chunks, [None] * chunks
            for c in (range(chunks - 1, -1, -1) if rev else range(chunks)):
                m_prev[c] = m
                m_new = jnp.maximum(bl[c:c + 1] + m, m_loc[c:c + 1])
                a_rows[c] = jnp.exp(bl[c:c + 1] + m - m_new)
                g_rows[c] = jnp.exp(m_loc[c:c + 1] - m_new)
                m = m_new
            m_scr[u] = m
            inter = bcum + jnp.concatenate(m_prev, axis=0)
            m_t = jnp.maximum(bcum + cm, inter)
            cs_scr[u] = cs
            rt_scr[u] = bcum - m_t
            ai_scr[u] = jnp.exp(inter - m_t)
            em_scr[u] = jnp.exp(-m_t)
            we_scr[u] = jnp.exp(w - m_loc)
            a_scr[u] = jnp.concatenate(a_rows, axis=0)
            g_scr[u] = jnp.concatenate(g_rows, axis=0)

    dirs = ((qf_ref, kf_ref, vf_ref, hf_ref), (qb_ref, kb_ref, vb_ref, hb_ref))
    units = [(d, p) for d in range(N_DIR) for p in range(ML_PAIRS)]
    lane64 = lax.broadcasted_iota(jnp.int32, (CHUNK, PAIR), 1)
    one_hi = jnp.where(lane64 == HEAD_DIM, 1.0, 0.0).astype(BF16)
    one_lo = jnp.where(lane64 == 0, 1.0, 0.0).astype(BF16)
    lane_row = lax.broadcasted_iota(jnp.int32, (1, PAIR), 1)

    def key_tile(d, p, r0):
        k_pair = _stack_pair(dirs[d][1][pl.ds(r0, CHUNK), p * PAIR:(p + 1) * PAIR])
        return (k_pair.astype(F32) * (HEAD_DIM ** -0.5)).astype(BF16)

    def contributions(t, carry):
        for cc in range(ML_CHUNKS_PER_TRIP):
            c = t * ML_CHUNKS_PER_TRIP + cc
            r0 = pl.multiple_of(c * CHUNK, CHUNK)
            for u, (d, p) in enumerate(units):
                v2 = dirs[d][2][pl.ds(r0, CHUNK), p * PAIR:(p + 1) * PAIR]
                v_ext = jnp.concatenate([jnp.where(lane64 < HEAD_DIM, v2, one_hi),
                                         jnp.where(lane64 >= HEAD_DIM, v2, one_lo)], axis=0)
                v_t = v_ext.astype(F32).T
                vt_scr[u, c] = v_t.astype(BF16)
                ut_scr[u, c] = _dot((v_t * we_scr[u, pl.ds(c, 1), :]).astype(BF16), key_tile(d, p, r0))
        return carry

    lax.fori_loop(0, chunks // ML_CHUNKS_PER_TRIP, contributions, 0)

    def outputs(i, carry):
        cs_ = (i, chunks - 1 - i)
        masks = [_pair_masks(True)[0], _pair_masks(False)[0]]
        items = []
        for u, (d, p) in enumerate(units):
            c = cs_[d]
            r0 = pl.multiple_of(c * CHUNK, CHUNK)
            row = pl.ds(c, 1)
            q_pair = _stack_pair(dirs[d][0][pl.ds(r0, CHUNK), p * PAIR:(p + 1) * PAIR])
            os_a, os_b = _outer_sum_operands(cs_scr[u, row, :], rt_scr[u, row, :])
            items.append(dict(u=u, d=d, p=p, c=c, r0=r0, row=row, q=q_pair,
                              gram=_dot_nt(key_tile(d, p, r0), q_pair),
                              osum=_dot_tn(os_a, os_b)))
        for it in items:
            u = it["u"]
            ct = c_scr[u]
            it["ct"] = ct
            it["st"] = _dot_nt(ct.astype(BF16), it["q"])
        for it in items:
            u, c = it["u"], it["c"]
            s_t = it.pop("gram") * jnp.exp(jnp.where(masks[it["d"]], it.pop("osum"), NEG))
            it["intra"] = jnp.sum(s_t, axis=0, keepdims=True)
            it["num"] = _dot(vt_scr[u, c], s_t.astype(BF16))
        for it in items:
            u, c, d, p, row = it["u"], it["c"], it["d"], it["p"], it["row"]
            a_inter = ai_scr[u, row, :]
            st = it.pop("st")
            den = a_inter * jnp.where(lane_row < HEAD_DIM, st[HEAD_DIM:HEAD_DIM + 1], st[0:1]) + it.pop("intra")
            out_t = (it.pop("num") + a_inter * st) * (1.0 / jnp.maximum(jnp.abs(den), em_scr[u, row, :]))
            out = out_t.T
            dirs[d][3][pl.ds(it["r0"], CHUNK), p * PAIR:(p + 1) * PAIR] = jnp.where(
                lane64 < HEAD_DIM, out[:CHUNK], out[CHUNK:])
            c_scr[u] = a_scr[u, row, :] * it.pop("ct") + g_scr[u, row, :] * ut_scr[u, c]
        return carry

    lax.fori_loop(0, chunks, outputs, 0)


def _mlstm(q, k, v, gates, i_bias, f_bias):
    b, s, _ = q.shape
    tb = min(SEQ_BLOCK, s)
    nb = s // tb
    chunks = tb // CHUNK
    i_rows = _gate_rows(gates, GATE_ML_I, ML_HEADS)
    f_rows = _gate_rows(gates, GATE_ML_F, ML_HEADS)
    ibr = _param_rows(i_bias, ML_HEADS)
    fbr = _param_rows(f_bias, ML_HEADS)

    def fwd(bi, j):
        return (bi, j, 0)

    def bwd(bi, j):
        return (bi, nb - 1 - j, 0)

    seq = lambda w, im: pl.BlockSpec((None, tb, w), im)
    rows_f = pl.BlockSpec((None, None, ML_PAIRS, chunks, PAIR), lambda bi, j: (bi, 0, 0, j, 0))
    rows_b = pl.BlockSpec((None, None, ML_PAIRS, chunks, PAIR), lambda bi, j: (bi, 1, 0, nb - 1 - j, 0))
    full = lambda a: pl.BlockSpec(a.shape, lambda bi, j: (0,) * a.ndim)
    out = jax.ShapeDtypeStruct((b, s, ML_WIDTH), F32)
    rows = pltpu.VMEM((ML_UNITS, chunks, PAIR), F32)
    return pl.pallas_call(
        functools.partial(_mlstm_kernel, chunks=chunks),
        out_shape=(out, out),
        grid=(b, nb),
        in_specs=[seq(ML_WIDTH, fwd), seq(ML_WIDTH, fwd), seq(ML_WIDTH, fwd), rows_f, rows_f,
                  seq(ML_WIDTH, bwd), seq(ML_WIDTH, bwd), seq(ML_WIDTH, bwd), rows_b, rows_b,
                  full(ibr), full(fbr)],
        out_specs=(seq(ML_WIDTH, fwd), seq(ML_WIDTH, bwd)),
        scratch_shapes=[pltpu.VMEM((ML_UNITS, PAIR, PAIR), F32),
                        pltpu.VMEM((ML_UNITS, 1, PAIR), F32),
                        rows, rows, rows, rows, rows, rows, rows,
                        pltpu.VMEM((ML_UNITS, chunks, PAIR, PAIR), BF16),
                        pltpu.VMEM((ML_UNITS, chunks, PAIR, PAIR), F32)],
        compiler_params=_params("parallel", "arbitrary"),
        name="mlstm",
    )(q, k, v, i_rows, f_rows, q, k, v, i_rows, f_rows, ibr, fbr)


GD_PAIRS = GDN_HEADS // 2
CONV_HALO = 8
PREP_ROWS = 128


def _gdn_prep_kernel(x_ref, prev_ref, next_ref, w_ref, q_ref, k_ref, v_ref, ext_scr, *, tb):
    j = pl.program_id(1)
    nb = pl.num_programs(1)
    ext_scr[0:CONV_HALO, :] = jnp.where(j > 0, prev_ref[...], 0.0)
    ext_scr[CONV_HALO:CONV_HALO + tb, :] = x_ref[...]
    ext_scr[CONV_HALO + tb:, :] = jnp.where(j < nb - 1, next_ref[...], 0.0)
    base = CONV_HALO - CONV_K // 2
    seg = _segment_mean_matrix(PAIR)
    taps = [w_ref[t:t + 1, :] for t in range(CONV_K)]

    def rows_block(i, carry):
        r0 = pl.multiple_of(i * PREP_ROWS, PREP_ROWS)
        for ct in range(3 * GDN_WIDTH // PAIR):
            cols = slice(ct * PAIR, (ct + 1) * PAIR)
            slab = ext_scr[pl.ds(r0, PREP_ROWS + 2 * CONV_HALO), cols]
            y = slab[base:base + PREP_ROWS] * taps[0][:, cols]
            for t in range(1, CONV_K):
                y = y + slab[base + t:base + t + PREP_ROWS] * taps[t][:, cols]
            y = y * _sigmoid(y)
            which, off = divmod(ct * PAIR, GDN_WIDTH)
            out_ref = (q_ref, k_ref, v_ref)[which]
            if which < 2:
                y = y * lax.rsqrt(_exact_right(y * y, seg) + EPS)
            out_ref[pl.ds(r0, PREP_ROWS), off:off + PAIR] = y
        return carry

    lax.fori_loop(0, tb // PREP_ROWS, rows_block, 0)


def _gdn_prep(qkv, conv_w):
    b, s, w = qkv.shape
    tb = min(SEQ_BLOCK, s)
    nb = s // tb
    hb = tb // CONV_HALO
    last = s // CONV_HALO - 1
    out = jax.ShapeDtypeStruct((b, s, GDN_WIDTH), F32)
    return pl.pallas_call(
        functools.partial(_gdn_prep_kernel, tb=tb),
        out_shape=(out, out, out),
        grid=(b, nb),
        in_specs=[pl.BlockSpec((None, tb, w), lambda bi, j: (bi, j, 0)),
                  pl.BlockSpec((None, CONV_HALO, w), lambda bi, j: (bi, jnp.maximum(j * hb - 1, 0), 0)),
                  pl.BlockSpec((None, CONV_HALO, w), lambda bi, j: (bi, jnp.minimum((j + 1) * hb, last), 0)),
                  pl.BlockSpec((CONV_K, w), lambda bi, j: (0, 0))],
        out_specs=tuple(pl.BlockSpec((None, tb, GDN_WIDTH), lambda bi, j: (bi, j, 0)) for _ in range(3)),
        scratch_shapes=[pltpu.VMEM((tb + 2 * CONV_HALO, w), F32)],
        compiler_params=_params("parallel", "arbitrary"),
        name="gdn_prep",
    )(qkv, qkv, qkv, conv_w.astype(F32))


GD_CHUNKS_PER_TRIP = 2
GD_INV_BLOCK = 16


def _gdn_kernel(qf_ref, kf_ref, vf_ref, gf_ref, af_ref,
                qb_ref, kb_ref, vb_ref, gb_ref, ab_ref,
                bias_ref, alog_ref, alr_ref, dtr_ref,
                of_ref, ob_ref,
                s_scr, gr_scr, eg_scr, u_scr, wq_scr, at_scr, ks_scr, *, chunks):
    j = pl.program_id(1)

    @pl.when(j == 0)
    def _():
        s_scr[...] = jnp.zeros_like(s_scr)

    bias = bias_ref[...]
    decay = -jnp.exp(alog_ref[...])
    lane_row = lax.broadcasted_iota(jnp.int32, (chunks, PAIR), 1)
    for d, a_ref in enumerate((af_ref, ab_ref)):
        ucum = _pair_cumsum_matrix(rev=bool(d))
        last = 0 if d else CHUNK - 1
        for p in range(GD_PAIRS):
            g = -jnp.exp(alr_ref[d, p]) * _softplus(a_ref[p] + dtr_ref[d, p])
            gc = _exact_right(g, ucum)
            eg = jnp.exp(jnp.where(lane_row < HEAD_DIM, gc[:, last:last + 1],
                                   gc[:, HEAD_DIM + last:HEAD_DIM + last + 1]))
            for c in range(chunks):
                gr_scr[d * GD_PAIRS + p, c] = gc[c:c + 1, :]
                eg_scr[d * GD_PAIRS + p, c] = eg[c:c + 1, :]

    r = lax.broadcasted_iota(jnp.int32, (PAIR, PAIR), 0)
    cidx = lax.broadcasted_iota(jnp.int32, (PAIR, PAIR), 1)
    eye = jnp.where(r == cidx, 1.0, 0.0).astype(F32)
    same16 = (r // GD_INV_BLOCK) == (cidx // GD_INV_BLOCK)
    dirs = ((qf_ref, kf_ref, vf_ref, gf_ref, of_ref), (qb_ref, kb_ref, vb_ref, gb_ref, ob_ref))
    units = [(d, p) for d in range(N_DIR) for p in range(GD_PAIRS)]

    def precompute(t, carry):
        masks = [_pair_masks(False), _pair_masks(True)]
        tris = [_chunk_tri(False), _chunk_tri(True)]
        items = []
        for cc in range(GD_CHUNKS_PER_TRIP):
            c = t * GD_CHUNKS_PER_TRIP + cc
            r0 = pl.multiple_of(c * CHUNK, CHUNK)
            for d in range(N_DIR):
                gt = dirs[d][3][pl.ds(r0, CHUNK), :]
                beta_all = _sigmoid(gt)
                gc_all = _exact_left(tris[d], decay * _softplus(gt + bias))
                for p in range(GD_PAIRS):
                    items.append(dict(d=d, p=p, c=c, r0=r0, u=d * GD_PAIRS + p,
                                      cols=slice(p * PAIR, (p + 1) * PAIR),
                                      beta=_pair_col(beta_all, GATE_GD_B + d * GDN_HEADS + 2 * p),
                                      gc=_pair_col(gc_all, GATE_GD_A + d * GDN_HEADS + 2 * p)))
        for it in items:
            q_ref, k_ref = dirs[it["d"]][0], dirs[it["d"]][1]
            q_b = _stack_pair(q_ref[pl.ds(it["r0"], CHUNK), it["cols"]]).astype(BF16)
            k_b = _stack_pair(k_ref[pl.ds(it["r0"], CHUNK), it["cols"]]).astype(BF16)
            it["grams"] = _dot_nt(jnp.concatenate([q_b, k_b], axis=0), k_b)
        for it in items:
            valid, strict = masks[it["d"]]
            decay_ts = jnp.exp(jnp.where(valid, it["gc"] - gr_scr[it["u"], it["c"]], NEG))
            grams = it.pop("grams")
            at_scr[it["u"], it["c"]] = (grams[:PAIR] * (HEAD_DIM ** -0.5) * decay_ts).astype(BF16)
            neg_a = jnp.where(strict, -(grams[PAIR:] * it["beta"] * decay_ts), 0.0)
            nd = jnp.where(same16, neg_a, 0.0)
            it["nl"] = (neg_a - nd).astype(BF16)
            it["t_inv"] = eye + nd
            n_b = nd.astype(BF16)
            it["pw"] = _dot(n_b, n_b)
        for _ in range(2):
            for it in items:
                pw_b = it["pw"].astype(BF16)
                both = _dot(jnp.concatenate([it["t_inv"].astype(BF16), pw_b], axis=0), pw_b)
                it["t_inv"] = it["t_inv"] + both[:PAIR]
                it["pw"] = both[PAIR:]
        for it in items:
            it["t_inv"] = it["t_inv"] + _dot(it["t_inv"].astype(BF16), it.pop("pw").astype(BF16))
        for it in items:
            it["x_b"] = it["t_inv"].astype(BF16)
            it["m"] = _dot(it["x_b"], it.pop("nl"))
        for it in items:
            m_b = it.pop("m").astype(BF16)
            both = _dot(m_b, jnp.concatenate([m_b, it.pop("x_b")], axis=1))
            it["m2"] = both[:, :PAIR]
            it["t_inv"] = it["t_inv"] + both[:, PAIR:]
        for it in items:
            it["t_inv"] = it["t_inv"] + _dot(it.pop("m2").astype(BF16), it["t_inv"].astype(BF16))
        for it in items:
            q_ref, k_ref, v_ref = dirs[it["d"]][:3]
            rows = pl.ds(it["r0"], CHUNK)
            q_pair = _stack_pair(q_ref[rows, it["cols"]])
            k_pair = _stack_pair(k_ref[rows, it["cols"]])
            v_pair = _stack_pair(v_ref[rows, it["cols"]])
            beta, gc = it["beta"], it["gc"]
            egc = jnp.exp(gc)
            rhs = jnp.concatenate([v_pair * beta, k_pair * (beta * egc)], axis=1).astype(BF16)
            uw = _dot(it["t_inv"].astype(BF16), rhs)
            g_last = _pair_last(gc, bool(it["d"]))
            q_dec = q_pair * ((HEAD_DIM ** -0.5) * egc)
            u_scr[it["u"], it["c"]] = uw[:, :PAIR]
            wq_scr[it["u"], it["c"]] = jnp.concatenate([uw[:, PAIR:], q_dec], axis=0).astype(BF16)
            ks_scr[it["u"], it["c"]] = (k_pair * jnp.exp(g_last - gc)).astype(BF16)
        return carry

    lax.fori_loop(0, chunks // GD_CHUNKS_PER_TRIP, precompute, 0)

    def scan(i, carry):
        cs = (i, chunks - 1 - i)
        state = [s_scr[u] for u in range(len(units))]
        ws = [_dot(wq_scr[u, cs[d]], state[u].astype(BF16)) for u, (d, p) in enumerate(units)]
        v_new = [(u_scr[u, cs[d]] - ws[u][:PAIR]).astype(BF16) for u, (d, p) in enumerate(units)]
        for u, (d, p) in enumerate(units):
            s_scr[u] = state[u] * eg_scr[u, cs[d]] + _dot_tn(ks_scr[u, cs[d]], v_new[u])
        for u, (d, p) in enumerate(units):
            o = ws[u][PAIR:] + _dot(at_scr[u, cs[d]], v_new[u])
            r0 = pl.multiple_of(cs[d] * CHUNK, CHUNK)
            dirs[d][4][pl.ds(r0, CHUNK), p * PAIR:(p + 1) * PAIR] = o[:CHUNK] + o[CHUNK:]
        return carry

    lax.fori_loop(0, chunks, scan, 0)


def _gdn(q, k, v, gates, a_log, dt_bias):
    b, s, _ = q.shape
    tb = min(GDN_SEQ_BLOCK, s)
    nb = s // tb
    chunks = tb // CHUNK
    a_rows = _gate_rows(gates, GATE_GD_A, GDN_HEADS)
    bias_vec = _lane_vector([(GATE_GD_A, dt_bias)])
    alog_vec = _lane_vector([(GATE_GD_A, a_log)])
    alr = _param_rows(a_log, GDN_HEADS)
    dtr = _param_rows(dt_bias, GDN_HEADS)

    def fwd(bi, j):
        return (bi, j, 0)

    def bwd(bi, j):
        return (bi, nb - 1 - j, 0)

    seq = lambda w, im: pl.BlockSpec((None, tb, w), im)
    rows_f = pl.BlockSpec((None, None, GD_PAIRS, chunks, PAIR), lambda bi, j: (bi, 0, 0, j, 0))
    rows_b = pl.BlockSpec((None, None, GD_PAIRS, chunks, PAIR), lambda bi, j: (bi, 1, 0, nb - 1 - j, 0))
    full = lambda a: pl.BlockSpec(a.shape, lambda bi, j: (0,) * a.ndim)
    units = N_DIR * GD_PAIRS
    out = jax.ShapeDtypeStruct((b, s, GDN_WIDTH), F32)
    return pl.pallas_call(
        functools.partial(_gdn_kernel, chunks=chunks),
        out_shape=(out, out),
        grid=(b, nb),
        in_specs=[seq(GDN_WIDTH, fwd), seq(GDN_WIDTH, fwd), seq(GDN_WIDTH, fwd), seq(GATE_LANES, fwd), rows_f,
                  seq(GDN_WIDTH, bwd), seq(GDN_WIDTH, bwd), seq(GDN_WIDTH, bwd), seq(GATE_LANES, bwd), rows_b,
                  full(bias_vec), full(alog_vec), full(alr), full(dtr)],
        out_specs=(seq(GDN_WIDTH, fwd), seq(GDN_WIDTH, bwd)),
        scratch_shapes=[pltpu.VMEM((units, PAIR, PAIR), F32),
                        pltpu.VMEM((units, chunks, 1, PAIR), F32),
                        pltpu.VMEM((units, chunks, 1, PAIR), F32),
                        pltpu.VMEM((units, chunks, PAIR, PAIR), F32),
                        pltpu.VMEM((units, chunks, 2 * PAIR, PAIR), BF16),
                        pltpu.VMEM((units, chunks, PAIR, PAIR), BF16),
                        pltpu.VMEM((units, chunks, PAIR, PAIR), BF16)],
        compiler_params=_params("parallel", "arbitrary"),
        name="gated_deltanet",
    )(q, k, v, gates, a_rows, q, k, v, gates, a_rows, bias_vec, alog_vec, alr, dtr)


def _out_proj_kernel(x_ref, na_ref, hf_ref, hb_ref, mo_ref, mw_ref, of_ref, ob_ref, gz_ref, gw_ref,
                     w_ref, o_ref):
    hs = hf_ref[...] + hb_ref[...]
    y_ml = hs * lax.rsqrt(_head_sumsq(hs) * (1.0 / HEAD_DIM) + EPS) * mw_ref[...] * _sigmoid(mo_ref[...])
    os_ = of_ref[...] + ob_ref[...]
    z = gz_ref[...]
    y_gd = os_ * lax.rsqrt(_head_sumsq(os_) * (1.0 / HEAD_DIM) + EPS) * gw_ref[...] * (z * _sigmoid(z))
    acc = x_ref[...] + _dot(na_ref[...], w_ref[0:NA_WIDTH, :])
    acc = acc + _dot(y_ml.astype(BF16), w_ref[NA_WIDTH:NA_WIDTH + ML_WIDTH, :])
    acc = acc + _dot(y_gd.astype(BF16), w_ref[NA_WIDTH + ML_WIDTH:, :])
    o_ref[...] = acc


def _out_proj(x2d, y_na, hf, hb, ml_o, ml_norm_w, of, ob, gd_z, gdn_norm_w, w_out):
    t = x2d.shape[0]
    tm = min(TM_ROWS, t)
    row = lambda w: pl.BlockSpec((tm, w), lambda i: (i, 0))
    const = lambda r, c: pl.BlockSpec((r, c), lambda i: (0, 0))
    return pl.pallas_call(
        _out_proj_kernel,
        out_shape=jax.ShapeDtypeStruct((t, D_MODEL), F32),
        grid=(t // tm,),
        in_specs=[row(D_MODEL), row(NA_WIDTH), row(ML_WIDTH), row(ML_WIDTH), row(ML_WIDTH), const(1, ML_WIDTH),
                  row(GDN_WIDTH), row(GDN_WIDTH), row(GDN_WIDTH), const(1, GDN_WIDTH),
                  const(D_MODEL, D_MODEL)],
        out_specs=row(D_MODEL),
        compiler_params=_params("parallel"),
        name="out_proj",
    )(x2d, y_na, hf, hb, ml_o, ml_norm_w.reshape(1, ML_WIDTH).astype(F32), of, ob, gd_z,
      gdn_norm_w.reshape(1, GDN_WIDTH).astype(F32), w_out.astype(BF16))


def _mem_kv_kernel(m_ref, nw_ref, w_ref, k_ref, v_ref):
    h = _rms(m_ref[...], nw_ref[...]).astype(BF16)
    k_ref[...] = _dot(h, w_ref[:, :D_MODEL]).astype(BF16)
    v_ref[...] = _dot(h, w_ref[:, D_MODEL:]).astype(BF16)


def _mem_kv(mem2d, norm_w, w_kv):
    t = mem2d.shape[0]
    tm = min(TM_ROWS, t)
    out = jax.ShapeDtypeStruct((t, D_MODEL), BF16)
    return pl.pallas_call(
        _mem_kv_kernel,
        out_shape=(out, out),
        grid=(t // tm,),
        in_specs=[pl.BlockSpec((tm, D_MODEL), lambda i: (i, 0)),
                  pl.BlockSpec((1, D_MODEL), lambda i: (0, 0)),
                  pl.BlockSpec((D_MODEL, 2 * D_MODEL), lambda i: (0, 0))],
        out_specs=(pl.BlockSpec((tm, D_MODEL), lambda i: (i, 0)), pl.BlockSpec((tm, D_MODEL), lambda i: (i, 0))),
        compiler_params=_params("parallel"),
        name="mem_kv",
    )(mem2d, norm_w.reshape(1, D_MODEL).astype(F32), w_kv.astype(BF16))


def _xattn_kernel(x_ref, nw_ref, wq_ref, k_ref, v_ref, wo_ref, o_ref):
    x = x_ref[...]
    q = _dot(_rms(x, nw_ref[...]).astype(BF16), wq_ref[...]).astype(BF16)
    acc = x
    for h in range(XA_HEADS):
        cols = slice(h * XA_HEAD_DIM, (h + 1) * XA_HEAD_DIM)
        s = _dot_nt(q[:, cols], k_ref[:, cols]) * (XA_HEAD_DIM ** -0.5)
        m = jnp.max(s, axis=-1, keepdims=True)
        e = jnp.exp(s - m)
        l = jnp.sum(e, axis=-1, keepdims=True)
        o_h = _dot(e.astype(BF16), v_ref[:, cols]) / l
        acc = acc + _dot(o_h.astype(BF16), wo_ref[cols, :])
    o_ref[...] = acc


def _xattn(x3d, norm_w, w_q, k, v, w_o):
    b, s, _ = x3d.shape
    tm = min(TM_ROWS, s)
    n_mem = k.shape[1]
    return pl.pallas_call(
        _xattn_kernel,
        out_shape=jax.ShapeDtypeStruct(x3d.shape, F32),
        grid=(b, s // tm),
        in_specs=[pl.BlockSpec((None, tm, D_MODEL), lambda bi, i: (bi, i, 0)),
                  pl.BlockSpec((1, D_MODEL), lambda bi, i: (0, 0)),
                  pl.BlockSpec((D_MODEL, D_MODEL), lambda bi, i: (0, 0)),
                  pl.BlockSpec((None, n_mem, D_MODEL), lambda bi, i: (bi, 0, 0)),
                  pl.BlockSpec((None, n_mem, D_MODEL), lambda bi, i: (bi, 0, 0)),
                  pl.BlockSpec((D_MODEL, D_MODEL), lambda bi, i: (0, 0))],
        out_specs=pl.BlockSpec((None, tm, D_MODEL), lambda bi, i: (bi, i, 0)),
        compiler_params=_params("parallel", "parallel"),
        name="cross_attention",
    )(x3d, norm_w.reshape(1, D_MODEL).astype(F32), w_q.astype(BF16), k, v, w_o.astype(BF16))


FF_CHUNK = 512


def _ffn_kernel(x_ref, nw_ref, w1_ref, w2_ref, fw_ref, o_ref, *, final_norm):
    x = x_ref[...]
    h = _rms(x, nw_ref[...]).astype(BF16)
    acc = x
    for c0 in range(0, D_FF, FF_CHUNK):
        a = jnp.maximum(_dot(h, w1_ref[:, c0:c0 + FF_CHUNK]), 0.0)
        acc = acc + _dot((a * a).astype(BF16), w2_ref[c0:c0 + FF_CHUNK, :])
    o_ref[...] = _rms(acc, fw_ref[...]) if final_norm else acc


def _ffn(x2d, norm_w, w1, w2, final_w, final_norm):
    t = x2d.shape[0]
    tm = min(TM_ROWS, t)
    return pl.pallas_call(
        functools.partial(_ffn_kernel, final_norm=final_norm),
        out_shape=jax.ShapeDtypeStruct((t, D_MODEL), F32),
        grid=(t // tm,),
        in_specs=[pl.BlockSpec((tm, D_MODEL), lambda i: (i, 0)),
                  pl.BlockSpec((1, D_MODEL), lambda i: (0, 0)),
                  pl.BlockSpec((D_MODEL, D_FF), lambda i: (0, 0), pipeline_mode=pl.Buffered(1)),
                  pl.BlockSpec((D_FF, D_MODEL), lambda i: (0, 0), pipeline_mode=pl.Buffered(1)),
                  pl.BlockSpec((1, D_MODEL), lambda i: (0, 0))],
        out_specs=pl.BlockSpec((tm, D_MODEL), lambda i: (i, 0)),
        compiler_params=_params("parallel"),
        name="ffn",
    )(x2d, norm_w.reshape(1, D_MODEL).astype(F32), w1.astype(BF16), w2.astype(BF16),
      final_w.reshape(1, D_MODEL).astype(F32))


def kernel(x, mem, norm_mix_w, w_in, na_rel_bias, ml_i_bias, ml_f_bias, ml_norm_w, gdn_conv_w, gdn_a_log,
           gdn_dt_bias, gdn_norm_w, w_out, norm_xa_w, norm_mem_w, w_xq, w_xkv, w_xo, norm_ffn_w, w_ff1,
           w_ff2, norm_out_w):
    b, s, d = x.shape
    depth = w_in.shape[0]
    x2d = x.reshape(b * s, d).astype(F32)
    mem2d = mem.reshape(-1, d).astype(F32)
    for l in range(depth):
        (na_q, na_k, na_v, ml_q, ml_k, ml_v, ml_o, gd_qkv, gd_z, gates) = _in_proj(
            x2d, norm_mix_w[l].astype(F32), _permute_w_in(w_in[l]))
        seq = lambda a: a.reshape(b, s, a.shape[-1])
        gates3 = seq(gates)
        y_na = _neighbourhood_attention(seq(na_q), seq(na_k), seq(na_v), _na_bias_table(na_rel_bias[l]))
        hf, hb = _mlstm(seq(ml_q), seq(ml_k), seq(ml_v), gates3, ml_i_bias[l], ml_f_bias[l])
        gq, gk, gv = _gdn_prep(seq(gd_qkv), gdn_conv_w[l])
        of, ob = _gdn(gq, gk, gv, gates3, gdn_a_log[l], gdn_dt_bias[l])
        flat = lambda a: a.reshape(b * s, a.shape[-1])
        x2d = _out_proj(x2d, flat(y_na), flat(hf), flat(hb), ml_o, ml_norm_w[l], flat(of), flat(ob), gd_z,
                        gdn_norm_w[l], w_out[l])
        mk, mv = _mem_kv(mem2d, norm_mem_w[l], w_xkv[l])
        n_mem = mem.shape[1]
        x2d = _xattn(x2d.reshape(b, s, d), norm_xa_w[l], w_xq[l], mk.reshape(b, n_mem, d),
                     mv.reshape(b, n_mem, d), w_xo[l]).reshape(b * s, d)
        x2d = _ffn(x2d, norm_ffn_w[l], w_ff1[l], w_ff2[l], norm_out_w, final_norm=(l == depth - 1))
    return x2d.reshape(b, s, d).astype(x.dtype)
```

```python
import functools

import numpy as np
import jax
import jax.numpy as jnp
from jax import lax
from jax.experimental import pallas as pl
from jax.experimental.pallas import tpu as pltpu

F32 = jnp.float32
BF16 = jnp.bfloat16

D_MODEL = 1024
HEAD_DIM = 64
GRID_W = 64
NA_HEADS = 6
NA_WIN_ROWS = 8
NA_WIN_COLS = 16
ML_HEADS = 4
GDN_HEADS = 6
CONV_K = 5
N_DIR = 2
XA_HEADS = 4
XA_HEAD_DIM = D_MODEL // XA_HEADS
D_FF = 4 * D_MODEL
NA_WIDTH = NA_HEADS * HEAD_DIM
ML_WIDTH = ML_HEADS * HEAD_DIM
GDN_WIDTH = GDN_HEADS * HEAD_DIM
EPS = 1e-6
CHUNK = 64
PAIR = 2 * HEAD_DIM
NEG = -1e30

IN_SIZES = (NA_WIDTH, NA_WIDTH, NA_WIDTH,
            ML_WIDTH, ML_WIDTH, ML_WIDTH, ML_WIDTH, N_DIR * ML_HEADS, N_DIR * ML_HEADS,
            GDN_WIDTH, GDN_WIDTH, GDN_WIDTH, GDN_WIDTH, N_DIR * GDN_HEADS, N_DIR * GDN_HEADS)

GATE_ML_I = 0
GATE_ML_F = GATE_ML_I + N_DIR * ML_HEADS
GATE_GD_B = GATE_ML_F + N_DIR * ML_HEADS
GATE_GD_A = GATE_GD_B + N_DIR * GDN_HEADS
GATE_USED = GATE_GD_A + N_DIR * GDN_HEADS
GATE_LANES = 128

V7X_VMEM_LIMIT = 56 * 1024 * 1024
MXU_COLS = 256

TM_ROWS = 512
NA_ROWS_PER_STEP = 8
SEQ_BLOCK = 1024
GDN_SEQ_BLOCK = 1024


def _params(*sem):
    return pltpu.CompilerParams(dimension_semantics=sem, vmem_limit_bytes=V7X_VMEM_LIMIT)


def _dot(a, b):
    return jnp.dot(a, b, preferred_element_type=F32)


def _dot_nt(a, b):
    return lax.dot_general(a, b, (((1,), (1,)), ((), ())), preferred_element_type=F32)


def _dot_tn(a, b):
    return lax.dot_general(a, b, (((0,), (0,)), ((), ())), preferred_element_type=F32)


def _split3(x):
    x1 = x.astype(BF16)
    r1 = x - x1.astype(F32)
    x2 = r1.astype(BF16)
    x3 = (r1 - x2.astype(F32)).astype(BF16)
    return x1, x2, x3


def _sum_right(x, m01):
    x1 = x.astype(BF16)
    x2 = (x - x1.astype(F32)).astype(BF16)
    return _dot(x1, m01) + _dot(x2, m01)


def _exact_left(m01, x):
    x1, x2, x3 = _split3(x)
    return _dot(m01, x1) + _dot(m01, x2) + _dot(m01, x3)


def _exact_right(x, m01):
    x1, x2, x3 = _split3(x)
    return _dot(x1, m01) + _dot(x2, m01) + _dot(x3, m01)


def _rms(x, w):
    ms = jnp.mean(x * x, axis=-1, keepdims=True)
    return x * lax.rsqrt(ms + EPS) * w


def _softplus(x):
    return jnp.maximum(x, 0.0) + jnp.log1p(jnp.exp(-jnp.abs(x)))


def _sigmoid(x):
    return 1.0 / (1.0 + jnp.exp(-x))


def _segment_mean_matrix(width):
    r = lax.broadcasted_iota(jnp.int32, (width, width), 0) // HEAD_DIM
    c = lax.broadcasted_iota(jnp.int32, (width, width), 1) // HEAD_DIM
    return jnp.where(r == c, 1.0, 0.0).astype(BF16)


def _head_sumsq(t):
    return _sum_right(t * t, _segment_mean_matrix(t.shape[-1]))


def _aligned(x, m):
    return x if isinstance(x, int) else pl.multiple_of(x, m)


def _pair_masks(rev):
    r = lax.broadcasted_iota(jnp.int32, (PAIR, PAIR), 0)
    c = lax.broadcasted_iota(jnp.int32, (PAIR, PAIR), 1)
    same = (r // HEAD_DIM) == (c // HEAD_DIM)
    t, s = r % HEAD_DIM, c % HEAD_DIM
    if rev:
        return same & (s >= t), same & (s > t)
    return same & (s <= t), same & (s < t)


def _chunk_tri(rev):
    t = lax.broadcasted_iota(jnp.int32, (CHUNK, CHUNK), 0)
    s = lax.broadcasted_iota(jnp.int32, (CHUNK, CHUNK), 1)
    return jnp.where((s >= t) if rev else (s <= t), 1.0, 0.0).astype(BF16)


def _stack_pair(x2):
    lane = lax.broadcasted_iota(jnp.int32, x2.shape, 1)
    zero = jnp.zeros_like(x2)
    return jnp.concatenate([jnp.where(lane < HEAD_DIM, x2, zero),
                            jnp.where(lane >= HEAD_DIM, x2, zero)], axis=0)


def _pair_col(tile, lane0):
    return jnp.concatenate([tile[:, lane0:lane0 + 1], tile[:, lane0 + 1:lane0 + 2]], axis=0)


def _pair_last(col, rev):
    i0 = 0 if rev else CHUNK - 1
    a = jnp.broadcast_to(col[i0:i0 + 1, :], (CHUNK, 1))
    b = jnp.broadcast_to(col[CHUNK + i0:CHUNK + i0 + 1, :], (CHUNK, 1))
    return jnp.concatenate([a, b], axis=0)


IN_SEGMENTS = (
    (NA_WIDTH, BF16), (NA_WIDTH, BF16), (NA_WIDTH, BF16),
    (ML_WIDTH, BF16), (ML_WIDTH, BF16), (ML_WIDTH, BF16), (ML_WIDTH, F32),
    (3 * GDN_WIDTH, F32), (GDN_WIDTH, F32), (GATE_LANES, F32))
IN_COLS = sum(w for w, _ in IN_SEGMENTS)
IN_DOT_COLS = 3 * MXU_COLS


def _in_proj_kernel(x_ref, nw_ref, w_ref, *out_refs):
    h = _rms(x_ref[...], nw_ref[...]).astype(BF16)
    starts = np.cumsum([0] + [w for w, _ in IN_SEGMENTS])
    for c0 in range(0, IN_COLS, IN_DOT_COLS):
        c1 = min(c0 + IN_DOT_COLS, IN_COLS)
        acc = _dot(h, w_ref[:, c0:c1])
        for o_ref, s0, s1 in zip(out_refs, starts[:-1], starts[1:]):
            a, b = max(c0, int(s0)), min(c1, int(s1))
            if a < b:
                o_ref[:, a - int(s0):b - int(s0)] = acc[:, a - c0:b - c0].astype(o_ref.dtype)


def _in_proj(x2d, norm_w, w_perm):
    t = x2d.shape[0]
    tm = min(TM_ROWS, t)
    outs = tuple(jax.ShapeDtypeStruct((t, w), dt) for w, dt in IN_SEGMENTS)
    return pl.pallas_call(
        _in_proj_kernel,
        out_shape=outs,
        grid=(t // tm,),
        in_specs=[pl.BlockSpec((tm, D_MODEL), lambda i: (i, 0)),
                  pl.BlockSpec((1, D_MODEL), lambda i: (0, 0)),
                  pl.BlockSpec((D_MODEL, IN_COLS), lambda i: (0, 0))],
        out_specs=tuple(pl.BlockSpec((tm, w), lambda i: (i, 0)) for w, _ in IN_SEGMENTS),
        compiler_params=_params("parallel"),
        name="in_proj",
    )(x2d, norm_w.reshape(1, D_MODEL), w_perm)


def _permute_w_in(w):
    parts = jnp.split(w, np.cumsum(IN_SIZES)[:-1], axis=-1)
    (na_q, na_k, na_v, ml_q, ml_k, ml_v, ml_o, ml_i, ml_f, gd_q, gd_k, gd_v, gd_z, gd_b, gd_a) = parts
    pad = jnp.zeros((w.shape[0], GATE_LANES - GATE_USED), w.dtype)
    return jnp.concatenate([na_q, na_k, na_v, ml_q, ml_k, ml_v, ml_o, gd_q, gd_k, gd_v, gd_z,
                            ml_i, ml_f, gd_b, gd_a, pad], axis=-1).astype(BF16)


NA_BIAS_ROWS = 96


def _na_bias_kernel(rb_ref, o_ref):
    n = GRID_W * GRID_W
    dc = lax.broadcasted_iota(jnp.int32, (GATE_LANES, n), 0)
    col = lax.broadcasted_iota(jnp.int32, (GATE_LANES, n), 1)
    q, kc = col // GRID_W, col % GRID_W
    c0 = jnp.clip(q - NA_WIN_COLS // 2, 0, GRID_W - NA_WIN_COLS)
    valid = (kc >= c0) & (kc < c0 + NA_WIN_COLS)
    onehot = jnp.where(valid & (kc - q + (NA_WIN_COLS - 1) == dc), 1.0, 0.0).astype(BF16)
    o_ref[...] = jnp.where(valid[0:1, :], _exact_right(rb_ref[...], onehot), NEG)


def _na_bias_table(rel_bias):
    nh, ndr, ndc = rel_bias.shape
    rb = jnp.zeros((NA_BIAS_ROWS, GATE_LANES), F32).at[:nh * ndr, :ndc].set(
        rel_bias.astype(F32).reshape(nh * ndr, ndc))
    band = pl.pallas_call(
        _na_bias_kernel,
        out_shape=jax.ShapeDtypeStruct((NA_BIAS_ROWS, GRID_W * GRID_W), F32),
        name="na_bias_expand",
    )(rb)
    band = band[:nh * ndr].reshape(nh, ndr, GRID_W, GRID_W)
    first = np.clip(np.arange(ndr + 1) - 1, 0, ndr - 1)
    second = np.clip(np.arange(ndr + 1), 0, ndr - 1)
    tab = jnp.concatenate([band[:, first], band[:, second]], axis=-1)
    tab = tab.reshape(NA_HEADS // 2, 2, ndr + 1, GRID_W, PAIR)
    return jnp.moveaxis(tab, 1, 2).reshape(NA_HEADS // 2, ndr + 1, 2 * GRID_W, PAIR)


NA_ITEM_ROWS = 4
NA_KEY_ROWS = NA_ITEM_ROWS + NA_WIN_ROWS


def _na_kernel(q_ref, k_ref, v_ref, bias_ref, o_ref, *, rows, rows_per_step):
    j = pl.program_id(1)
    nkeys = NA_KEY_ROWS * GRID_W
    lane = lax.broadcasted_iota(jnp.int32, (GRID_W, PAIR), 1)
    lane_blk = lax.broadcasted_iota(jnp.int32, (PAIR, PAIR), 1)
    items = []
    for g in range(rows_per_step // NA_ITEM_ROWS):
        r_first = j * rows_per_step + g * NA_ITEM_ROWS
        kr0 = jnp.clip(r_first - NA_WIN_ROWS // 2, 0, rows - NA_KEY_ROWS)
        koff = pl.multiple_of(kr0 * GRID_W, GRID_W)
        entry, pen = {}, {}
        for i in range(NA_ITEM_ROWS):
            r = r_first + i
            r0 = jnp.clip(r - NA_WIN_ROWS // 2, 0, rows - NA_WIN_ROWS)
            for jp in range(NA_KEY_ROWS // 2):
                key_a = kr0 + 2 * jp
                ok_a = (key_a >= r0) & (key_a < r0 + NA_WIN_ROWS)
                ok_b = (key_a + 1 >= r0) & (key_a + 1 < r0 + NA_WIN_ROWS)
                entry[i, jp] = jnp.clip(key_a - r + (NA_WIN_ROWS - 1), -1, 2 * NA_WIN_ROWS - 2) + 1
                pen[i, jp] = jnp.where(lane_blk < GRID_W, jnp.where(ok_a, 0.0, NEG), jnp.where(ok_b, 0.0, NEG))
        for p in range(NA_HEADS // 2):
            items.append(dict(g=g, p=p, koff=koff, entry=entry, pen=pen, cols=slice(p * PAIR, (p + 1) * PAIR)))
    for it in items:
        q0 = it["g"] * NA_ITEM_ROWS * GRID_W
        q_lhs = jnp.concatenate(
            [_stack_pair(q_ref[q0 + i * GRID_W:q0 + (i + 1) * GRID_W, it["cols"]]) for i in range(NA_ITEM_ROWS)],
            axis=0)
        q_lhs = (q_lhs.astype(F32) * (HEAD_DIM ** -0.5)).astype(BF16)
        it["s"] = _dot_nt(q_lhs, k_ref[pl.ds(it["koff"], nkeys), it["cols"]])
    for it in items:
        s = it.pop("s")
        blocks = []
        for i in range(NA_ITEM_ROWS):
            blocks.append(jnp.concatenate(
                [s[i * PAIR:(i + 1) * PAIR, jp * PAIR:(jp + 1) * PAIR] + bias_ref[it["p"], it["entry"][i, jp]]
                 + it["pen"][i, jp] for jp in range(NA_KEY_ROWS // 2)], axis=1))
        s = jnp.concatenate(blocks, axis=0)
        e = jnp.exp(s - jnp.max(s, axis=-1, keepdims=True))
        it["l"] = jnp.sum(e, axis=-1, keepdims=True)
        it["o"] = _dot(e.astype(BF16), v_ref[pl.ds(it["koff"], nkeys), it["cols"]])
    for it in items:
        o = it.pop("o") * (1.0 / it.pop("l"))
        q0 = it["g"] * NA_ITEM_ROWS * GRID_W
        for i in range(NA_ITEM_ROWS):
            blk = o[i * PAIR:(i + 1) * PAIR]
            o_ref[q0 + i * GRID_W:q0 + (i + 1) * GRID_W, it["cols"]] = jnp.where(
                lane < HEAD_DIM, blk[:GRID_W], blk[GRID_W:]).astype(o_ref.dtype)


def _neighbourhood_attention(q, k, v, bias_tab):
    b, s, _ = q.shape
    rows = s // GRID_W
    assert rows >= NA_KEY_ROWS
    rps = min(NA_ROWS_PER_STEP, rows)
    tq = rps * GRID_W
    return pl.pallas_call(
        functools.partial(_na_kernel, rows=rows, rows_per_step=rps),
        out_shape=jax.ShapeDtypeStruct((b, s, NA_WIDTH), BF16),
        grid=(b, rows // rps),
        in_specs=[pl.BlockSpec((None, tq, NA_WIDTH), lambda bi, j: (bi, j, 0)),
                  pl.BlockSpec((None, s, NA_WIDTH), lambda bi, j: (bi, 0, 0)),
                  pl.BlockSpec((None, s, NA_WIDTH), lambda bi, j: (bi, 0, 0)),
                  pl.BlockSpec(bias_tab.shape, lambda bi, j: (0, 0, 0, 0))],
        out_specs=pl.BlockSpec((None, tq, NA_WIDTH), lambda bi, j: (bi, j, 0)),
        compiler_params=_params("parallel", "arbitrary"),
        name="neighbourhood_attention",
    )(q, k, v, bias_tab)


def _gate_rows(gates, lane0, heads):
    b, s, _ = gates.shape
    g = gates[:, :, lane0:lane0 + N_DIR * heads].reshape(b, s // CHUNK, CHUNK, N_DIR, heads // 2, 2)
    return jnp.transpose(g, (0, 3, 4, 1, 5, 2)).reshape(b, N_DIR, heads // 2, s // CHUNK, PAIR)


def _param_rows(p, heads):
    return jnp.repeat(p.astype(F32).reshape(N_DIR, heads // 2, 2), HEAD_DIM, axis=-1).reshape(
        N_DIR, heads // 2, 1, PAIR)


def _lane_vector(entries):
    v = jnp.zeros((GATE_LANES,), F32)
    for lane0, vals in entries:
        v = lax.dynamic_update_slice(v, vals.astype(F32).reshape(-1), (lane0,))
    return v.reshape(1, GATE_LANES)


def _pair_cumsum_matrix(rev):
    valid, _ = _pair_masks(rev)
    r = lax.broadcasted_iota(jnp.int32, (PAIR, PAIR), 0)
    c = lax.broadcasted_iota(jnp.int32, (PAIR, PAIR), 1)
    same = (r // HEAD_DIM) == (c // HEAD_DIM)
    sp, s = r % HEAD_DIM, c % HEAD_DIM
    del valid
    return jnp.where(same & ((sp >= s) if rev else (sp <= s)), 1.0, 0.0).astype(BF16)


ML_PAIRS = ML_HEADS // 2
ML_UNITS = N_DIR * ML_PAIRS
ML_CHUNKS_PER_TRIP = 2
ML_STEPS_PER_TRIP = 4
OS_ROWS = 16


def _outer_sum_operands(col_term, row_term):
    ones = jnp.ones((3, PAIR), BF16)
    zeros = jnp.zeros((OS_ROWS - 6, PAIR), BF16)
    a = jnp.concatenate(list(_split3(col_term)) + [ones, zeros], axis=0)
    b = jnp.concatenate([ones] + list(_split3(row_term)) + [zeros], axis=0)
    return a, b


def _mlstm_kernel(qf_ref, kf_ref, vf_ref, if_ref, ff_ref,
                  qb_ref, kb_ref, vb_ref, ib_ref, fb_ref,
                  ibr_ref, fbr_ref,
                  hf_ref, hb_ref,
                  c_scr, m_scr, cs_scr, rt_scr, ai_scr, em_scr, we_scr, a_scr, g_scr, vt_scr, ut_scr, *, chunks):
    j = pl.program_id(1)

    @pl.when(j == 0)
    def _():
        c_scr[...] = jnp.zeros_like(c_scr)
        m_scr[...] = jnp.zeros_like(m_scr)

    lane = lax.broadcasted_iota(jnp.int32, (chunks, PAIR), 1)
    pos = lane % HEAD_DIM
    lo = lane < HEAD_DIM
    for d, (i_ref, f_ref) in enumerate(((if_ref, ff_ref), (ib_ref, fb_ref))):
        rev = bool(d)
        ucum = _pair_cumsum_matrix(rev)
        last = 0 if rev else CHUNK - 1
        for p in range(ML_PAIRS):
            u = d * ML_PAIRS + p
            logf = -_softplus(-(f_ref[p] + fbr_ref[d, p]))
            bcum = _exact_right(logf, ucum)
            cs = i_ref[p] + ibr_ref[d, p] - bcum
            bl = jnp.where(lo, bcum[:, last:last + 1], bcum[:, HEAD_DIM + last:HEAD_DIM + last + 1])
            cm = cs
            for k in (1, 2, 4, 8, 16, 32):
                if rev:
                    cm = jnp.where(pos < HEAD_DIM - k, jnp.maximum(cm, pltpu.roll(cm, PAIR - k, axis=1)), cm)
                else:
                    cm = jnp.where(pos >= k, jnp.maximum(cm, pltpu.roll(cm, k, axis=1)), cm)
            w = bl + cs
            m_loc = jnp.where(lo, jnp.max(jnp.where(lo, w, NEG), axis=-1, keepdims=True),
                              jnp.max(jnp.where(lo, NEG, w), axis=-1, keepdims=True))
            m = m_scr[u]
            m_prev, a_rows, g_rows = [None] * chunks, [None] * chunks, [None] * chunks
            for c in (range(chunks - 1, -1, -1) if rev else range(chunks)):
                m_prev[c] = m
                m_new = jnp.maximum(bl[c:c + 1] + m, m_loc[c:c + 1])
                a_rows[c] = jnp.exp(bl[c:c + 1] + m - m_new)
                g_rows[c] = jnp.exp(m_loc[c:c + 1] - m_new)
                m = m_new
            m_scr[u] = m
            inter = bcum + jnp.concatenate(m_prev, axis=0)
            m_t = jnp.maximum(bcum + cm, inter)
            cs_scr[u] = cs
            rt_scr[u] = bcum - m_t
            ai_scr[u] = jnp.exp(inter - m_t)
            em_scr[u] = jnp.exp(-m_t)
            we_scr[u] = jnp.exp(w - m_loc)
            a_scr[u] = jnp.concatenate(a_rows, axis=0)
            g_scr[u] = jnp.concatenate(g_rows, axis=0)

    dirs = ((qf_ref, kf_ref, vf_ref, hf_ref), (qb_ref, kb_ref, vb_ref, hb_ref))
    units = [(d, p) for d in range(N_DIR) for p in range(ML_PAIRS)]
    lane64 = lax.broadcasted_iota(jnp.int32, (CHUNK, PAIR), 1)
    one_hi = jnp.where(lane64 == HEAD_DIM, 1.0, 0.0).astype(BF16)
    one_lo = jnp.where(lane64 == 0, 1.0, 0.0).astype(BF16)
    lane_row = lax.broadcasted_iota(jnp.int32, (1, PAIR), 1)

    def key_tile(d, p, r0):
        k_pair = _stack_pair(dirs[d][1][pl.ds(r0, CHUNK), p * PAIR:(p + 1) * PAIR])
        return (k_pair.astype(F32) * (HEAD_DIM ** -0.5)).astype(BF16)

    def contributions(t, carry):
        for cc in range(ML_CHUNKS_PER_TRIP):
            c = t * ML_CHUNKS_PER_TRIP + cc
            r0 = pl.multiple_of(c * CHUNK, CHUNK)
            for u, (d, p) in enumerate(units):
                v2 = dirs[d][2][pl.ds(r0, CHUNK), p * PAIR:(p + 1) * PAIR]
                v_ext = jnp.concatenate([jnp.where(lane64 < HEAD_DIM, v2, one_hi),
                                         jnp.where(lane64 >= HEAD_DIM, v2, one_lo)], axis=0)
                v_t = v_ext.astype(F32).T
                vt_scr[u, c] = v_t.astype(BF16)
                ut_scr[u, c] = _dot((v_t * we_scr[u, pl.ds(c, 1), :]).astype(BF16), key_tile(d, p, r0))
        return carry

    lax.fori_loop(0, chunks // ML_CHUNKS_PER_TRIP, contributions, 0)

    def outputs(t, carry):
        masks = [_pair_masks(True)[0], _pair_masks(False)[0]]
        items = []
        for u, (d, p) in enumerate(units):
            ct = c_scr[u]
            for k in range(ML_STEPS_PER_TRIP):
                step = t * ML_STEPS_PER_TRIP + k
                c = (chunks - 1 - step) if d else step
                r0 = pl.multiple_of(c * CHUNK, CHUNK)
                row = pl.ds(c, 1)
                items.append(dict(u=u, d=d, p=p, c=c, r0=r0, row=row, ct=ct.astype(BF16)))
                ct = a_scr[u, row, :] * ct + g_scr[u, row, :] * ut_scr[u, c]
            c_scr[u] = ct
        for it in items:
            u, d, p, r0, row = it["u"], it["d"], it["p"], it["r0"], it["row"]
            q_pair = _stack_pair(dirs[d][0][pl.ds(r0, CHUNK), p * PAIR:(p + 1) * PAIR])
            os_a, os_b = _outer_sum_operands(cs_scr[u, row, :], rt_scr[u, row, :])
            it["gram"] = _dot_nt(key_tile(d, p, r0), q_pair)
            it["osum"] = _dot_tn(os_a, os_b)
            it["st"] = _dot_nt(it.pop("ct"), q_pair)
        for it in items:
            u, c = it["u"], it["c"]
            s_t = it.pop("gram") * jnp.exp(jnp.where(masks[it["d"]], it.pop("osum"), NEG))
            it["intra"] = jnp.sum(s_t, axis=0, keepdims=True)
            it["num"] = _dot(vt_scr[u, c], s_t.astype(BF16))
        for it in items:
            u, d, p, row = it["u"], it["d"], it["p"], it["row"]
            a_inter = ai_scr[u, row, :]
            st = it.pop("st")
            den = a_inter * jnp.where(lane_row < HEAD_DIM, st[HEAD_DIM:HEAD_DIM + 1], st[0:1]) + it.pop("intra")
            out_t = (it.pop("num") + a_inter * st) * (1.0 / jnp.maximum(jnp.abs(den), em_scr[u, row, :]))
            out = out_t.T
            dirs[d][3][pl.ds(it["r0"], CHUNK), p * PAIR:(p + 1) * PAIR] = jnp.where(
                lane64 < HEAD_DIM, out[:CHUNK], out[CHUNK:])
        return carry

    lax.fori_loop(0, chunks // ML_STEPS_PER_TRIP, outputs, 0)


def _mlstm(q, k, v, gates, i_bias, f_bias):
    b, s, _ = q.shape
    tb = min(SEQ_BLOCK, s)
    nb = s // tb
    chunks = tb // CHUNK
    i_rows = _gate_rows(gates, GATE_ML_I, ML_HEADS)
    f_rows = _gate_rows(gates, GATE_ML_F, ML_HEADS)
    ibr = _param_rows(i_bias, ML_HEADS)
    fbr = _param_rows(f_bias, ML_HEADS)

    def fwd(bi, j):
        return (bi, j, 0)

    def bwd(bi, j):
        return (bi, nb - 1 - j, 0)

    seq = lambda w, im: pl.BlockSpec((None, tb, w), im)
    rows_f = pl.BlockSpec((None, None, ML_PAIRS, chunks, PAIR), lambda bi, j: (bi, 0, 0, j, 0))
    rows_b = pl.BlockSpec((None, None, ML_PAIRS, chunks, PAIR), lambda bi, j: (bi, 1, 0, nb - 1 - j, 0))
    full = lambda a: pl.BlockSpec(a.shape, lambda bi, j: (0,) * a.ndim)
    out = jax.ShapeDtypeStruct((b, s, ML_WIDTH), F32)
    rows = pltpu.VMEM((ML_UNITS, chunks, PAIR), F32)
    return pl.pallas_call(
        functools.partial(_mlstm_kernel, chunks=chunks),
        out_shape=(out, out),
        grid=(b, nb),
        in_specs=[seq(ML_WIDTH, fwd), seq(ML_WIDTH, fwd), seq(ML_WIDTH, fwd), rows_f, rows_f,
                  seq(ML_WIDTH, bwd), seq(ML_WIDTH, bwd), seq(ML_WIDTH, bwd), rows_b, rows_b,
                  full(ibr), full(fbr)],
        out_specs=(seq(ML_WIDTH, fwd), seq(ML_WIDTH, bwd)),
        scratch_shapes=[pltpu.VMEM((ML_UNITS, PAIR, PAIR), F32),
                        pltpu.VMEM((ML_UNITS, 1, PAIR), F32),
                        rows, rows, rows, rows, rows, rows, rows,
                        pltpu.VMEM((ML_UNITS, chunks, PAIR, PAIR), BF16),
                        pltpu.VMEM((ML_UNITS, chunks, PAIR, PAIR), F32)],
        compiler_params=_params("parallel", "arbitrary"),
        name="mlstm",
    )(q, k, v, i_rows, f_rows, q, k, v, i_rows, f_rows, ibr, fbr)


GD_PAIRS = GDN_HEADS // 2
CONV_HALO = 8
PREP_ROWS = 128


def _gdn_prep_kernel(x_ref, prev_ref, next_ref, w_ref, q_ref, k_ref, v_ref, ext_scr, *, tb):
    j = pl.program_id(1)
    nb = pl.num_programs(1)
    ext_scr[0:CONV_HALO, :] = jnp.where(j > 0, prev_ref[...], 0.0)
    ext_scr[CONV_HALO:CONV_HALO + tb, :] = x_ref[...]
    ext_scr[CONV_HALO + tb:, :] = jnp.where(j < nb - 1, next_ref[...], 0.0)
    base = CONV_HALO - CONV_K // 2
    seg = _segment_mean_matrix(PAIR)
    taps = [w_ref[t:t + 1, :] for t in range(CONV_K)]

    def rows_block(i, carry):
        r0 = pl.multiple_of(i * PREP_ROWS, PREP_ROWS)
        for ct in range(3 * GDN_WIDTH // PAIR):
            cols = slice(ct * PAIR, (ct + 1) * PAIR)
            slab = ext_scr[pl.ds(r0, PREP_ROWS + 2 * CONV_HALO), cols]
            y = slab[base:base + PREP_ROWS] * taps[0][:, cols]
            for t in range(1, CONV_K):
                y = y + slab[base + t:base + t + PREP_ROWS] * taps[t][:, cols]
            y = y * _sigmoid(y)
            which, off = divmod(ct * PAIR, GDN_WIDTH)
            out_ref = (q_ref, k_ref, v_ref)[which]
            if which < 2:
                y = y * lax.rsqrt(_sum_right(y * y, seg) + EPS)
            out_ref[pl.ds(r0, PREP_ROWS), off:off + PAIR] = y
        return carry

    lax.fori_loop(0, tb // PREP_ROWS, rows_block, 0)


def _gdn_prep(qkv, conv_w):
    b, s, w = qkv.shape
    tb = min(SEQ_BLOCK, s)
    nb = s // tb
    hb = tb // CONV_HALO
    last = s // CONV_HALO - 1
    out = jax.ShapeDtypeStruct((b, s, GDN_WIDTH), F32)
    return pl.pallas_call(
        functools.partial(_gdn_prep_kernel, tb=tb),
        out_shape=(out, out, out),
        grid=(b, nb),
        in_specs=[pl.BlockSpec((None, tb, w), lambda bi, j: (bi, j, 0)),
                  pl.BlockSpec((None, CONV_HALO, w), lambda bi, j: (bi, jnp.maximum(j * hb - 1, 0), 0)),
                  pl.BlockSpec((None, CONV_HALO, w), lambda bi, j: (bi, jnp.minimum((j + 1) * hb, last), 0)),
                  pl.BlockSpec((CONV_K, w), lambda bi, j: (0, 0))],
        out_specs=tuple(pl.BlockSpec((None, tb, GDN_WIDTH), lambda bi, j: (bi, j, 0)) for _ in range(3)),
        scratch_shapes=[pltpu.VMEM((tb + 2 * CONV_HALO, w), F32)],
        compiler_params=_params("parallel", "arbitrary"),
        name="gdn_prep",
    )(qkv, qkv, qkv, conv_w.astype(F32))


GD_CHUNKS_PER_TRIP = 2
GD_INV_BLOCK = 16


def _gdn_kernel(qf_ref, kf_ref, vf_ref, gf_ref, af_ref,
                qb_ref, kb_ref, vb_ref, gb_ref, ab_ref,
                bias_ref, alog_ref, alr_ref, dtr_ref,
                of_ref, ob_ref,
                s_scr, gr_scr, eg_scr, u_scr, wq_scr, at_scr, ks_scr, *, chunks):
    j = pl.program_id(1)

    @pl.when(j == 0)
    def _():
        s_scr[...] = jnp.zeros_like(s_scr)

    bias = bias_ref[...]
    decay = -jnp.exp(alog_ref[...])
    lane_row = lax.broadcasted_iota(jnp.int32, (chunks, PAIR), 1)
    for d, a_ref in enumerate((af_ref, ab_ref)):
        ucum = _pair_cumsum_matrix(rev=bool(d))
        last = 0 if d else CHUNK - 1
        for p in range(GD_PAIRS):
            g = -jnp.exp(alr_ref[d, p]) * _softplus(a_ref[p] + dtr_ref[d, p])
            gc = _exact_right(g, ucum)
            eg = jnp.exp(jnp.where(lane_row < HEAD_DIM, gc[:, last:last + 1],
                                   gc[:, HEAD_DIM + last:HEAD_DIM + last + 1]))
            for c in range(chunks):
                gr_scr[d * GD_PAIRS + p, c] = gc[c:c + 1, :]
                eg_scr[d * GD_PAIRS + p, c] = eg[c:c + 1, :]

    r = lax.broadcasted_iota(jnp.int32, (PAIR, PAIR), 0)
    cidx = lax.broadcasted_iota(jnp.int32, (PAIR, PAIR), 1)
    eye = jnp.where(r == cidx, 1.0, 0.0).astype(F32)
    same16 = (r // GD_INV_BLOCK) == (cidx // GD_INV_BLOCK)
    dirs = ((qf_ref, kf_ref, vf_ref, gf_ref, of_ref), (qb_ref, kb_ref, vb_ref, gb_ref, ob_ref))
    units = [(d, p) for d in range(N_DIR) for p in range(GD_PAIRS)]

    def precompute(t):
        masks = [_pair_masks(False), _pair_masks(True)]
        tris = [_chunk_tri(False), _chunk_tri(True)]
        items = []
        for cc in range(GD_CHUNKS_PER_TRIP):
            step = t * GD_CHUNKS_PER_TRIP + cc
            cs = (step, chunks - 1 - step)
            r0 = [_aligned(c * CHUNK, CHUNK) for c in cs]
            for d in range(N_DIR):
                gt = dirs[d][3][pl.ds(r0[d], CHUNK), :]
                beta_all = _sigmoid(gt)
                gc_all = _exact_left(tris[d], decay * _softplus(gt + bias))
                for p in range(GD_PAIRS):
                    items.append(dict(d=d, p=p, c=cs[d], r0=r0[d], u=d * GD_PAIRS + p,
                                      cols=slice(p * PAIR, (p + 1) * PAIR),
                                      beta=_pair_col(beta_all, GATE_GD_B + d * GDN_HEADS + 2 * p),
                                      gc=_pair_col(gc_all, GATE_GD_A + d * GDN_HEADS + 2 * p)))
        for it in items:
            q_ref, k_ref = dirs[it["d"]][0], dirs[it["d"]][1]
            q_b = _stack_pair(q_ref[pl.ds(it["r0"], CHUNK), it["cols"]]).astype(BF16)
            k_b = _stack_pair(k_ref[pl.ds(it["r0"], CHUNK), it["cols"]]).astype(BF16)
            it["grams"] = _dot_nt(jnp.concatenate([q_b, k_b], axis=0), k_b)
        yield
        for it in items:
            valid, strict = masks[it["d"]]
            decay_ts = jnp.exp(jnp.where(valid, it["gc"] - gr_scr[it["u"], it["c"]], NEG))
            grams = it.pop("grams")
            at_scr[it["u"], it["c"]] = (grams[:PAIR] * (HEAD_DIM ** -0.5) * decay_ts).astype(BF16)
            neg_a = jnp.where(strict, -(grams[PAIR:] * it["beta"] * decay_ts), 0.0)
            nd = jnp.where(same16, neg_a, 0.0)
            it["nl"] = (neg_a - nd).astype(BF16)
            it["t_inv"] = eye + nd
            n_b = nd.astype(BF16)
            it["pw"] = _dot(n_b, n_b)
        yield
        for _ in range(2):
            for it in items:
                pw_b = it["pw"].astype(BF16)
                both = _dot(jnp.concatenate([it["t_inv"].astype(BF16), pw_b], axis=0), pw_b)
                it["t_inv"] = it["t_inv"] + both[:PAIR]
                it["pw"] = both[PAIR:]
        yield
        for it in items:
            it["t_inv"] = it["t_inv"] + _dot(it["t_inv"].astype(BF16), it.pop("pw").astype(BF16))
        yield
        for it in items:
            it["x_b"] = it["t_inv"].astype(BF16)
            it["m"] = _dot(it["x_b"], it.pop("nl"))
        yield
        for it in items:
            m_b = it.pop("m").astype(BF16)
            both = _dot(m_b, jnp.concatenate([m_b, it.pop("x_b")], axis=1))
            it["m2"] = both[:, :PAIR]
            it["t_inv"] = it["t_inv"] + both[:, PAIR:]
        yield
        for it in items:
            it["t_inv"] = it["t_inv"] + _dot(it.pop("m2").astype(BF16), it["t_inv"].astype(BF16))
        yield
        for it in items:
            q_ref, k_ref, v_ref = dirs[it["d"]][:3]
            rows = pl.ds(it["r0"], CHUNK)
            q_pair = _stack_pair(q_ref[rows, it["cols"]])
            k_pair = _stack_pair(k_ref[rows, it["cols"]])
            v_pair = _stack_pair(v_ref[rows, it["cols"]])
            beta, gc = it["beta"], it["gc"]
            egc = jnp.exp(gc)
            rhs = jnp.concatenate([v_pair * beta, k_pair * (beta * egc)], axis=1).astype(BF16)
            uw = _dot(it["t_inv"].astype(BF16), rhs)
            g_last = _pair_last(gc, bool(it["d"]))
            q_dec = q_pair * ((HEAD_DIM ** -0.5) * egc)
            u_scr[it["u"], it["c"]] = uw[:, :PAIR]
            wq_scr[it["u"], it["c"]] = jnp.concatenate([uw[:, PAIR:], q_dec], axis=0).astype(BF16)
            ks_scr[it["u"], it["c"]] = (k_pair * jnp.exp(g_last - gc)).astype(BF16)

    def scan(i):
        cs = (i, chunks - 1 - i)
        state = [s_scr[u] for u in range(len(units))]
        ws = [_dot(wq_scr[u, cs[d]], state[u].astype(BF16)) for u, (d, p) in enumerate(units)]
        yield
        v_new = [(u_scr[u, cs[d]] - ws[u][:PAIR]).astype(BF16) for u, (d, p) in enumerate(units)]
        for u, (d, p) in enumerate(units):
            s_scr[u] = state[u] * eg_scr[u, cs[d]] + _dot_tn(ks_scr[u, cs[d]], v_new[u])
        for u, (d, p) in enumerate(units):
            o = ws[u][PAIR:] + _dot(at_scr[u, cs[d]], v_new[u])
            r0 = _aligned(cs[d] * CHUNK, CHUNK)
            dirs[d][4][pl.ds(r0, CHUNK), p * PAIR:(p + 1) * PAIR] = o[:CHUNK] + o[CHUNK:]

    trips = chunks // GD_CHUNKS_PER_TRIP

    def scan_trip(t):
        for cc in range(GD_CHUNKS_PER_TRIP):
            yield from scan(t * GD_CHUNKS_PER_TRIP + cc)
            yield

    def fused(t, carry):
        filler = scan_trip(t - 1)
        for _ in precompute(t):
            next(filler, None)
        for _ in filler:
            pass
        return carry

    for _ in precompute(0):
        pass
    lax.fori_loop(1, trips, fused, 0)
    for _ in scan_trip(trips - 1):
        pass


def _gdn(q, k, v, gates, a_log, dt_bias):
    b, s, _ = q.shape
    tb = min(GDN_SEQ_BLOCK, s)
    nb = s // tb
    chunks = tb // CHUNK
    a_rows = _gate_rows(gates, GATE_GD_A, GDN_HEADS)
    bias_vec = _lane_vector([(GATE_GD_A, dt_bias)])
    alog_vec = _lane_vector([(GATE_GD_A, a_log)])
    alr = _param_rows(a_log, GDN_HEADS)
    dtr = _param_rows(dt_bias, GDN_HEADS)

    def fwd(bi, j):
        return (bi, j, 0)

    def bwd(bi, j):
        return (bi, nb - 1 - j, 0)

    seq = lambda w, im: pl.BlockSpec((None, tb, w), im)
    rows_f = pl.BlockSpec((None, None, GD_PAIRS, chunks, PAIR), lambda bi, j: (bi, 0, 0, j, 0))
    rows_b = pl.BlockSpec((None, None, GD_PAIRS, chunks, PAIR), lambda bi, j: (bi, 1, 0, nb - 1 - j, 0))
    full = lambda a: pl.BlockSpec(a.shape, lambda bi, j: (0,) * a.ndim)
    units = N_DIR * GD_PAIRS
    out = jax.ShapeDtypeStruct((b, s, GDN_WIDTH), F32)
    return pl.pallas_call(
        functools.partial(_gdn_kernel, chunks=chunks),
        out_shape=(out, out),
        grid=(b, nb),
        in_specs=[seq(GDN_WIDTH, fwd), seq(GDN_WIDTH, fwd), seq(GDN_WIDTH, fwd), seq(GATE_LANES, fwd), rows_f,
                  seq(GDN_WIDTH, bwd), seq(GDN_WIDTH, bwd), seq(GDN_WIDTH, bwd), seq(GATE_LANES, bwd), rows_b,
                  full(bias_vec), full(alog_vec), full(alr), full(dtr)],
        out_specs=(seq(GDN_WIDTH, fwd), seq(GDN_WIDTH, bwd)),
        scratch_shapes=[pltpu.VMEM((units, PAIR, PAIR), F32),
                        pltpu.VMEM((units, chunks, 1, PAIR), F32),
                        pltpu.VMEM((units, chunks, 1, PAIR), F32),
                        pltpu.VMEM((units, chunks, PAIR, PAIR), F32),
                        pltpu.VMEM((units, chunks, 2 * PAIR, PAIR), BF16),
                        pltpu.VMEM((units, chunks, PAIR, PAIR), BF16),
                        pltpu.VMEM((units, chunks, PAIR, PAIR), BF16)],
        compiler_params=_params("parallel", "arbitrary"),
        name="gated_deltanet",
    )(q, k, v, gates, a_rows, q, k, v, gates, a_rows, bias_vec, alog_vec, alr, dtr)


def _out_proj_kernel(x_ref, na_ref, hf_ref, hb_ref, mo_ref, mw_ref, of_ref, ob_ref, gz_ref, gw_ref,
                     w_ref, o_ref):
    hs = hf_ref[...] + hb_ref[...]
    y_ml = hs * lax.rsqrt(_head_sumsq(hs) * (1.0 / HEAD_DIM) + EPS) * mw_ref[...] * _sigmoid(mo_ref[...])
    os_ = of_ref[...] + ob_ref[...]
    z = gz_ref[...]
    y_gd = os_ * lax.rsqrt(_head_sumsq(os_) * (1.0 / HEAD_DIM) + EPS) * gw_ref[...] * (z * _sigmoid(z))
    acc = x_ref[...] + _dot(na_ref[...], w_ref[0:NA_WIDTH, :])
    acc = acc + _dot(y_ml.astype(BF16), w_ref[NA_WIDTH:NA_WIDTH + ML_WIDTH, :])
    acc = acc + _dot(y_gd.astype(BF16), w_ref[NA_WIDTH + ML_WIDTH:, :])
    o_ref[...] = acc


def _out_proj(x2d, y_na, hf, hb, ml_o, ml_norm_w, of, ob, gd_z, gdn_norm_w, w_out):
    t = x2d.shape[0]
    tm = min(TM_ROWS, t)
    row = lambda w: pl.BlockSpec((tm, w), lambda i: (i, 0))
    const = lambda r, c: pl.BlockSpec((r, c), lambda i: (0, 0))
    return pl.pallas_call(
        _out_proj_kernel,
        out_shape=jax.ShapeDtypeStruct((t, D_MODEL), F32),
        grid=(t // tm,),
        in_specs=[row(D_MODEL), row(NA_WIDTH), row(ML_WIDTH), row(ML_WIDTH), row(ML_WIDTH), const(1, ML_WIDTH),
                  row(GDN_WIDTH), row(GDN_WIDTH), row(GDN_WIDTH), const(1, GDN_WIDTH),
                  const(D_MODEL, D_MODEL)],
        out_specs=row(D_MODEL),
        compiler_params=_params("parallel"),
        name="out_proj",
    )(x2d, y_na, hf, hb, ml_o, ml_norm_w.reshape(1, ML_WIDTH).astype(F32), of, ob, gd_z,
      gdn_norm_w.reshape(1, GDN_WIDTH).astype(F32), w_out.astype(BF16))


def _mem_kv_kernel(m_ref, nw_ref, w_ref, k_ref, v_ref):
    h = _rms(m_ref[...], nw_ref[...]).astype(BF16)
    k_ref[...] = _dot(h, w_ref[:, :D_MODEL]).astype(BF16)
    v_ref[...] = _dot(h, w_ref[:, D_MODEL:]).astype(BF16)


def _mem_kv(mem2d, norm_w, w_kv):
    t = mem2d.shape[0]
    tm = min(TM_ROWS, t)
    out = jax.ShapeDtypeStruct((t, D_MODEL), BF16)
    return pl.pallas_call(
        _mem_kv_kernel,
        out_shape=(out, out),
        grid=(t // tm,),
        in_specs=[pl.BlockSpec((tm, D_MODEL), lambda i: (i, 0)),
                  pl.BlockSpec((1, D_MODEL), lambda i: (0, 0)),
                  pl.BlockSpec((D_MODEL, 2 * D_MODEL), lambda i: (0, 0))],
        out_specs=(pl.BlockSpec((tm, D_MODEL), lambda i: (i, 0)), pl.BlockSpec((tm, D_MODEL), lambda i: (i, 0))),
        compiler_params=_params("parallel"),
        name="mem_kv",
    )(mem2d, norm_w.reshape(1, D_MODEL).astype(F32), w_kv.astype(BF16))


def _xattn_kernel(x_ref, nw_ref, wq_ref, k_ref, v_ref, wo_ref, o_ref):
    x = x_ref[...]
    q = _dot(_rms(x, nw_ref[...]).astype(BF16), wq_ref[...]).astype(BF16)
    heads = [slice(h * XA_HEAD_DIM, (h + 1) * XA_HEAD_DIM) for h in range(XA_HEADS)]
    s = [_dot_nt(q[:, cols], k_ref[:, cols]) * (XA_HEAD_DIM ** -0.5) for cols in heads]
    e = [jnp.exp(s_h - jnp.max(s_h, axis=-1, keepdims=True)) for s_h in s]
    inv_l = [1.0 / jnp.sum(e_h, axis=-1, keepdims=True) for e_h in e]
    o = [(_dot(e_h.astype(BF16), v_ref[:, cols]) * il).astype(BF16) for e_h, il, cols in zip(e, inv_l, heads)]
    o_ref[...] = x + _dot(jnp.concatenate(o, axis=1), wo_ref[...])


def _xattn(x3d, norm_w, w_q, k, v, w_o):
    b, s, _ = x3d.shape
    tm = min(TM_ROWS, s)
    n_mem = k.shape[1]
    return pl.pallas_call(
        _xattn_kernel,
        out_shape=jax.ShapeDtypeStruct(x3d.shape, F32),
        grid=(b, s // tm),
        in_specs=[pl.BlockSpec((None, tm, D_MODEL), lambda bi, i: (bi, i, 0)),
                  pl.BlockSpec((1, D_MODEL), lambda bi, i: (0, 0)),
                  pl.BlockSpec((D_MODEL, D_MODEL), lambda bi, i: (0, 0)),
                  pl.BlockSpec((None, n_mem, D_MODEL), lambda bi, i: (bi, 0, 0)),
                  pl.BlockSpec((None, n_mem, D_MODEL), lambda bi, i: (bi, 0, 0)),
                  pl.BlockSpec((D_MODEL, D_MODEL), lambda bi, i: (0, 0))],
        out_specs=pl.BlockSpec((None, tm, D_MODEL), lambda bi, i: (bi, i, 0)),
        compiler_params=_params("parallel", "parallel"),
        name="cross_attention",
    )(x3d, norm_w.reshape(1, D_MODEL).astype(F32), w_q.astype(BF16), k, v, w_o.astype(BF16))


FF_CHUNK = 512


def _ffn_kernel(x_ref, nw_ref, w1_ref, w2_ref, fw_ref, o_ref, *, final_norm):
    x = x_ref[...]
    h = _rms(x, nw_ref[...]).astype(BF16)
    acc = x
    for c0 in range(0, D_FF, FF_CHUNK):
        a = jnp.maximum(_dot(h, w1_ref[:, c0:c0 + FF_CHUNK]), 0.0)
        acc = acc + _dot((a * a).astype(BF16), w2_ref[c0:c0 + FF_CHUNK, :])
    o_ref[...] = _rms(acc, fw_ref[...]) if final_norm else acc


def _ffn(x2d, norm_w, w1, w2, final_w, final_norm):
    t = x2d.shape[0]
    tm = min(TM_ROWS, t)
    return pl.pallas_call(
        functools.partial(_ffn_kernel, final_norm=final_norm),
        out_shape=jax.ShapeDtypeStruct((t, D_MODEL), F32),
        grid=(t // tm,),
        in_specs=[pl.BlockSpec((tm, D_MODEL), lambda i: (i, 0)),
                  pl.BlockSpec((1, D_MODEL), lambda i: (0, 0)),
                  pl.BlockSpec((D_MODEL, D_FF), lambda i: (0, 0), pipeline_mode=pl.Buffered(1)),
                  pl.BlockSpec((D_FF, D_MODEL), lambda i: (0, 0), pipeline_mode=pl.Buffered(1)),
                  pl.BlockSpec((1, D_MODEL), lambda i: (0, 0))],
        out_specs=pl.BlockSpec((tm, D_MODEL), lambda i: (i, 0)),
        compiler_params=_params("parallel"),
        name="ffn",
    )(x2d, norm_w.reshape(1, D_MODEL).astype(F32), w1.astype(BF16), w2.astype(BF16),
      final_w.reshape(1, D_MODEL).astype(F32))


def kernel(x, mem, norm_mix_w, w_in, na_rel_bias, ml_i_bias, ml_f_bias, ml_norm_w, gdn_conv_w, gdn_a_log,
           gdn_dt_bias, gdn_norm_w, w_out, norm_xa_w, norm_mem_w, w_xq, w_xkv, w_xo, norm_ffn_w, w_ff1,
           w_ff2, norm_out_w):
    b, s, d = x.shape
    depth = w_in.shape[0]
    x2d = x.reshape(b * s, d).astype(F32)
    mem2d = mem.reshape(-1, d).astype(F32)
    for l in range(depth):
        (na_q, na_k, na_v, ml_q, ml_k, ml_v, ml_o, gd_qkv, gd_z, gates) = _in_proj(
            x2d, norm_mix_w[l].astype(F32), _permute_w_in(w_in[l]))
        seq = lambda a: a.reshape(b, s, a.shape[-1])
        gates3 = seq(gates)
        y_na = _neighbourhood_attention(seq(na_q), seq(na_k), seq(na_v), _na_bias_table(na_rel_bias[l]))
        hf, hb = _mlstm(seq(ml_q), seq(ml_k), seq(ml_v), gates3, ml_i_bias[l], ml_f_bias[l])
        gq, gk, gv = _gdn_prep(seq(gd_qkv), gdn_conv_w[l])
        of, ob = _gdn(gq, gk, gv, gates3, gdn_a_log[l], gdn_dt_bias[l])
        flat = lambda a: a.reshape(b * s, a.shape[-1])
        x2d = _out_proj(x2d, flat(y_na), flat(hf), flat(hb), ml_o, ml_norm_w[l], flat(of), flat(ob), gd_z,
                        gdn_norm_w[l], w_out[l])
        mk, mv = _mem_kv(mem2d, norm_mem_w[l], w_xkv[l])
        n_mem = mem.shape[1]
        x2d = _xattn(x2d.reshape(b, s, d), norm_xa_w[l], w_xq[l], mk.reshape(b, n_mem, d),
                     mv.reshape(b, n_mem, d), w_xo[l]).reshape(b * s, d)
        x2d = _ffn(x2d, norm_ffn_w[l], w_ff1[l], w_ff2[l], norm_out_w, final_norm=(l == depth - 1))
    return x2d.reshape(b, s, d).astype(x.dtype)
```

```python
import functools

import numpy as np
import jax
import jax.numpy as jnp
from jax import lax
from jax.experimental import pallas as pl
from jax.experimental.pallas import tpu as pltpu

F32 = jnp.float32
BF16 = jnp.bfloat16

D_MODEL = 1024
HEAD_DIM = 64
GRID_W = 64
NA_HEADS = 6
NA_WIN_ROWS = 8
NA_WIN_COLS = 16
ML_HEADS = 4
GDN_HEADS = 6
CONV_K = 5
N_DIR = 2
XA_HEADS = 4
XA_HEAD_DIM = D_MODEL // XA_HEADS
D_FF = 4 * D_MODEL
NA_WIDTH = NA_HEADS * HEAD_DIM
ML_WIDTH = ML_HEADS * HEAD_DIM
GDN_WIDTH = GDN_HEADS * HEAD_DIM
EPS = 1e-6
CHUNK = 64
PAIR = 2 * HEAD_DIM
NEG = -1e30

IN_SIZES = (NA_WIDTH, NA_WIDTH, NA_WIDTH,
            ML_WIDTH, ML_WIDTH, ML_WIDTH, ML_WIDTH, N_DIR * ML_HEADS, N_DIR * ML_HEADS,
            GDN_WIDTH, GDN_WIDTH, GDN_WIDTH, GDN_WIDTH, N_DIR * GDN_HEADS, N_DIR * GDN_HEADS)

GATE_ML_I = 0
GATE_ML_F = GATE_ML_I + N_DIR * ML_HEADS
GATE_GD_B = GATE_ML_F + N_DIR * ML_HEADS
GATE_GD_A = GATE_GD_B + N_DIR * GDN_HEADS
GATE_USED = GATE_GD_A + N_DIR * GDN_HEADS
GATE_LANES = 128

V7X_VMEM_LIMIT = 56 * 1024 * 1024
MXU_COLS = 256

TM_ROWS = 1024
NA_ROWS_PER_STEP = 8
SEQ_BLOCK = 1024
GDN_SEQ_BLOCK = 1024


def _params(*sem):
    return pltpu.CompilerParams(dimension_semantics=sem, vmem_limit_bytes=V7X_VMEM_LIMIT)


def _dot(a, b):
    return jnp.dot(a, b, preferred_element_type=F32)


def _dot_nt(a, b):
    return lax.dot_general(a, b, (((1,), (1,)), ((), ())), preferred_element_type=F32)


def _dot_tn(a, b):
    return lax.dot_general(a, b, (((0,), (0,)), ((), ())), preferred_element_type=F32)


def _split3(x):
    x1 = x.astype(BF16)
    r1 = x - x1.astype(F32)
    x2 = r1.astype(BF16)
    x3 = (r1 - x2.astype(F32)).astype(BF16)
    return x1, x2, x3


def _sum_right(x, m01):
    x1 = x.astype(BF16)
    x2 = (x - x1.astype(F32)).astype(BF16)
    return _dot(x1, m01) + _dot(x2, m01)


def _exact_right(x, m01):
    x1, x2, x3 = _split3(x)
    return _dot(x1, m01) + _dot(x2, m01) + _dot(x3, m01)


def _rms(x, w):
    ms = jnp.mean(x * x, axis=-1, keepdims=True)
    return x * lax.rsqrt(ms + EPS) * w


def _softplus(x):
    return jnp.maximum(x, 0.0) + jnp.log1p(jnp.exp(-jnp.abs(x)))


def _sigmoid(x):
    return 1.0 / (1.0 + jnp.exp(-x))


def _segment_mean_matrix(width):
    r = lax.broadcasted_iota(jnp.int32, (width, width), 0) // HEAD_DIM
    c = lax.broadcasted_iota(jnp.int32, (width, width), 1) // HEAD_DIM
    return jnp.where(r == c, 1.0, 0.0).astype(BF16)


def _head_sumsq(t):
    return _sum_right(t * t, _segment_mean_matrix(t.shape[-1]))


def _aligned(x, m):
    return x if isinstance(x, int) else pl.multiple_of(x, m)


def _pair_masks(rev):
    r = lax.broadcasted_iota(jnp.int32, (PAIR, PAIR), 0)
    c = lax.broadcasted_iota(jnp.int32, (PAIR, PAIR), 1)
    same = (r // HEAD_DIM) == (c // HEAD_DIM)
    t, s = r % HEAD_DIM, c % HEAD_DIM
    if rev:
        return same & (s >= t), same & (s > t)
    return same & (s <= t), same & (s < t)


def _stack_pair(x2):
    lane = lax.broadcasted_iota(jnp.int32, x2.shape, 1)
    zero = jnp.zeros_like(x2)
    return jnp.concatenate([jnp.where(lane < HEAD_DIM, x2, zero),
                            jnp.where(lane >= HEAD_DIM, x2, zero)], axis=0)


IN_SEGMENTS = (
    (NA_WIDTH, BF16), (NA_WIDTH, BF16), (NA_WIDTH, BF16),
    (ML_WIDTH, BF16), (ML_WIDTH, BF16), (ML_WIDTH, BF16), (ML_WIDTH, F32),
    (3 * GDN_WIDTH, F32), (GDN_WIDTH, F32), (GATE_LANES, F32))
IN_COLS = sum(w for w, _ in IN_SEGMENTS)
IN_DOT_COLS = 3 * MXU_COLS


def _in_proj_kernel(x_ref, nw_ref, w_ref, *out_refs):
    h = _rms(x_ref[...], nw_ref[...]).astype(BF16)
    starts = np.cumsum([0] + [w for w, _ in IN_SEGMENTS])
    for c0 in range(0, IN_COLS, IN_DOT_COLS):
        c1 = min(c0 + IN_DOT_COLS, IN_COLS)
        acc = _dot(h, w_ref[:, c0:c1])
        for o_ref, s0, s1 in zip(out_refs, starts[:-1], starts[1:]):
            a, b = max(c0, int(s0)), min(c1, int(s1))
            if a < b:
                o_ref[:, a - int(s0):b - int(s0)] = acc[:, a - c0:b - c0].astype(o_ref.dtype)


def _in_proj(x2d, norm_w, w_perm):
    t = x2d.shape[0]
    tm = min(TM_ROWS, t)
    outs = tuple(jax.ShapeDtypeStruct((t, w), dt) for w, dt in IN_SEGMENTS)
    return pl.pallas_call(
        _in_proj_kernel,
        out_shape=outs,
        grid=(t // tm,),
        in_specs=[pl.BlockSpec((tm, D_MODEL), lambda i: (i, 0)),
                  pl.BlockSpec((1, D_MODEL), lambda i: (0, 0)),
                  pl.BlockSpec((D_MODEL, IN_COLS), lambda i: (0, 0), pipeline_mode=pl.Buffered(1))],
        out_specs=tuple(pl.BlockSpec((tm, w), lambda i: (i, 0)) for w, _ in IN_SEGMENTS),
        compiler_params=_params("parallel"),
        name="in_proj",
    )(x2d, norm_w.reshape(1, D_MODEL), w_perm)


def _permute_w_in(w):
    parts = jnp.split(w, np.cumsum(IN_SIZES)[:-1], axis=-1)
    (na_q, na_k, na_v, ml_q, ml_k, ml_v, ml_o, ml_i, ml_f, gd_q, gd_k, gd_v, gd_z, gd_b, gd_a) = parts
    pad = jnp.zeros((w.shape[0], GATE_LANES - GATE_USED), w.dtype)
    return jnp.concatenate([na_q, na_k, na_v, ml_q, ml_k, ml_v, ml_o, gd_q, gd_k, gd_v, gd_z,
                            ml_i, ml_f, gd_b, gd_a, pad], axis=-1).astype(BF16)


NA_BIAS_ROWS = 96


def _na_bias_kernel(rb_ref, o_ref):
    n = GRID_W * GRID_W
    dc = lax.broadcasted_iota(jnp.int32, (GATE_LANES, n), 0)
    col = lax.broadcasted_iota(jnp.int32, (GATE_LANES, n), 1)
    q, kc = col // GRID_W, col % GRID_W
    c0 = jnp.clip(q - NA_WIN_COLS // 2, 0, GRID_W - NA_WIN_COLS)
    valid = (kc >= c0) & (kc < c0 + NA_WIN_COLS)
    onehot = jnp.where(valid & (kc - q + (NA_WIN_COLS - 1) == dc), 1.0, 0.0).astype(BF16)
    o_ref[...] = jnp.where(valid[0:1, :], _exact_right(rb_ref[...], onehot), NEG)


def _na_bias_table(rel_bias):
    nh, ndr, ndc = rel_bias.shape
    rb = jnp.zeros((NA_BIAS_ROWS, GATE_LANES), F32).at[:nh * ndr, :ndc].set(
        rel_bias.astype(F32).reshape(nh * ndr, ndc))
    band = pl.pallas_call(
        _na_bias_kernel,
        out_shape=jax.ShapeDtypeStruct((NA_BIAS_ROWS, GRID_W * GRID_W), F32),
        name="na_bias_expand",
    )(rb)
    band = band[:nh * ndr].reshape(nh, ndr, GRID_W, GRID_W)
    first = np.clip(np.arange(ndr + 1) - 1, 0, ndr - 1)
    second = np.clip(np.arange(ndr + 1), 0, ndr - 1)
    tab = jnp.concatenate([band[:, first], band[:, second]], axis=-1)
    tab = tab.reshape(NA_HEADS // 2, 2, ndr + 1, GRID_W, PAIR)
    return jnp.moveaxis(tab, 1, 2).reshape(NA_HEADS // 2, ndr + 1, 2 * GRID_W, PAIR)


NA_ITEM_ROWS = 4
NA_KEY_ROWS = NA_ITEM_ROWS + NA_WIN_ROWS


def _na_kernel(q_ref, k_ref, v_ref, bias_ref, o_ref, *, rows, rows_per_step):
    j = pl.program_id(1)
    nkeys = NA_KEY_ROWS * GRID_W
    lane = lax.broadcasted_iota(jnp.int32, (GRID_W, PAIR), 1)
    lane_blk = lax.broadcasted_iota(jnp.int32, (PAIR, PAIR), 1)
    items = []
    for g in range(rows_per_step // NA_ITEM_ROWS):
        r_first = j * rows_per_step + g * NA_ITEM_ROWS
        kr0 = jnp.clip(r_first - NA_WIN_ROWS // 2, 0, rows - NA_KEY_ROWS)
        koff = pl.multiple_of(kr0 * GRID_W, GRID_W)
        entry, pen = {}, {}
        for i in range(NA_ITEM_ROWS):
            r = r_first + i
            r0 = jnp.clip(r - NA_WIN_ROWS // 2, 0, rows - NA_WIN_ROWS)
            for jp in range(NA_KEY_ROWS // 2):
                key_a = kr0 + 2 * jp
                ok_a = (key_a >= r0) & (key_a < r0 + NA_WIN_ROWS)
                ok_b = (key_a + 1 >= r0) & (key_a + 1 < r0 + NA_WIN_ROWS)
                entry[i, jp] = jnp.clip(key_a - r + (NA_WIN_ROWS - 1), -1, 2 * NA_WIN_ROWS - 2) + 1
                pen[i, jp] = jnp.where(lane_blk < GRID_W, jnp.where(ok_a, 0.0, NEG), jnp.where(ok_b, 0.0, NEG))
        for p in range(NA_HEADS // 2):
            items.append(dict(g=g, p=p, koff=koff, entry=entry, pen=pen, cols=slice(p * PAIR, (p + 1) * PAIR)))
    for it in items:
        q0 = it["g"] * NA_ITEM_ROWS * GRID_W
        q_lhs = jnp.concatenate(
            [_stack_pair(q_ref[q0 + i * GRID_W:q0 + (i + 1) * GRID_W, it["cols"]]) for i in range(NA_ITEM_ROWS)],
            axis=0)
        q_lhs = (q_lhs.astype(F32) * (HEAD_DIM ** -0.5)).astype(BF16)
        it["s"] = _dot_nt(q_lhs, k_ref[pl.ds(it["koff"], nkeys), it["cols"]])
    for it in items:
        s = it.pop("s")
        blocks = []
        for i in range(NA_ITEM_ROWS):
            blocks.append(jnp.concatenate(
                [s[i * PAIR:(i + 1) * PAIR, jp * PAIR:(jp + 1) * PAIR] + bias_ref[it["p"], it["entry"][i, jp]]
                 + it["pen"][i, jp] for jp in range(NA_KEY_ROWS // 2)], axis=1))
        s = jnp.concatenate(blocks, axis=0)
        e = jnp.exp(s - jnp.max(s, axis=-1, keepdims=True))
        it["l"] = jnp.sum(e, axis=-1, keepdims=True)
        it["o"] = _dot(e.astype(BF16), v_ref[pl.ds(it["koff"], nkeys), it["cols"]])
    for it in items:
        o = it.pop("o") * (1.0 / it.pop("l"))
        q0 = it["g"] * NA_ITEM_ROWS * GRID_W
        for i in range(NA_ITEM_ROWS):
            blk = o[i * PAIR:(i + 1) * PAIR]
            o_ref[q0 + i * GRID_W:q0 + (i + 1) * GRID_W, it["cols"]] = jnp.where(
                lane < HEAD_DIM, blk[:GRID_W], blk[GRID_W:]).astype(o_ref.dtype)


def _neighbourhood_attention(q, k, v, bias_tab):
    b, s, _ = q.shape
    rows = s // GRID_W
    assert rows >= NA_KEY_ROWS
    rps = min(NA_ROWS_PER_STEP, rows)
    tq = rps * GRID_W
    return pl.pallas_call(
        functools.partial(_na_kernel, rows=rows, rows_per_step=rps),
        out_shape=jax.ShapeDtypeStruct((b, s, NA_WIDTH), BF16),
        grid=(b, rows // rps),
        in_specs=[pl.BlockSpec((None, tq, NA_WIDTH), lambda bi, j: (bi, j, 0)),
                  pl.BlockSpec((None, s, NA_WIDTH), lambda bi, j: (bi, 0, 0)),
                  pl.BlockSpec((None, s, NA_WIDTH), lambda bi, j: (bi, 0, 0)),
                  pl.BlockSpec(bias_tab.shape, lambda bi, j: (0, 0, 0, 0))],
        out_specs=pl.BlockSpec((None, tq, NA_WIDTH), lambda bi, j: (bi, j, 0)),
        compiler_params=_params("parallel", "arbitrary"),
        name="neighbourhood_attention",
    )(q, k, v, bias_tab)


def _gate_rows(gates, lane0, heads):
    b, s, _ = gates.shape
    g = gates[:, :, lane0:lane0 + N_DIR * heads].reshape(b, s // CHUNK, CHUNK, N_DIR, heads // 2, 2)
    return jnp.transpose(g, (0, 3, 4, 1, 5, 2)).reshape(b, N_DIR, heads // 2, s // CHUNK, PAIR)


def _param_rows(p, heads):
    return jnp.repeat(p.astype(F32).reshape(N_DIR, heads // 2, 2), HEAD_DIM, axis=-1).reshape(
        N_DIR, heads // 2, 1, PAIR)


def _pair_cumsum_matrix(rev):
    r = lax.broadcasted_iota(jnp.int32, (PAIR, PAIR), 0)
    c = lax.broadcasted_iota(jnp.int32, (PAIR, PAIR), 1)
    same = (r // HEAD_DIM) == (c // HEAD_DIM)
    sp, s = r % HEAD_DIM, c % HEAD_DIM
    return jnp.where(same & ((sp >= s) if rev else (sp <= s)), 1.0, 0.0).astype(BF16)


ML_PAIRS = ML_HEADS // 2
ML_UNITS = N_DIR * ML_PAIRS
ML_CHUNKS_PER_TRIP = 2
ML_STEPS_PER_TRIP = 4
OS_ROWS = 16


def _outer_sum_operands(col_term, row_term):
    ones = jnp.ones((3, PAIR), BF16)
    zeros = jnp.zeros((OS_ROWS - 6, PAIR), BF16)
    a = jnp.concatenate(list(_split3(col_term)) + [ones, zeros], axis=0)
    b = jnp.concatenate([ones] + list(_split3(row_term)) + [zeros], axis=0)
    return a, b


def _mlstm_kernel(qf_ref, kf_ref, vf_ref, if_ref, ff_ref,
                  qb_ref, kb_ref, vb_ref, ib_ref, fb_ref,
                  ibr_ref, fbr_ref,
                  hf_ref, hb_ref,
                  c_scr, m_scr, cs_scr, rt_scr, ai_scr, em_scr, we_scr, a_scr, g_scr, vt_scr, ut_scr, *, chunks):
    j = pl.program_id(1)

    @pl.when(j == 0)
    def _():
        c_scr[...] = jnp.zeros_like(c_scr)
        m_scr[...] = jnp.zeros_like(m_scr)

    lane = lax.broadcasted_iota(jnp.int32, (chunks, PAIR), 1)
    pos = lane % HEAD_DIM
    lo = lane < HEAD_DIM
    for d, (i_ref, f_ref) in enumerate(((if_ref, ff_ref), (ib_ref, fb_ref))):
        rev = bool(d)
        ucum = _pair_cumsum_matrix(rev)
        last = 0 if rev else CHUNK - 1
        for p in range(ML_PAIRS):
            u = d * ML_PAIRS + p
            logf = -_softplus(-(f_ref[p] + fbr_ref[d, p]))
            bcum = _exact_right(logf, ucum)
            cs = i_ref[p] + ibr_ref[d, p] - bcum
            bl = jnp.where(lo, bcum[:, last:last + 1], bcum[:, HEAD_DIM + last:HEAD_DIM + last + 1])
            cm = cs
            for k in (1, 2, 4, 8, 16, 32):
                if rev:
                    cm = jnp.where(pos < HEAD_DIM - k, jnp.maximum(cm, pltpu.roll(cm, PAIR - k, axis=1)), cm)
                else:
                    cm = jnp.where(pos >= k, jnp.maximum(cm, pltpu.roll(cm, k, axis=1)), cm)
            w = bl + cs
            m_loc = jnp.where(lo, jnp.max(jnp.where(lo, w, NEG), axis=-1, keepdims=True),
                              jnp.max(jnp.where(lo, NEG, w), axis=-1, keepdims=True))
            m = m_scr[u]
            m_prev, a_rows, g_rows = [None] * chunks, [None] * chunks, [None] * chunks
            for c in (range(chunks - 1, -1, -1) if rev else range(chunks)):
                m_prev[c] = m
                m_new = jnp.maximum(bl[c:c + 1] + m, m_loc[c:c + 1])
                a_rows[c] = jnp.exp(bl[c:c + 1] + m - m_new)
                g_rows[c] = jnp.exp(m_loc[c:c + 1] - m_new)
                m = m_new
            m_scr[u] = m
            inter = bcum + jnp.concatenate(m_prev, axis=0)
            m_t = jnp.maximum(bcum + cm, inter)
            cs_scr[u] = cs
            rt_scr[u] = bcum - m_t
            ai_scr[u] = jnp.exp(inter - m_t)
            em_scr[u] = jnp.exp(-m_t)
            we_scr[u] = jnp.exp(w - m_loc)
            a_scr[u] = jnp.concatenate(a_rows, axis=0)
            g_scr[u] = jnp.concatenate(g_rows, axis=0)

    dirs = ((qf_ref, kf_ref, vf_ref, hf_ref), (qb_ref, kb_ref, vb_ref, hb_ref))
    units = [(d, p) for d in range(N_DIR) for p in range(ML_PAIRS)]
    lane64 = lax.broadcasted_iota(jnp.int32, (CHUNK, PAIR), 1)
    one_hi = jnp.where(lane64 == HEAD_DIM, 1.0, 0.0).astype(BF16)
    one_lo = jnp.where(lane64 == 0, 1.0, 0.0).astype(BF16)
    lane_row = lax.broadcasted_iota(jnp.int32, (1, PAIR), 1)

    def key_tile(d, p, r0):
        k_pair = _stack_pair(dirs[d][1][pl.ds(r0, CHUNK), p * PAIR:(p + 1) * PAIR])
        return (k_pair.astype(F32) * (HEAD_DIM ** -0.5)).astype(BF16)

    def contributions(t, carry):
        for cc in range(ML_CHUNKS_PER_TRIP):
            c = t * ML_CHUNKS_PER_TRIP + cc
            r0 = pl.multiple_of(c * CHUNK, CHUNK)
            for u, (d, p) in enumerate(units):
                v2 = dirs[d][2][pl.ds(r0, CHUNK), p * PAIR:(p + 1) * PAIR]
                v_ext = jnp.concatenate([jnp.where(lane64 < HEAD_DIM, v2, one_hi),
                                         jnp.where(lane64 >= HEAD_DIM, v2, one_lo)], axis=0)
                v_t = v_ext.astype(F32).T
                vt_scr[u, c] = v_t.astype(BF16)
                ut_scr[u, c] = _dot((v_t * we_scr[u, pl.ds(c, 1), :]).astype(BF16), key_tile(d, p, r0))
        return carry

    lax.fori_loop(0, chunks // ML_CHUNKS_PER_TRIP, contributions, 0)

    def outputs(t, carry):
        masks = [_pair_masks(True)[0], _pair_masks(False)[0]]
        items = []
        for u, (d, p) in enumerate(units):
            ct = c_scr[u]
            for k in range(ML_STEPS_PER_TRIP):
                step = t * ML_STEPS_PER_TRIP + k
                c = (chunks - 1 - step) if d else step
                r0 = pl.multiple_of(c * CHUNK, CHUNK)
                row = pl.ds(c, 1)
                items.append(dict(u=u, d=d, p=p, c=c, r0=r0, row=row, ct=ct.astype(BF16)))
                ct = a_scr[u, row, :] * ct + g_scr[u, row, :] * ut_scr[u, c]
            c_scr[u] = ct
        for it in items:
            u, d, p, r0, row = it["u"], it["d"], it["p"], it["r0"], it["row"]
            q_pair = _stack_pair(dirs[d][0][pl.ds(r0, CHUNK), p * PAIR:(p + 1) * PAIR])
            os_a, os_b = _outer_sum_operands(cs_scr[u, row, :], rt_scr[u, row, :])
            it["gram"] = _dot_nt(key_tile(d, p, r0), q_pair)
            it["osum"] = _dot_tn(os_a, os_b)
            it["st"] = _dot_nt(it.pop("ct"), q_pair)
        for it in items:
            u, c = it["u"], it["c"]
            s_t = it.pop("gram") * jnp.exp(jnp.where(masks[it["d"]], it.pop("osum"), NEG))
            it["intra"] = jnp.sum(s_t, axis=0, keepdims=True)
            it["num"] = _dot(vt_scr[u, c], s_t.astype(BF16))
        for it in items:
            u, d, p, row = it["u"], it["d"], it["p"], it["row"]
            a_inter = ai_scr[u, row, :]
            st = it.pop("st")
            den = a_inter * jnp.where(lane_row < HEAD_DIM, st[HEAD_DIM:HEAD_DIM + 1], st[0:1]) + it.pop("intra")
            out_t = (it.pop("num") + a_inter * st) * (1.0 / jnp.maximum(jnp.abs(den), em_scr[u, row, :]))
            out = out_t.T
            dirs[d][3][pl.ds(it["r0"], CHUNK), p * PAIR:(p + 1) * PAIR] = jnp.where(
                lane64 < HEAD_DIM, out[:CHUNK], out[CHUNK:])
        return carry

    lax.fori_loop(0, chunks // ML_STEPS_PER_TRIP, outputs, 0)


def _mlstm(q, k, v, gates, i_bias, f_bias):
    b, s, _ = q.shape
    tb = min(SEQ_BLOCK, s)
    nb = s // tb
    chunks = tb // CHUNK
    i_rows = _gate_rows(gates, GATE_ML_I, ML_HEADS)
    f_rows = _gate_rows(gates, GATE_ML_F, ML_HEADS)
    ibr = _param_rows(i_bias, ML_HEADS)
    fbr = _param_rows(f_bias, ML_HEADS)

    def fwd(bi, j):
        return (bi, j, 0)

    def bwd(bi, j):
        return (bi, nb - 1 - j, 0)

    seq = lambda w, im: pl.BlockSpec((None, tb, w), im)
    rows_f = pl.BlockSpec((None, None, ML_PAIRS, chunks, PAIR), lambda bi, j: (bi, 0, 0, j, 0))
    rows_b = pl.BlockSpec((None, None, ML_PAIRS, chunks, PAIR), lambda bi, j: (bi, 1, 0, nb - 1 - j, 0))
    full = lambda a: pl.BlockSpec(a.shape, lambda bi, j: (0,) * a.ndim)
    out = jax.ShapeDtypeStruct((b, s, ML_WIDTH), F32)
    rows = pltpu.VMEM((ML_UNITS, chunks, PAIR), F32)
    return pl.pallas_call(
        functools.partial(_mlstm_kernel, chunks=chunks),
        out_shape=(out, out),
        grid=(b, nb),
        in_specs=[seq(ML_WIDTH, fwd), seq(ML_WIDTH, fwd), seq(ML_WIDTH, fwd), rows_f, rows_f,
                  seq(ML_WIDTH, bwd), seq(ML_WIDTH, bwd), seq(ML_WIDTH, bwd), rows_b, rows_b,
                  full(ibr), full(fbr)],
        out_specs=(seq(ML_WIDTH, fwd), seq(ML_WIDTH, bwd)),
        scratch_shapes=[pltpu.VMEM((ML_UNITS, PAIR, PAIR), F32),
                        pltpu.VMEM((ML_UNITS, 1, PAIR), F32),
                        rows, rows, rows, rows, rows, rows, rows,
                        pltpu.VMEM((ML_UNITS, chunks, PAIR, PAIR), BF16),
                        pltpu.VMEM((ML_UNITS, chunks, PAIR, PAIR), F32)],
        compiler_params=_params("parallel", "arbitrary"),
        name="mlstm",
    )(q, k, v, i_rows, f_rows, q, k, v, i_rows, f_rows, ibr, fbr)


GD_PAIRS = GDN_HEADS // 2
CONV_HALO = 8
PREP_ROWS = 128


def _gdn_prep_kernel(x_ref, prev_ref, next_ref, w_ref, q_ref, k_ref, v_ref, ext_scr, *, tb):
    j = pl.program_id(1)
    nb = pl.num_programs(1)
    ext_scr[0:CONV_HALO, :] = jnp.where(j > 0, prev_ref[...], 0.0)
    ext_scr[CONV_HALO:CONV_HALO + tb, :] = x_ref[...]
    ext_scr[CONV_HALO + tb:, :] = jnp.where(j < nb - 1, next_ref[...], 0.0)
    base = CONV_HALO - CONV_K // 2
    seg = _segment_mean_matrix(PAIR)
    taps = [w_ref[t:t + 1, :] for t in range(CONV_K)]

    def rows_block(i, carry):
        r0 = pl.multiple_of(i * PREP_ROWS, PREP_ROWS)
        for ct in range(3 * GDN_WIDTH // PAIR):
            cols = slice(ct * PAIR, (ct + 1) * PAIR)
            slab = ext_scr[pl.ds(r0, PREP_ROWS + 2 * CONV_HALO), cols]
            y = slab[base:base + PREP_ROWS] * taps[0][:, cols]
            for t in range(1, CONV_K):
                y = y + slab[base + t:base + t + PREP_ROWS] * taps[t][:, cols]
            y = y * _sigmoid(y)
            which, off = divmod(ct * PAIR, GDN_WIDTH)
            out_ref = (q_ref, k_ref, v_ref)[which]
            if which < 2:
                y = y * lax.rsqrt(_sum_right(y * y, seg) + EPS)
            out_ref[pl.ds(r0, PREP_ROWS), off:off + PAIR] = y
        return carry

    lax.fori_loop(0, tb // PREP_ROWS, rows_block, 0)


def _gdn_prep(qkv, conv_w):
    b, s, w = qkv.shape
    tb = min(SEQ_BLOCK, s)
    nb = s // tb
    hb = tb // CONV_HALO
    last = s // CONV_HALO - 1
    out = jax.ShapeDtypeStruct((b, s, GDN_WIDTH), F32)
    return pl.pallas_call(
        functools.partial(_gdn_prep_kernel, tb=tb),
        out_shape=(out, out, out),
        grid=(b, nb),
        in_specs=[pl.BlockSpec((None, tb, w), lambda bi, j: (bi, j, 0)),
                  pl.BlockSpec((None, CONV_HALO, w), lambda bi, j: (bi, jnp.maximum(j * hb - 1, 0), 0)),
                  pl.BlockSpec((None, CONV_HALO, w), lambda bi, j: (bi, jnp.minimum((j + 1) * hb, last), 0)),
                  pl.BlockSpec((CONV_K, w), lambda bi, j: (0, 0))],
        out_specs=tuple(pl.BlockSpec((None, tb, GDN_WIDTH), lambda bi, j: (bi, j, 0)) for _ in range(3)),
        scratch_shapes=[pltpu.VMEM((tb + 2 * CONV_HALO, w), F32)],
        compiler_params=_params("parallel", "arbitrary"),
        name="gdn_prep",
    )(qkv, qkv, qkv, conv_w.astype(F32))


GD_CHUNKS_PER_TRIP = 2
GD_INV_BLOCK = 16
GD_UNITS = N_DIR * GD_PAIRS


def _gdn_kernel(qf_ref, kf_ref, vf_ref, af_ref, bf_ref,
                qb_ref, kb_ref, vb_ref, ab_ref, bb_ref,
                alr_ref, dtr_ref,
                of_ref, ob_ref,
                s_scr, gc_scr, beta_scr, egc_scr, qsc_scr, fb_scr, eg_scr,
                u_scr, wq_scr, at_scr, ks_scr, *, chunks):
    j = pl.program_id(1)

    @pl.when(j == 0)
    def _():
        s_scr[...] = jnp.zeros_like(s_scr)

    lane_row = lax.broadcasted_iota(jnp.int32, (chunks, PAIR), 1)
    for d, (a_ref, b_ref) in enumerate(((af_ref, bf_ref), (ab_ref, bb_ref))):
        ucum = _pair_cumsum_matrix(rev=bool(d))
        last = 0 if d else CHUNK - 1
        for p in range(GD_PAIRS):
            u = d * GD_PAIRS + p
            g = -jnp.exp(alr_ref[d, p]) * _softplus(a_ref[p] + dtr_ref[d, p])
            gc = _exact_right(g, ucum)
            beta = _sigmoid(b_ref[p])
            g_last = jnp.where(lane_row < HEAD_DIM, gc[:, last:last + 1],
                               gc[:, HEAD_DIM + last:HEAD_DIM + last + 1])
            gc_scr[u] = gc
            beta_scr[u] = beta
            egc_scr[u] = jnp.exp(gc)
            qsc_scr[u] = jnp.exp(gc) * (HEAD_DIM ** -0.5)
            fb_scr[u] = jnp.exp(g_last - gc) * beta
            eg_scr[u] = jnp.exp(g_last)

    r = lax.broadcasted_iota(jnp.int32, (PAIR, PAIR), 0)
    cidx = lax.broadcasted_iota(jnp.int32, (PAIR, PAIR), 1)
    is_diag = r == cidx
    same16 = (r // GD_INV_BLOCK) == (cidx // GD_INV_BLOCK)
    eye_s = jnp.where(lax.broadcasted_iota(jnp.int32, (GD_INV_BLOCK, PAIR), 0)
                      == lax.broadcasted_iota(jnp.int32, (GD_INV_BLOCK, PAIR), 1) % GD_INV_BLOCK, 1.0, 0.0)
    dirs = ((qf_ref, kf_ref, vf_ref, of_ref), (qb_ref, kb_ref, vb_ref, ob_ref))
    units = [(d, p) for d in range(N_DIR) for p in range(GD_PAIRS)]

    def strip_of(bd):
        s = bd[0:GD_INV_BLOCK]
        for b in range(1, PAIR // GD_INV_BLOCK):
            s = s + bd[b * GD_INV_BLOCK:(b + 1) * GD_INV_BLOCK]
        return s

    def blockdiag_of(s):
        return jnp.where(same16, jnp.concatenate([s] * (PAIR // GD_INV_BLOCK), axis=0), jnp.zeros((), s.dtype))

    def precompute(t):
        masks = [_pair_masks(False), _pair_masks(True)]
        items = []
        for cc in range(GD_CHUNKS_PER_TRIP):
            step = t * GD_CHUNKS_PER_TRIP + cc
            for d in range(N_DIR):
                c = (chunks - 1 - step) if d else step
                for p in range(GD_PAIRS):
                    items.append(dict(d=d, p=p, c=c, r0=_aligned(c * CHUNK, CHUNK), row=pl.ds(c, 1),
                                      u=d * GD_PAIRS + p, cols=slice(p * PAIR, (p + 1) * PAIR)))
        for it in items:
            u, row = it["u"], it["row"]
            q_ref, k_ref = dirs[it["d"]][0], dirs[it["d"]][1]
            q_b = _stack_pair(q_ref[pl.ds(it["r0"], CHUNK), it["cols"]]).astype(BF16)
            k_b = _stack_pair(k_ref[pl.ds(it["r0"], CHUNK), it["cols"]]).astype(BF16)
            it["grams"] = _dot_nt(jnp.concatenate([q_b, k_b], axis=0), k_b)
            gc = gc_scr[u, row, :]
            it["osum"] = _dot_tn(*_outer_sum_operands(gc, -gc))
        yield
        for it in items:
            u, row = it["u"], it["row"]
            valid, strict = masks[it["d"]]
            decay = jnp.exp(jnp.where(valid, it.pop("osum"), NEG)) * beta_scr[u, row, :]
            grams = it.pop("grams")
            attn = (grams[:PAIR] * (HEAD_DIM ** -0.5) * decay).astype(BF16)
            q_scale = jnp.where(is_diag, qsc_scr[u, row, :], 0.0).astype(BF16)
            at_scr[u, it["c"]] = jnp.concatenate([attn, q_scale], axis=1)
            neg_a = jnp.where(strict, -(grams[PAIR:] * decay), 0.0)
            nd = jnp.where(same16, neg_a, 0.0)
            it["nl"] = (neg_a - nd).astype(BF16)
            nd_s = strip_of(nd)
            it["t_s"] = eye_s + nd_s
            it["pw_s"] = _dot(nd_s.astype(BF16), nd.astype(BF16))
        yield
        for _ in range(2):
            for it in items:
                pw_s = it["pw_s"].astype(BF16)
                both = _dot(jnp.concatenate([it["t_s"].astype(BF16), pw_s], axis=0), blockdiag_of(pw_s))
                it["t_s"] = it["t_s"] + both[:GD_INV_BLOCK]
                it["pw_s"] = both[GD_INV_BLOCK:]
        yield
        for it in items:
            pw_s = it.pop("pw_s").astype(BF16)
            t_s = it.pop("t_s")
            it["t_inv"] = blockdiag_of(t_s + _dot(t_s.astype(BF16), blockdiag_of(pw_s)))
        yield
        for it in items:
            it["x_b"] = it["t_inv"].astype(BF16)
            it["m"] = _dot(it["x_b"], it.pop("nl"))
        yield
        for it in items:
            m_b = it.pop("m").astype(BF16)
            both = _dot(m_b, jnp.concatenate([m_b, it.pop("x_b")], axis=1))
            it["m2"] = both[:, :PAIR]
            it["t_inv"] = it["t_inv"] + both[:, PAIR:]
        yield
        for it in items:
            it["t_inv"] = it["t_inv"] + _dot(it.pop("m2").astype(BF16), it["t_inv"].astype(BF16))
        yield
        for it in items:
            u, c, row = it["u"], it["c"], it["row"]
            q_ref, k_ref, v_ref = dirs[it["d"]][:3]
            rows = pl.ds(it["r0"], CHUNK)
            q_b = _stack_pair(q_ref[rows, it["cols"]]).astype(BF16)
            k_pair = _stack_pair(k_ref[rows, it["cols"]])
            v_b = _stack_pair(v_ref[rows, it["cols"]]).astype(BF16)
            t_inv = it.pop("t_inv")
            u_scr[u, c] = _dot(t_inv.astype(BF16), v_b)
            w = _dot((t_inv * egc_scr[u, row, :]).astype(BF16), k_pair.astype(BF16))
            wq_scr[u, c] = jnp.concatenate([w.astype(BF16), q_b], axis=0)
            ks_scr[u, c] = (k_pair.T * fb_scr[u, row, :]).astype(BF16)

    def scan(i):
        cs = (i, chunks - 1 - i)
        state = [s_scr[u] for u in range(len(units))]
        ws = [_dot(wq_scr[u, cs[d]], state[u].astype(BF16)) for u, (d, p) in enumerate(units)]
        yield
        v_qs = [jnp.concatenate([u_scr[u, cs[d]] - ws[u][:PAIR], ws[u][PAIR:]], axis=0).astype(BF16)
                for u, (d, p) in enumerate(units)]
        for u, (d, p) in enumerate(units):
            s_scr[u] = state[u] * eg_scr[u, pl.ds(cs[d], 1), :] + _dot(ks_scr[u, cs[d]], v_qs[u][:PAIR])
        for u, (d, p) in enumerate(units):
            o = _dot(at_scr[u, cs[d]], v_qs[u])
            r0 = _aligned(cs[d] * CHUNK, CHUNK)
            dirs[d][3][pl.ds(r0, CHUNK), p * PAIR:(p + 1) * PAIR] = o[:CHUNK] + o[CHUNK:]

    trips = chunks // GD_CHUNKS_PER_TRIP

    def scan_trip(t):
        for cc in range(GD_CHUNKS_PER_TRIP):
            yield from scan(t * GD_CHUNKS_PER_TRIP + cc)
            yield

    def fused(t, carry):
        filler = scan_trip(t - 1)
        for _ in precompute(t):
            next(filler, None)
        for _ in filler:
            pass
        return carry

    for _ in precompute(0):
        pass
    lax.fori_loop(1, trips, fused, 0)
    for _ in scan_trip(trips - 1):
        pass


def _gdn(q, k, v, gates, a_log, dt_bias):
    b, s, _ = q.shape
    tb = min(GDN_SEQ_BLOCK, s)
    nb = s // tb
    chunks = tb // CHUNK
    a_rows = _gate_rows(gates, GATE_GD_A, GDN_HEADS)
    b_rows = _gate_rows(gates, GATE_GD_B, GDN_HEADS)
    alr = _param_rows(a_log, GDN_HEADS)
    dtr = _param_rows(dt_bias, GDN_HEADS)

    def fwd(bi, j):
        return (bi, j, 0)

    def bwd(bi, j):
        return (bi, nb - 1 - j, 0)

    seq = lambda w, im: pl.BlockSpec((None, tb, w), im)
    rows_f = pl.BlockSpec((None, None, GD_PAIRS, chunks, PAIR), lambda bi, j: (bi, 0, 0, j, 0))
    rows_b = pl.BlockSpec((None, None, GD_PAIRS, chunks, PAIR), lambda bi, j: (bi, 1, 0, nb - 1 - j, 0))
    full = lambda a: pl.BlockSpec(a.shape, lambda bi, j: (0,) * a.ndim)
    out = jax.ShapeDtypeStruct((b, s, GDN_WIDTH), F32)
    rows = pltpu.VMEM((GD_UNITS, chunks, PAIR), F32)
    return pl.pallas_call(
        functools.partial(_gdn_kernel, chunks=chunks),
        out_shape=(out, out),
        grid=(b, nb),
        in_specs=[seq(GDN_WIDTH, fwd), seq(GDN_WIDTH, fwd), seq(GDN_WIDTH, fwd), rows_f, rows_f,
                  seq(GDN_WIDTH, bwd), seq(GDN_WIDTH, bwd), seq(GDN_WIDTH, bwd), rows_b, rows_b,
                  full(alr), full(dtr)],
        out_specs=(seq(GDN_WIDTH, fwd), seq(GDN_WIDTH, bwd)),
        scratch_shapes=[pltpu.VMEM((GD_UNITS, PAIR, PAIR), F32),
                        rows, rows, rows, rows, rows, rows,
                        pltpu.VMEM((GD_UNITS, chunks, PAIR, PAIR), F32),
                        pltpu.VMEM((GD_UNITS, chunks, 2 * PAIR, PAIR), BF16),
                        pltpu.VMEM((GD_UNITS, chunks, PAIR, 2 * PAIR), BF16),
                        pltpu.VMEM((GD_UNITS, chunks, PAIR, PAIR), BF16)],
        compiler_params=_params("parallel", "arbitrary"),
        name="gated_deltanet",
    )(q, k, v, a_rows, b_rows, q, k, v, a_rows, b_rows, alr, dtr)


def _out_proj_kernel(x_ref, na_ref, hf_ref, hb_ref, mo_ref, mw_ref, of_ref, ob_ref, gz_ref, gw_ref,
                     w_ref, o_ref):
    hs = hf_ref[...] + hb_ref[...]
    y_ml = hs * lax.rsqrt(_head_sumsq(hs) * (1.0 / HEAD_DIM) + EPS) * mw_ref[...] * _sigmoid(mo_ref[...])
    os_ = of_ref[...] + ob_ref[...]
    z = gz_ref[...]
    y_gd = os_ * lax.rsqrt(_head_sumsq(os_) * (1.0 / HEAD_DIM) + EPS) * gw_ref[...] * (z * _sigmoid(z))
    acc = x_ref[...] + _dot(na_ref[...], w_ref[0:NA_WIDTH, :])
    acc = acc + _dot(y_ml.astype(BF16), w_ref[NA_WIDTH:NA_WIDTH + ML_WIDTH, :])
    acc = acc + _dot(y_gd.astype(BF16), w_ref[NA_WIDTH + ML_WIDTH:, :])
    o_ref[...] = acc


def _out_proj(x2d, y_na, hf, hb, ml_o, ml_norm_w, of, ob, gd_z, gdn_norm_w, w_out):
    t = x2d.shape[0]
    tm = min(TM_ROWS, t)
    row = lambda w: pl.BlockSpec((tm, w), lambda i: (i, 0))
    const = lambda r, c: pl.BlockSpec((r, c), lambda i: (0, 0))
    return pl.pallas_call(
        _out_proj_kernel,
        out_shape=jax.ShapeDtypeStruct((t, D_MODEL), F32),
        grid=(t // tm,),
        in_specs=[row(D_MODEL), row(NA_WIDTH), row(ML_WIDTH), row(ML_WIDTH), row(ML_WIDTH), const(1, ML_WIDTH),
                  row(GDN_WIDTH), row(GDN_WIDTH), row(GDN_WIDTH), const(1, GDN_WIDTH),
                  const(D_MODEL, D_MODEL)],
        out_specs=row(D_MODEL),
        compiler_params=_params("parallel"),
        name="out_proj",
    )(x2d, y_na, hf, hb, ml_o, ml_norm_w.reshape(1, ML_WIDTH).astype(F32), of, ob, gd_z,
      gdn_norm_w.reshape(1, GDN_WIDTH).astype(F32), w_out.astype(BF16))


def _mem_kv_kernel(m_ref, nw_ref, w_ref, k_ref, v_ref):
    h = _rms(m_ref[...], nw_ref[...]).astype(BF16)
    k_ref[...] = _dot(h, w_ref[:, :D_MODEL]).astype(BF16)
    v_ref[...] = _dot(h, w_ref[:, D_MODEL:]).astype(BF16)


def _mem_kv(mem2d, norm_w, w_kv):
    t = mem2d.shape[0]
    tm = min(TM_ROWS, t)
    out = jax.ShapeDtypeStruct((t, D_MODEL), BF16)
    return pl.pallas_call(
        _mem_kv_kernel,
        out_shape=(out, out),
        grid=(t // tm,),
        in_specs=[pl.BlockSpec((tm, D_MODEL), lambda i: (i, 0)),
                  pl.BlockSpec((1, D_MODEL), lambda i: (0, 0)),
                  pl.BlockSpec((D_MODEL, 2 * D_MODEL), lambda i: (0, 0))],
        out_specs=(pl.BlockSpec((tm, D_MODEL), lambda i: (i, 0)), pl.BlockSpec((tm, D_MODEL), lambda i: (i, 0))),
        compiler_params=_params("parallel"),
        name="mem_kv",
    )(mem2d, norm_w.reshape(1, D_MODEL).astype(F32), w_kv.astype(BF16))


def _xattn_kernel(x_ref, nw_ref, wq_ref, k_ref, v_ref, wo_ref, o_ref):
    x = x_ref[...]
    q = _dot(_rms(x, nw_ref[...]).astype(BF16), wq_ref[...]).astype(BF16)
    heads = [slice(h * XA_HEAD_DIM, (h + 1) * XA_HEAD_DIM) for h in range(XA_HEADS)]
    s = [_dot_nt(q[:, cols], k_ref[:, cols]) * (XA_HEAD_DIM ** -0.5) for cols in heads]
    e = [jnp.exp(s_h - jnp.max(s_h, axis=-1, keepdims=True)) for s_h in s]
    inv_l = [1.0 / jnp.sum(e_h, axis=-1, keepdims=True) for e_h in e]
    o = [(_dot(e_h.astype(BF16), v_ref[:, cols]) * il).astype(BF16) for e_h, il, cols in zip(e, inv_l, heads)]
    o_ref[...] = x + _dot(jnp.concatenate(o, axis=1), wo_ref[...])


def _xattn(x3d, norm_w, w_q, k, v, w_o):
    b, s, _ = x3d.shape
    tm = min(TM_ROWS, s)
    n_mem = k.shape[1]
    return pl.pallas_call(
        _xattn_kernel,
        out_shape=jax.ShapeDtypeStruct(x3d.shape, F32),
        grid=(b, s // tm),
        in_specs=[pl.BlockSpec((None, tm, D_MODEL), lambda bi, i: (bi, i, 0)),
                  pl.BlockSpec((1, D_MODEL), lambda bi, i: (0, 0)),
                  pl.BlockSpec((D_MODEL, D_MODEL), lambda bi, i: (0, 0)),
                  pl.BlockSpec((None, n_mem, D_MODEL), lambda bi, i: (bi, 0, 0)),
                  pl.BlockSpec((None, n_mem, D_MODEL), lambda bi, i: (bi, 0, 0)),
                  pl.BlockSpec((D_MODEL, D_MODEL), lambda bi, i: (0, 0))],
        out_specs=pl.BlockSpec((None, tm, D_MODEL), lambda bi, i: (bi, i, 0)),
        compiler_params=_params("parallel", "parallel"),
        name="cross_attention",
    )(x3d, norm_w.reshape(1, D_MODEL).astype(F32), w_q.astype(BF16), k, v, w_o.astype(BF16))


FF_CHUNK = 512


def _ffn_kernel(x_ref, nw_ref, w1_ref, w2_ref, fw_ref, o_ref, *, final_norm):
    x = x_ref[...]
    h = _rms(x, nw_ref[...]).astype(BF16)
    acc = x
    for c0 in range(0, D_FF, FF_CHUNK):
        a = jnp.maximum(_dot(h, w1_ref[:, c0:c0 + FF_CHUNK]), 0.0)
        acc = acc + _dot((a * a).astype(BF16), w2_ref[c0:c0 + FF_CHUNK, :])
    o_ref[...] = _rms(acc, fw_ref[...]) if final_norm else acc


def _ffn(x2d, norm_w, w1, w2, final_w, final_norm):
    t = x2d.shape[0]
    tm = min(TM_ROWS, t)
    return pl.pallas_call(
        functools.partial(_ffn_kernel, final_norm=final_norm),
        out_shape=jax.ShapeDtypeStruct((t, D_MODEL), F32),
        grid=(t // tm,),
        in_specs=[pl.BlockSpec((tm, D_MODEL), lambda i: (i, 0)),
                  pl.BlockSpec((1, D_MODEL), lambda i: (0, 0)),
                  pl.BlockSpec((D_MODEL, D_FF), lambda i: (0, 0), pipeline_mode=pl.Buffered(1)),
                  pl.BlockSpec((D_FF, D_MODEL), lambda i: (0, 0), pipeline_mode=pl.Buffered(1)),
                  pl.BlockSpec((1, D_MODEL), lambda i: (0, 0))],
        out_specs=pl.BlockSpec((tm, D_MODEL), lambda i: (i, 0)),
        compiler_params=_params("parallel"),
        name="ffn",
    )(x2d, norm_w.reshape(1, D_MODEL).astype(F32), w1.astype(BF16), w2.astype(BF16),
      final_w.reshape(1, D_MODEL).astype(F32))


def kernel(x, mem, norm_mix_w, w_in, na_rel_bias, ml_i_bias, ml_f_bias, ml_norm_w, gdn_conv_w, gdn_a_log,
           gdn_dt_bias, gdn_norm_w, w_out, norm_xa_w, norm_mem_w, w_xq, w_xkv, w_xo, norm_ffn_w, w_ff1,
           w_ff2, norm_out_w):
    b, s, d = x.shape
    depth = w_in.shape[0]
    x2d = x.reshape(b * s, d).astype(F32)
    mem2d = mem.reshape(-1, d).astype(F32)
    for l in range(depth):
        (na_q, na_k, na_v, ml_q, ml_k, ml_v, ml_o, gd_qkv, gd_z, gates) = _in_proj(
            x2d, norm_mix_w[l].astype(F32), _permute_w_in(w_in[l]))
        seq = lambda a: a.reshape(b, s, a.shape[-1])
        gates3 = seq(gates)
        y_na = _neighbourhood_attention(seq(na_q), seq(na_k), seq(na_v), _na_bias_table(na_rel_bias[l]))
        hf, hb = _mlstm(seq(ml_q), seq(ml_k), seq(ml_v), gates3, ml_i_bias[l], ml_f_bias[l])
        gq, gk, gv = _gdn_prep(seq(gd_qkv), gdn_conv_w[l])
        of, ob = _gdn(gq, gk, gv, gates3, gdn_a_log[l], gdn_dt_bias[l])
        flat = lambda a: a.reshape(b * s, a.shape[-1])
        x2d = _out_proj(x2d, flat(y_na), flat(hf), flat(hb), ml_o, ml_norm_w[l], flat(of), flat(ob), gd_z,
                        gdn_norm_w[l], w_out[l])
        mk, mv = _mem_kv(mem2d, norm_mem_w[l], w_xkv[l])
        n_mem = mem.shape[1]
        x2d = _xattn(x2d.reshape(b, s, d), norm_xa_w[l], w_xq[l], mk.reshape(b, n_mem, d),
                     mv.reshape(b, n_mem, d), w_xo[l]).reshape(b * s, d)
        x2d = _ffn(x2d, norm_ffn_w[l], w_ff1[l], w_ff2[l], norm_out_w, final_norm=(l == depth - 1))
    return x2d.reshape(b, s, d).astype(x.dtype)
```

```python
import functools

import numpy as np
import jax
import jax.numpy as jnp
from jax import lax
from jax.experimental import pallas as pl
from jax.experimental.pallas import tpu as pltpu

F32 = jnp.float32
BF16 = jnp.bfloat16

D_MODEL = 1024
HEAD_DIM = 64
GRID_W = 64
NA_HEADS = 6
NA_WIN_ROWS = 8
NA_WIN_COLS = 16
ML_HEADS = 4
GDN_HEADS = 6
CONV_K = 5
N_DIR = 2
XA_HEADS = 4
XA_HEAD_DIM = D_MODEL // XA_HEADS
D_FF = 4 * D_MODEL
NA_WIDTH = NA_HEADS * HEAD_DIM
ML_WIDTH = ML_HEADS * HEAD_DIM
GDN_WIDTH = GDN_HEADS * HEAD_DIM
EPS = 1e-6
CHUNK = 64
PAIR = 2 * HEAD_DIM
NEG = -1e30

IN_SIZES = (NA_WIDTH, NA_WIDTH, NA_WIDTH,
            ML_WIDTH, ML_WIDTH, ML_WIDTH, ML_WIDTH, N_DIR * ML_HEADS, N_DIR * ML_HEADS,
            GDN_WIDTH, GDN_WIDTH, GDN_WIDTH, GDN_WIDTH, N_DIR * GDN_HEADS, N_DIR * GDN_HEADS)

GATE_GD_B = 0
GATE_GD_A = GATE_GD_B + N_DIR * GDN_HEADS
GATE_ML_I = GATE_GD_A + N_DIR * GDN_HEADS
GATE_ML_F = GATE_ML_I + N_DIR * ML_HEADS
GATE_USED = GATE_ML_F + N_DIR * ML_HEADS
GATE_LANES = 128

V7X_VMEM_LIMIT = 56 * 1024 * 1024
MXU_COLS = 256

TM_ROWS = 1024
NA_ROWS_PER_STEP = 8
SEQ_BLOCK = 1024
GDN_SEQ_BLOCK = 1024


def _params(*sem):
    return pltpu.CompilerParams(dimension_semantics=sem, vmem_limit_bytes=V7X_VMEM_LIMIT)


def _dot(a, b):
    return jnp.dot(a, b, preferred_element_type=F32)


def _dot_nt(a, b):
    return lax.dot_general(a, b, (((1,), (1,)), ((), ())), preferred_element_type=F32)


def _dot_tn(a, b):
    return lax.dot_general(a, b, (((0,), (0,)), ((), ())), preferred_element_type=F32)


def _split3(x):
    x1 = x.astype(BF16)
    r1 = x - x1.astype(F32)
    x2 = r1.astype(BF16)
    x3 = (r1 - x2.astype(F32)).astype(BF16)
    return x1, x2, x3


def _sum_right(x, m01):
    x1 = x.astype(BF16)
    x2 = (x - x1.astype(F32)).astype(BF16)
    return _dot(x1, m01) + _dot(x2, m01)


def _exact_right(x, m01):
    x1, x2, x3 = _split3(x)
    return _dot(x1, m01) + _dot(x2, m01) + _dot(x3, m01)


def _rms(x, w):
    ms = jnp.mean(x * x, axis=-1, keepdims=True)
    return x * lax.rsqrt(ms + EPS) * w


def _softplus(x):
    return jnp.maximum(x, 0.0) + jnp.log1p(jnp.exp(-jnp.abs(x)))


def _sigmoid(x):
    return 1.0 / (1.0 + jnp.exp(-x))


def _segment_mean_matrix(width):
    r = lax.broadcasted_iota(jnp.int32, (width, width), 0) // HEAD_DIM
    c = lax.broadcasted_iota(jnp.int32, (width, width), 1) // HEAD_DIM
    return jnp.where(r == c, 1.0, 0.0).astype(BF16)


def _head_sumsq(t):
    return _sum_right(t * t, _segment_mean_matrix(t.shape[-1]))


def _aligned(x, m):
    return x if isinstance(x, int) else pl.multiple_of(x, m)


def _pair_masks(rev):
    r = lax.broadcasted_iota(jnp.int32, (PAIR, PAIR), 0)
    c = lax.broadcasted_iota(jnp.int32, (PAIR, PAIR), 1)
    same = (r // HEAD_DIM) == (c // HEAD_DIM)
    t, s = r % HEAD_DIM, c % HEAD_DIM
    if rev:
        return same & (s >= t), same & (s > t)
    return same & (s <= t), same & (s < t)


def _stack_pair(x2):
    lane = lax.broadcasted_iota(jnp.int32, x2.shape, 1)
    zero = jnp.zeros_like(x2)
    return jnp.concatenate([jnp.where(lane < HEAD_DIM, x2, zero),
                            jnp.where(lane >= HEAD_DIM, x2, zero)], axis=0)


GD_PAIRS = GDN_HEADS // 2
CONV_HALO = 8
PREP_ROWS = 128
IN_SEGMENTS = (
    (NA_WIDTH, BF16), (NA_WIDTH, BF16), (NA_WIDTH, BF16),
    (ML_WIDTH, BF16), (ML_WIDTH, BF16), (ML_WIDTH, BF16), (ML_WIDTH, F32),
    (GDN_WIDTH, F32), (GATE_LANES, F32))
CONV_COLS = 3 * GDN_WIDTH
IN_COLS = CONV_COLS + sum(w for w, _ in IN_SEGMENTS)
IN_DOT_COLS = 3 * MXU_COLS


def _in_proj_kernel(x_ref, prev_ref, next_ref, nw_ref, w_ref, cw_ref, gq_ref, gk_ref, gv_ref, *rest,
                    tm, seq_len):
    out_refs, ext_scr = rest[:-1], rest[-1]
    i = pl.program_id(0)
    nw = nw_ref[...]
    h = _rms(x_ref[...], nw).astype(BF16)
    blocks_per_seq = seq_len // tm
    has_prev = (i % blocks_per_seq) > 0
    has_next = (i % blocks_per_seq) < blocks_per_seq - 1
    halo = _rms(jnp.concatenate([prev_ref[...], next_ref[...]], axis=0), nw).astype(BF16)
    w_conv = w_ref[:, 0:CONV_COLS]
    halo_p = _dot(halo, w_conv)
    ext_scr[0:CONV_HALO, :] = jnp.where(has_prev, halo_p[:CONV_HALO], 0.0)
    ext_scr[CONV_HALO + tm:, :] = jnp.where(has_next, halo_p[CONV_HALO:], 0.0)
    ext_scr[CONV_HALO:CONV_HALO + tm, :] = _dot(h, w_conv)

    base = CONV_HALO - CONV_K // 2
    seg = _segment_mean_matrix(PAIR)
    taps = [cw_ref[t:t + 1, :] for t in range(CONV_K)]

    def conv_tiles():
        for rb in range(tm // PREP_ROWS):
            r0 = rb * PREP_ROWS
            for ct in range(CONV_COLS // PAIR):
                cols = slice(ct * PAIR, (ct + 1) * PAIR)
                y = ext_scr[r0 + base:r0 + base + PREP_ROWS, cols] * taps[0][:, cols]
                for t in range(1, CONV_K):
                    y = y + ext_scr[r0 + base + t:r0 + base + t + PREP_ROWS, cols] * taps[t][:, cols]
                y = y * _sigmoid(y)
                which, off = divmod(ct * PAIR, GDN_WIDTH)
                if which < 2:
                    y = y * lax.rsqrt(_sum_right(y * y, seg) + EPS)
                (gq_ref, gk_ref, gv_ref)[which][r0:r0 + PREP_ROWS, off:off + PAIR] = y
                yield

    tiles = conv_tiles()
    n_tiles = (tm // PREP_ROWS) * (CONV_COLS // PAIR)
    starts = CONV_COLS + np.cumsum([0] + [w for w, _ in IN_SEGMENTS])
    chunk_starts = list(range(CONV_COLS, IN_COLS, IN_DOT_COLS))
    for k, c0 in enumerate(chunk_starts):
        c1 = min(c0 + IN_DOT_COLS, IN_COLS)
        acc = _dot(h, w_ref[:, c0:c1])
        for o_ref, s0, s1 in zip(out_refs, starts[:-1], starts[1:]):
            a, b = max(c0, int(s0)), min(c1, int(s1))
            if a < b:
                o_ref[:, a - int(s0):b - int(s0)] = acc[:, a - c0:b - c0].astype(o_ref.dtype)
        for _ in range(-(-n_tiles // len(chunk_starts))):
            next(tiles, None)
    for _ in tiles:
        pass


def _in_proj(x2d, norm_w, w_perm, conv_w, seq_len):
    t = x2d.shape[0]
    tm = min(TM_ROWS, seq_len)
    hb = tm // CONV_HALO
    last = t // CONV_HALO - 1
    gd = jax.ShapeDtypeStruct((t, GDN_WIDTH), F32)
    outs = (gd, gd, gd) + tuple(jax.ShapeDtypeStruct((t, w), dt) for w, dt in IN_SEGMENTS)
    row = lambda w: pl.BlockSpec((tm, w), lambda i: (i, 0))
    return pl.pallas_call(
        functools.partial(_in_proj_kernel, tm=tm, seq_len=seq_len),
        out_shape=outs,
        grid=(t // tm,),
        in_specs=[row(D_MODEL),
                  pl.BlockSpec((CONV_HALO, D_MODEL), lambda i: (jnp.maximum(i * hb - 1, 0), 0)),
                  pl.BlockSpec((CONV_HALO, D_MODEL), lambda i: (jnp.minimum((i + 1) * hb, last), 0)),
                  pl.BlockSpec((1, D_MODEL), lambda i: (0, 0)),
                  pl.BlockSpec((D_MODEL, IN_COLS), lambda i: (0, 0), pipeline_mode=pl.Buffered(1)),
                  pl.BlockSpec((CONV_K, CONV_COLS), lambda i: (0, 0))],
        out_specs=(row(GDN_WIDTH), row(GDN_WIDTH), row(GDN_WIDTH)) + tuple(row(w) for w, _ in IN_SEGMENTS),
        scratch_shapes=[pltpu.VMEM((tm + 2 * CONV_HALO, CONV_COLS), F32)],
        compiler_params=_params("parallel"),
        name="in_proj",
    )(x2d, x2d, x2d, norm_w.reshape(1, D_MODEL), w_perm, conv_w.astype(F32))


def _permute_w_in(w):
    parts = jnp.split(w, np.cumsum(IN_SIZES)[:-1], axis=-1)
    (na_q, na_k, na_v, ml_q, ml_k, ml_v, ml_o, ml_i, ml_f, gd_q, gd_k, gd_v, gd_z, gd_b, gd_a) = parts
    pad = jnp.zeros(w.shape[:-1] + (GATE_LANES - GATE_USED,), w.dtype)
    return jnp.concatenate([gd_q, gd_k, gd_v, na_q, na_k, na_v, ml_q, ml_k, ml_v, ml_o, gd_z,
                            gd_b, gd_a, ml_i, ml_f, pad], axis=-1).astype(BF16)


NA_BIAS_ROWS = 96


def _na_bias_kernel(rb_ref, o_ref):
    n = GRID_W * GRID_W
    dc = lax.broadcasted_iota(jnp.int32, (GATE_LANES, n), 0)
    col = lax.broadcasted_iota(jnp.int32, (GATE_LANES, n), 1)
    q, kc = col // GRID_W, col % GRID_W
    c0 = jnp.clip(q - NA_WIN_COLS // 2, 0, GRID_W - NA_WIN_COLS)
    valid = (kc >= c0) & (kc < c0 + NA_WIN_COLS)
    onehot = jnp.where(valid & (kc - q + (NA_WIN_COLS - 1) == dc), 1.0, 0.0).astype(BF16)
    o_ref[...] = jnp.where(valid[0:1, :], _exact_right(rb_ref[...], onehot), NEG)


def _na_bias_table(rel_bias):
    nh, ndr, ndc = rel_bias.shape
    rb = jnp.zeros((NA_BIAS_ROWS, GATE_LANES), F32).at[:nh * ndr, :ndc].set(
        rel_bias.astype(F32).reshape(nh * ndr, ndc))
    band = pl.pallas_call(
        _na_bias_kernel,
        out_shape=jax.ShapeDtypeStruct((NA_BIAS_ROWS, GRID_W * GRID_W), F32),
        name="na_bias_expand",
    )(rb)
    band = band[:nh * ndr].reshape(nh, ndr, GRID_W, GRID_W)
    first = np.clip(np.arange(ndr + 1) - 1, 0, ndr - 1)
    second = np.clip(np.arange(ndr + 1), 0, ndr - 1)
    tab = jnp.concatenate([band[:, first], band[:, second]], axis=-1)
    tab = tab.reshape(NA_HEADS // 2, 2, ndr + 1, GRID_W, PAIR)
    return jnp.moveaxis(tab, 1, 2).reshape(NA_HEADS // 2, ndr + 1, 2 * GRID_W, PAIR)


NA_ITEM_ROWS = 4
NA_KEY_ROWS = NA_ITEM_ROWS + NA_WIN_ROWS


def _na_kernel(q_ref, k_ref, v_ref, bias_ref, o_ref, *, rows, rows_per_step):
    j = pl.program_id(1)
    nkeys = NA_KEY_ROWS * GRID_W
    lane = lax.broadcasted_iota(jnp.int32, (GRID_W, PAIR), 1)
    lane_blk = lax.broadcasted_iota(jnp.int32, (PAIR, PAIR), 1)
    items = []
    for g in range(rows_per_step // NA_ITEM_ROWS):
        r_first = j * rows_per_step + g * NA_ITEM_ROWS
        kr0 = jnp.clip(r_first - NA_WIN_ROWS // 2, 0, rows - NA_KEY_ROWS)
        koff = pl.multiple_of(kr0 * GRID_W, GRID_W)
        entry, pen = {}, {}
        for i in range(NA_ITEM_ROWS):
            r = r_first + i
            r0 = jnp.clip(r - NA_WIN_ROWS // 2, 0, rows - NA_WIN_ROWS)
            for jp in range(NA_KEY_ROWS // 2):
                key_a = kr0 + 2 * jp
                ok_a = (key_a >= r0) & (key_a < r0 + NA_WIN_ROWS)
                ok_b = (key_a + 1 >= r0) & (key_a + 1 < r0 + NA_WIN_ROWS)
                entry[i, jp] = jnp.clip(key_a - r + (NA_WIN_ROWS - 1), -1, 2 * NA_WIN_ROWS - 2) + 1
                pen[i, jp] = jnp.where(lane_blk < GRID_W, jnp.where(ok_a, 0.0, NEG), jnp.where(ok_b, 0.0, NEG))
        for p in range(NA_HEADS // 2):
            items.append(dict(g=g, p=p, koff=koff, entry=entry, pen=pen, cols=slice(p * PAIR, (p + 1) * PAIR)))
    for it in items:
        q0 = it["g"] * NA_ITEM_ROWS * GRID_W
        q_lhs = jnp.concatenate(
            [_stack_pair(q_ref[q0 + i * GRID_W:q0 + (i + 1) * GRID_W, it["cols"]]) for i in range(NA_ITEM_ROWS)],
            axis=0)
        q_lhs = (q_lhs.astype(F32) * (HEAD_DIM ** -0.5)).astype(BF16)
        it["s"] = _dot_nt(q_lhs, k_ref[pl.ds(it["koff"], nkeys), it["cols"]])
    for it in items:
        s = it.pop("s")
        blocks = []
        for i in range(NA_ITEM_ROWS):
            blocks.append(jnp.concatenate(
                [s[i * PAIR:(i + 1) * PAIR, jp * PAIR:(jp + 1) * PAIR] + bias_ref[it["p"], it["entry"][i, jp]]
                 + it["pen"][i, jp] for jp in range(NA_KEY_ROWS // 2)], axis=1))
        s = jnp.concatenate(blocks, axis=0)
        e = jnp.exp(s - jnp.max(s, axis=-1, keepdims=True))
        it["l"] = jnp.sum(e, axis=-1, keepdims=True)
        it["o"] = _dot(e.astype(BF16), v_ref[pl.ds(it["koff"], nkeys), it["cols"]])
    for it in items:
        o = it.pop("o") * (1.0 / it.pop("l"))
        q0 = it["g"] * NA_ITEM_ROWS * GRID_W
        for i in range(NA_ITEM_ROWS):
            blk = o[i * PAIR:(i + 1) * PAIR]
            o_ref[q0 + i * GRID_W:q0 + (i + 1) * GRID_W, it["cols"]] = jnp.where(
                lane < HEAD_DIM, blk[:GRID_W], blk[GRID_W:]).astype(o_ref.dtype)


def _neighbourhood_attention(q, k, v, bias_tab):
    b, s, _ = q.shape
    rows = s // GRID_W
    assert rows >= NA_KEY_ROWS
    rps = min(NA_ROWS_PER_STEP, rows)
    tq = rps * GRID_W
    return pl.pallas_call(
        functools.partial(_na_kernel, rows=rows, rows_per_step=rps),
        out_shape=jax.ShapeDtypeStruct((b, s, NA_WIDTH), BF16),
        grid=(b, rows // rps),
        in_specs=[pl.BlockSpec((None, tq, NA_WIDTH), lambda bi, j: (bi, j, 0)),
                  pl.BlockSpec((None, s, NA_WIDTH), lambda bi, j: (bi, 0, 0)),
                  pl.BlockSpec((None, s, NA_WIDTH), lambda bi, j: (bi, 0, 0)),
                  pl.BlockSpec(bias_tab.shape, lambda bi, j: (0, 0, 0, 0))],
        out_specs=pl.BlockSpec((None, tq, NA_WIDTH), lambda bi, j: (bi, j, 0)),
        compiler_params=_params("parallel", "arbitrary"),
        name="neighbourhood_attention",
    )(q, k, v, bias_tab)


def _gate_rows(gates):
    b, s, _ = gates.shape
    g = gates[:, :, :GATE_USED].reshape(b, s // CHUNK, CHUNK, GATE_USED // 2, 2)
    return jnp.transpose(g, (0, 3, 1, 4, 2)).reshape(b, GATE_USED // 2, s // CHUNK, PAIR)


def _gate_row_spec(lane0, heads, d, chunks, chunk_block):
    pairs = heads // 2
    unit_block = (lane0 // 2 + d * pairs) // pairs
    assert unit_block * pairs == lane0 // 2 + d * pairs
    return pl.BlockSpec((None, pairs, chunks, PAIR), lambda bi, j: (bi, unit_block, chunk_block(j), 0))


def _param_rows(p, heads):
    return jnp.repeat(p.astype(F32).reshape(N_DIR, heads // 2, 2), HEAD_DIM, axis=-1).reshape(
        N_DIR, heads // 2, 1, PAIR)


def _pair_cumsum_matrix(rev):
    r = lax.broadcasted_iota(jnp.int32, (PAIR, PAIR), 0)
    c = lax.broadcasted_iota(jnp.int32, (PAIR, PAIR), 1)
    same = (r // HEAD_DIM) == (c // HEAD_DIM)
    sp, s = r % HEAD_DIM, c % HEAD_DIM
    return jnp.where(same & ((sp >= s) if rev else (sp <= s)), 1.0, 0.0).astype(BF16)


ML_PAIRS = ML_HEADS // 2
ML_UNITS = N_DIR * ML_PAIRS
ML_CHUNKS_PER_TRIP = 4
ML_STEPS_PER_TRIP = 4
OS_ROWS = 16


def _outer_sum_operands(col_term, row_term):
    ones = jnp.ones((3, PAIR), BF16)
    zeros = jnp.zeros((OS_ROWS - 6, PAIR), BF16)
    a = jnp.concatenate(list(_split3(col_term)) + [ones, zeros], axis=0)
    b = jnp.concatenate([ones] + list(_split3(row_term)) + [zeros], axis=0)
    return a, b


def _mlstm_kernel(qf_ref, kf_ref, vf_ref, if_ref, ff_ref,
                  qb_ref, kb_ref, vb_ref, ib_ref, fb_ref,
                  ibr_ref, fbr_ref,
                  hf_ref, hb_ref,
                  c_scr, m_scr, cs_scr, rt_scr, ai_scr, em_scr, we_scr, a_scr, g_scr, vt_scr, ut_scr, *, chunks):
    j = pl.program_id(1)

    @pl.when(j == 0)
    def _():
        c_scr[...] = jnp.zeros_like(c_scr)
        m_scr[...] = jnp.zeros_like(m_scr)

    lane = lax.broadcasted_iota(jnp.int32, (chunks, PAIR), 1)
    pos = lane % HEAD_DIM
    lo = lane < HEAD_DIM
    for d, (i_ref, f_ref) in enumerate(((if_ref, ff_ref), (ib_ref, fb_ref))):
        rev = bool(d)
        ucum = _pair_cumsum_matrix(rev)
        last = 0 if rev else CHUNK - 1
        for p in range(ML_PAIRS):
            u = d * ML_PAIRS + p
            logf = -_softplus(-(f_ref[p] + fbr_ref[d, p]))
            bcum = _exact_right(logf, ucum)
            cs = i_ref[p] + ibr_ref[d, p] - bcum
            bl = jnp.where(lo, bcum[:, last:last + 1], bcum[:, HEAD_DIM + last:HEAD_DIM + last + 1])
            cm = cs
            for k in (1, 2, 4, 8, 16, 32):
                if rev:
                    cm = jnp.where(pos < HEAD_DIM - k, jnp.maximum(cm, pltpu.roll(cm, PAIR - k, axis=1)), cm)
                else:
                    cm = jnp.where(pos >= k, jnp.maximum(cm, pltpu.roll(cm, k, axis=1)), cm)
            w = bl + cs
            m_loc = jnp.where(lo, jnp.max(jnp.where(lo, w, NEG), axis=-1, keepdims=True),
                              jnp.max(jnp.where(lo, NEG, w), axis=-1, keepdims=True))
            m = m_scr[u]
            m_prev, a_rows, g_rows = [None] * chunks, [None] * chunks, [None] * chunks
            for c in (range(chunks - 1, -1, -1) if rev else range(chunks)):
                m_prev[c] = m
                m_new = jnp.maximum(bl[c:c + 1] + m, m_loc[c:c + 1])
                a_rows[c] = jnp.exp(bl[c:c + 1] + m - m_new)
                g_rows[c] = jnp.exp(m_loc[c:c + 1] - m_new)
                m = m_new
            m_scr[u] = m
            inter = bcum + jnp.concatenate(m_prev, axis=0)
            m_t = jnp.maximum(bcum + cm, inter)
            cs_scr[u] = cs
            rt_scr[u] = bcum - m_t
            ai_scr[u] = jnp.exp(inter - m_t)
            em_scr[u] = jnp.exp(-m_t)
            we_scr[u] = jnp.exp(w - m_loc)
            a_scr[u] = jnp.concatenate(a_rows, axis=0)
            g_scr[u] = jnp.concatenate(g_rows, axis=0)

    dirs = ((qf_ref, kf_ref, vf_ref, hf_ref), (qb_ref, kb_ref, vb_ref, hb_ref))
    units = [(d, p) for d in range(N_DIR) for p in range(ML_PAIRS)]
    lane64 = lax.broadcasted_iota(jnp.int32, (CHUNK, PAIR), 1)
    one_hi = jnp.where(lane64 == HEAD_DIM, 1.0, 0.0).astype(BF16)
    one_lo = jnp.where(lane64 == 0, 1.0, 0.0).astype(BF16)
    lane_row = lax.broadcasted_iota(jnp.int32, (1, PAIR), 1)

    def key_tile(d, p, r0):
        k_pair = _stack_pair(dirs[d][1][pl.ds(r0, CHUNK), p * PAIR:(p + 1) * PAIR])
        return (k_pair.astype(F32) * (HEAD_DIM ** -0.5)).astype(BF16)

    def contributions(t, carry):
        for cc in range(ML_CHUNKS_PER_TRIP):
            c = t * ML_CHUNKS_PER_TRIP + cc
            r0 = pl.multiple_of(c * CHUNK, CHUNK)
            for u, (d, p) in enumerate(units):
                v2 = dirs[d][2][pl.ds(r0, CHUNK), p * PAIR:(p + 1) * PAIR]
                v_ext = jnp.concatenate([jnp.where(lane64 < HEAD_DIM, v2, one_hi),
                                         jnp.where(lane64 >= HEAD_DIM, v2, one_lo)], axis=0)
                v_t = v_ext.astype(F32).T
                vt_scr[u, c] = v_t.astype(BF16)
                ut_scr[u, c] = _dot((v_t * we_scr[u, pl.ds(c, 1), :]).astype(BF16), key_tile(d, p, r0))
        return carry

    lax.fori_loop(0, chunks // ML_CHUNKS_PER_TRIP, contributions, 0)

    def outputs(t, carry):
        masks = [_pair_masks(True)[0], _pair_masks(False)[0]]
        items = []
        for u, (d, p) in enumerate(units):
            ct = c_scr[u]
            for k in range(ML_STEPS_PER_TRIP):
                step = t * ML_STEPS_PER_TRIP + k
                c = (chunks - 1 - step) if d else step
                r0 = pl.multiple_of(c * CHUNK, CHUNK)
                row = pl.ds(c, 1)
                items.append(dict(u=u, d=d, p=p, c=c, r0=r0, row=row, ct=ct.astype(BF16)))
                ct = a_scr[u, row, :] * ct + g_scr[u, row, :] * ut_scr[u, c]
            c_scr[u] = ct
        for it in items:
            u, d, p, r0, row = it["u"], it["d"], it["p"], it["r0"], it["row"]
            q_pair = _stack_pair(dirs[d][0][pl.ds(r0, CHUNK), p * PAIR:(p + 1) * PAIR])
            os_a, os_b = _outer_sum_operands(cs_scr[u, row, :], rt_scr[u, row, :])
            it["gram"] = _dot_nt(key_tile(d, p, r0), q_pair)
            it["osum"] = _dot_tn(os_a, os_b)
            it["st"] = _dot_nt(it.pop("ct"), q_pair)
        for it in items:
            u, c = it["u"], it["c"]
            s_t = it.pop("gram") * jnp.exp(jnp.where(masks[it["d"]], it.pop("osum"), NEG))
            it["intra"] = jnp.sum(s_t, axis=0, keepdims=True)
            it["num"] = _dot(vt_scr[u, c], s_t.astype(BF16))
        for it in items:
            u, d, p, row = it["u"], it["d"], it["p"], it["row"]
            a_inter = ai_scr[u, row, :]
            st = it.pop("st")
            den = a_inter * jnp.where(lane_row < HEAD_DIM, st[HEAD_DIM:HEAD_DIM + 1], st[0:1]) + it.pop("intra")
            out_t = (it.pop("num") + a_inter * st) * (1.0 / jnp.maximum(jnp.abs(den), em_scr[u, row, :]))
            out = out_t.T
            dirs[d][3][pl.ds(it["r0"], CHUNK), p * PAIR:(p + 1) * PAIR] = jnp.where(
                lane64 < HEAD_DIM, out[:CHUNK], out[CHUNK:])
        return carry

    lax.fori_loop(0, chunks // ML_STEPS_PER_TRIP, outputs, 0)


def _mlstm(q, k, v, gate_rows, i_bias, f_bias):
    b, s, _ = q.shape
    tb = min(SEQ_BLOCK, s)
    nb = s // tb
    chunks = tb // CHUNK
    ibr = _param_rows(i_bias, ML_HEADS)
    fbr = _param_rows(f_bias, ML_HEADS)

    def fwd(bi, j):
        return (bi, j, 0)

    def bwd(bi, j):
        return (bi, nb - 1 - j, 0)

    seq = lambda w, im: pl.BlockSpec((None, tb, w), im)
    rows = lambda lane0, d: _gate_row_spec(lane0, ML_HEADS, d, chunks, (lambda j: nb - 1 - j) if d else (lambda j: j))
    full = lambda a: pl.BlockSpec(a.shape, lambda bi, j: (0,) * a.ndim)
    out = jax.ShapeDtypeStruct((b, s, ML_WIDTH), F32)
    rows_scr = pltpu.VMEM((ML_UNITS, chunks, PAIR), F32)
    return pl.pallas_call(
        functools.partial(_mlstm_kernel, chunks=chunks),
        out_shape=(out, out),
        grid=(b, nb),
        in_specs=[seq(ML_WIDTH, fwd), seq(ML_WIDTH, fwd), seq(ML_WIDTH, fwd), rows(GATE_ML_I, 0), rows(GATE_ML_F, 0),
                  seq(ML_WIDTH, bwd), seq(ML_WIDTH, bwd), seq(ML_WIDTH, bwd), rows(GATE_ML_I, 1), rows(GATE_ML_F, 1),
                  full(ibr), full(fbr)],
        out_specs=(seq(ML_WIDTH, fwd), seq(ML_WIDTH, bwd)),
        scratch_shapes=[pltpu.VMEM((ML_UNITS, PAIR, PAIR), F32),
                        pltpu.VMEM((ML_UNITS, 1, PAIR), F32),
                        rows_scr, rows_scr, rows_scr, rows_scr, rows_scr, rows_scr, rows_scr,
                        pltpu.VMEM((ML_UNITS, chunks, PAIR, PAIR), BF16),
                        pltpu.VMEM((ML_UNITS, chunks, PAIR, PAIR), F32)],
        compiler_params=_params("parallel", "arbitrary"),
        name="mlstm",
    )(q, k, v, gate_rows, gate_rows, q, k, v, gate_rows, gate_rows, ibr, fbr)


GD_CHUNKS_PER_TRIP = 2
GD_INV_BLOCK = 16
GD_UNITS = N_DIR * GD_PAIRS


def _gdn_kernel(qf_ref, kf_ref, vf_ref, af_ref, bf_ref,
                qb_ref, kb_ref, vb_ref, ab_ref, bb_ref,
                alr_ref, dtr_ref,
                of_ref, ob_ref,
                s_scr, gc_scr, beta_scr, egc_scr, qsc_scr, fb_scr, eg_scr,
                u_scr, wq_scr, at_scr, ks_scr, *, chunks):
    j = pl.program_id(1)

    @pl.when(j == 0)
    def _():
        s_scr[...] = jnp.zeros_like(s_scr)

    lane_row = lax.broadcasted_iota(jnp.int32, (chunks, PAIR), 1)
    for d, (a_ref, b_ref) in enumerate(((af_ref, bf_ref), (ab_ref, bb_ref))):
        ucum = _pair_cumsum_matrix(rev=bool(d))
        last = 0 if d else CHUNK - 1
        for p in range(GD_PAIRS):
            u = d * GD_PAIRS + p
            g = -jnp.exp(alr_ref[d, p]) * _softplus(a_ref[p] + dtr_ref[d, p])
            gc = _exact_right(g, ucum)
            beta = _sigmoid(b_ref[p])
            g_last = jnp.where(lane_row < HEAD_DIM, gc[:, last:last + 1],
                               gc[:, HEAD_DIM + last:HEAD_DIM + last + 1])
            gc_scr[u] = gc
            beta_scr[u] = beta
            egc_scr[u] = jnp.exp(gc)
            qsc_scr[u] = jnp.exp(gc) * (HEAD_DIM ** -0.5)
            fb_scr[u] = jnp.exp(g_last - gc) * beta
            eg_scr[u] = jnp.exp(g_last)

    r = lax.broadcasted_iota(jnp.int32, (PAIR, PAIR), 0)
    cidx = lax.broadcasted_iota(jnp.int32, (PAIR, PAIR), 1)
    is_diag = r == cidx
    same16 = (r // GD_INV_BLOCK) == (cidx // GD_INV_BLOCK)
    eye_s = jnp.where(lax.broadcasted_iota(jnp.int32, (GD_INV_BLOCK, PAIR), 0)
                      == lax.broadcasted_iota(jnp.int32, (GD_INV_BLOCK, PAIR), 1) % GD_INV_BLOCK, 1.0, 0.0)
    dirs = ((qf_ref, kf_ref, vf_ref, of_ref), (qb_ref, kb_ref, vb_ref, ob_ref))
    units = [(d, p) for d in range(N_DIR) for p in range(GD_PAIRS)]

    def strip_of(bd):
        s = bd[0:GD_INV_BLOCK]
        for b in range(1, PAIR // GD_INV_BLOCK):
            s = s + bd[b * GD_INV_BLOCK:(b + 1) * GD_INV_BLOCK]
        return s

    def blockdiag_of(s):
        return jnp.where(same16, jnp.concatenate([s] * (PAIR // GD_INV_BLOCK), axis=0), jnp.zeros((), s.dtype))

    def precompute(t):
        masks = [_pair_masks(False), _pair_masks(True)]
        items = []
        for cc in range(GD_CHUNKS_PER_TRIP):
            step = t * GD_CHUNKS_PER_TRIP + cc
            for d in range(N_DIR):
                c = (chunks - 1 - step) if d else step
                for p in range(GD_PAIRS):
                    items.append(dict(d=d, p=p, c=c, r0=_aligned(c * CHUNK, CHUNK), row=pl.ds(c, 1),
                                      u=d * GD_PAIRS + p, cols=slice(p * PAIR, (p + 1) * PAIR)))
        for it in items:
            u, row = it["u"], it["row"]
            q_ref, k_ref = dirs[it["d"]][0], dirs[it["d"]][1]
            q_b = _stack_pair(q_ref[pl.ds(it["r0"], CHUNK), it["cols"]]).astype(BF16)
            k_b = _stack_pair(k_ref[pl.ds(it["r0"], CHUNK), it["cols"]]).astype(BF16)
            it["grams"] = _dot_nt(jnp.concatenate([q_b, k_b], axis=0), k_b)
            gc = gc_scr[u, row, :]
            it["osum"] = _dot_tn(*_outer_sum_operands(gc, -gc))
        yield
        for it in items:
            u, row = it["u"], it["row"]
            valid, strict = masks[it["d"]]
            decay = jnp.exp(jnp.where(valid, it.pop("osum"), NEG)) * beta_scr[u, row, :]
            grams = it.pop("grams")
            attn = (grams[:PAIR] * (HEAD_DIM ** -0.5) * decay).astype(BF16)
            q_scale = jnp.where(is_diag, qsc_scr[u, row, :], 0.0).astype(BF16)
            at_scr[u, it["c"]] = jnp.concatenate([attn, q_scale], axis=1)
            neg_a = jnp.where(strict, -(grams[PAIR:] * decay), 0.0)
            nd = jnp.where(same16, neg_a, 0.0)
            it["nl"] = (neg_a - nd).astype(BF16)
            nd_s = strip_of(nd)
            it["t_s"] = eye_s + nd_s
            it["pw_s"] = _dot(nd_s.astype(BF16), nd.astype(BF16))
        yield
        for _ in range(2):
            for it in items:
                pw_s = it["pw_s"].astype(BF16)
                both = _dot(jnp.concatenate([it["t_s"].astype(BF16), pw_s], axis=0), blockdiag_of(pw_s))
                it["t_s"] = it["t_s"] + both[:GD_INV_BLOCK]
                it["pw_s"] = both[GD_INV_BLOCK:]
        yield
        for it in items:
            pw_s = it.pop("pw_s").astype(BF16)
            t_s = it.pop("t_s")
            it["t_inv"] = blockdiag_of(t_s + _dot(t_s.astype(BF16), blockdiag_of(pw_s)))
        yield
        for it in items:
            it["x_b"] = it["t_inv"].astype(BF16)
            it["m"] = _dot(it["x_b"], it.pop("nl"))
        yield
        for it in items:
            m_b = it.pop("m").astype(BF16)
            both = _dot(m_b, jnp.concatenate([m_b, it.pop("x_b")], axis=1))
            it["m2"] = both[:, :PAIR]
            it["t_inv"] = it["t_inv"] + both[:, PAIR:]
        yield
        for it in items:
            it["t_inv"] = it["t_inv"] + _dot(it.pop("m2").astype(BF16), it["t_inv"].astype(BF16))
        yield
        for it in items:
            u, c, row = it["u"], it["c"], it["row"]
            q_ref, k_ref, v_ref = dirs[it["d"]][:3]
            rows = pl.ds(it["r0"], CHUNK)
            q_b = _stack_pair(q_ref[rows, it["cols"]]).astype(BF16)
            k_pair = _stack_pair(k_ref[rows, it["cols"]])
            v_b = _stack_pair(v_ref[rows, it["cols"]]).astype(BF16)
            t_inv = it.pop("t_inv")
            u_scr[u, c] = _dot(t_inv.astype(BF16), v_b)
            w = _dot((t_inv * egc_scr[u, row, :]).astype(BF16), k_pair.astype(BF16))
            wq_scr[u, c] = jnp.concatenate([w.astype(BF16), q_b], axis=0)
            ks_scr[u, c] = (k_pair.T * fb_scr[u, row, :]).astype(BF16)

    def scan(i):
        cs = (i, chunks - 1 - i)
        state = [s_scr[u] for u in range(len(units))]
        ws = [_dot(wq_scr[u, cs[d]], state[u].astype(BF16)) for u, (d, p) in enumerate(units)]
        yield
        v_qs = [jnp.concatenate([u_scr[u, cs[d]] - ws[u][:PAIR], ws[u][PAIR:]], axis=0).astype(BF16)
                for u, (d, p) in enumerate(units)]
        for u, (d, p) in enumerate(units):
            s_scr[u] = state[u] * eg_scr[u, pl.ds(cs[d], 1), :] + _dot(ks_scr[u, cs[d]], v_qs[u][:PAIR])
        for u, (d, p) in enumerate(units):
            o = _dot(at_scr[u, cs[d]], v_qs[u])
            r0 = _aligned(cs[d] * CHUNK, CHUNK)
            dirs[d][3][pl.ds(r0, CHUNK), p * PAIR:(p + 1) * PAIR] = o[:CHUNK] + o[CHUNK:]

    trips = chunks // GD_CHUNKS_PER_TRIP

    def scan_trip(t):
        for cc in range(GD_CHUNKS_PER_TRIP):
            yield from scan(t * GD_CHUNKS_PER_TRIP + cc)
            yield

    def fused(t, carry):
        filler = scan_trip(t - 1)
        for _ in precompute(t):
            next(filler, None)
        for _ in filler:
            pass
        return carry

    for _ in precompute(0):
        pass
    lax.fori_loop(1, trips, fused, 0)
    for _ in scan_trip(trips - 1):
        pass


def _gdn(q, k, v, gate_rows, a_log, dt_bias):
    b, s, _ = q.shape
    tb = min(GDN_SEQ_BLOCK, s)
    nb = s // tb
    chunks = tb // CHUNK
    alr = _param_rows(a_log, GDN_HEADS)
    dtr = _param_rows(dt_bias, GDN_HEADS)

    def fwd(bi, j):
        return (bi, j, 0)

    def bwd(bi, j):
        return (bi, nb - 1 - j, 0)

    seq = lambda w, im: pl.BlockSpec((None, tb, w), im)
    rows = lambda lane0, d: _gate_row_spec(lane0, GDN_HEADS, d, chunks, (lambda j: nb - 1 - j) if d else (lambda j: j))
    full = lambda a: pl.BlockSpec(a.shape, lambda bi, j: (0,) * a.ndim)
    out = jax.ShapeDtypeStruct((b, s, GDN_WIDTH), F32)
    rows_scr = pltpu.VMEM((GD_UNITS, chunks, PAIR), F32)
    return pl.pallas_call(
        functools.partial(_gdn_kernel, chunks=chunks),
        out_shape=(out, out),
        grid=(b, nb),
        in_specs=[seq(GDN_WIDTH, fwd), seq(GDN_WIDTH, fwd), seq(GDN_WIDTH, fwd), rows(GATE_GD_A, 0), rows(GATE_GD_B, 0),
                  seq(GDN_WIDTH, bwd), seq(GDN_WIDTH, bwd), seq(GDN_WIDTH, bwd), rows(GATE_GD_A, 1), rows(GATE_GD_B, 1),
                  full(alr), full(dtr)],
        out_specs=(seq(GDN_WIDTH, fwd), seq(GDN_WIDTH, bwd)),
        scratch_shapes=[pltpu.VMEM((GD_UNITS, PAIR, PAIR), F32),
                        rows_scr, rows_scr, rows_scr, rows_scr, rows_scr, rows_scr,
                        pltpu.VMEM((GD_UNITS, chunks, PAIR, PAIR), F32),
                        pltpu.VMEM((GD_UNITS, chunks, 2 * PAIR, PAIR), BF16),
                        pltpu.VMEM((GD_UNITS, chunks, PAIR, 2 * PAIR), BF16),
                        pltpu.VMEM((GD_UNITS, chunks, PAIR, PAIR), BF16)],
        compiler_params=_params("parallel", "arbitrary"),
        name="gated_deltanet",
    )(q, k, v, gate_rows, gate_rows, q, k, v, gate_rows, gate_rows, alr, dtr)


def _out_proj_kernel(x_ref, na_ref, hf_ref, hb_ref, mo_ref, mw_ref, of_ref, ob_ref, gz_ref, gw_ref,
                     w_ref, o_ref):
    hs = hf_ref[...] + hb_ref[...]
    y_ml = hs * lax.rsqrt(_head_sumsq(hs) * (1.0 / HEAD_DIM) + EPS) * mw_ref[...] * _sigmoid(mo_ref[...])
    os_ = of_ref[...] + ob_ref[...]
    z = gz_ref[...]
    y_gd = os_ * lax.rsqrt(_head_sumsq(os_) * (1.0 / HEAD_DIM) + EPS) * gw_ref[...] * (z * _sigmoid(z))
    acc = x_ref[...] + _dot(na_ref[...], w_ref[0:NA_WIDTH, :])
    acc = acc + _dot(y_ml.astype(BF16), w_ref[NA_WIDTH:NA_WIDTH + ML_WIDTH, :])
    acc = acc + _dot(y_gd.astype(BF16), w_ref[NA_WIDTH + ML_WIDTH:, :])
    o_ref[...] = acc


def _out_proj(x2d, y_na, hf, hb, ml_o, ml_norm_w, of, ob, gd_z, gdn_norm_w, w_out):
    t = x2d.shape[0]
    tm = min(TM_ROWS, t)
    row = lambda w: pl.BlockSpec((tm, w), lambda i: (i, 0))
    const = lambda r, c: pl.BlockSpec((r, c), lambda i: (0, 0))
    return pl.pallas_call(
        _out_proj_kernel,
        out_shape=jax.ShapeDtypeStruct((t, D_MODEL), F32),
        grid=(t // tm,),
        in_specs=[row(D_MODEL), row(NA_WIDTH), row(ML_WIDTH), row(ML_WIDTH), row(ML_WIDTH), const(1, ML_WIDTH),
                  row(GDN_WIDTH), row(GDN_WIDTH), row(GDN_WIDTH), const(1, GDN_WIDTH),
                  const(D_MODEL, D_MODEL)],
        out_specs=row(D_MODEL),
        compiler_params=_params("parallel"),
        name="out_proj",
    )(x2d, y_na, hf, hb, ml_o, ml_norm_w.reshape(1, ML_WIDTH).astype(F32), of, ob, gd_z,
      gdn_norm_w.reshape(1, GDN_WIDTH).astype(F32), w_out.astype(BF16))


def _mem_kv_kernel(m_ref, nw_ref, w_ref, k_ref, v_ref):
    h = _rms(m_ref[...], nw_ref[...]).astype(BF16)
    k_ref[...] = _dot(h, w_ref[:, :D_MODEL]).astype(BF16)
    v_ref[...] = _dot(h, w_ref[:, D_MODEL:]).astype(BF16)


def _mem_kv(mem2d, norm_w, w_kv):
    t = mem2d.shape[0]
    tm = min(TM_ROWS, t)
    out = jax.ShapeDtypeStruct((t, D_MODEL), BF16)
    return pl.pallas_call(
        _mem_kv_kernel,
        out_shape=(out, out),
        grid=(t // tm,),
        in_specs=[pl.BlockSpec((tm, D_MODEL), lambda i: (i, 0)),
                  pl.BlockSpec((1, D_MODEL), lambda i: (0, 0)),
                  pl.BlockSpec((D_MODEL, 2 * D_MODEL), lambda i: (0, 0))],
        out_specs=(pl.BlockSpec((tm, D_MODEL), lambda i: (i, 0)), pl.BlockSpec((tm, D_MODEL), lambda i: (i, 0))),
        compiler_params=_params("parallel"),
        name="mem_kv",
    )(mem2d, norm_w.reshape(1, D_MODEL).astype(F32), w_kv.astype(BF16))


def _xattn_kernel(x_ref, nw_ref, wq_ref, k_ref, v_ref, wo_ref, o_ref):
    x = x_ref[...]
    q = _dot(_rms(x, nw_ref[...]).astype(BF16), wq_ref[...]).astype(BF16)
    heads = [slice(h * XA_HEAD_DIM, (h + 1) * XA_HEAD_DIM) for h in range(XA_HEADS)]
    s = [_dot_nt(q[:, cols], k_ref[:, cols]) * (XA_HEAD_DIM ** -0.5) for cols in heads]
    e = [jnp.exp(s_h - jnp.max(s_h, axis=-1, keepdims=True)) for s_h in s]
    inv_l = [1.0 / jnp.sum(e_h, axis=-1, keepdims=True) for e_h in e]
    o = [(_dot(e_h.astype(BF16), v_ref[:, cols]) * il).astype(BF16) for e_h, il, cols in zip(e, inv_l, heads)]
    o_ref[...] = x + _dot(jnp.concatenate(o, axis=1), wo_ref[...])


def _xattn(x3d, norm_w, w_q, k, v, w_o):
    b, s, _ = x3d.shape
    tm = min(TM_ROWS, s)
    n_mem = k.shape[1]
    return pl.pallas_call(
        _xattn_kernel,
        out_shape=jax.ShapeDtypeStruct(x3d.shape, F32),
        grid=(b, s // tm),
        in_specs=[pl.BlockSpec((None, tm, D_MODEL), lambda bi, i: (bi, i, 0)),
                  pl.BlockSpec((1, D_MODEL), lambda bi, i: (0, 0)),
                  pl.BlockSpec((D_MODEL, D_MODEL), lambda bi, i: (0, 0)),
                  pl.BlockSpec((None, n_mem, D_MODEL), lambda bi, i: (bi, 0, 0)),
                  pl.BlockSpec((None, n_mem, D_MODEL), lambda bi, i: (bi, 0, 0)),
                  pl.BlockSpec((D_MODEL, D_MODEL), lambda bi, i: (0, 0))],
        out_specs=pl.BlockSpec((None, tm, D_MODEL), lambda bi, i: (bi, i, 0)),
        compiler_params=_params("parallel", "parallel"),
        name="cross_attention",
    )(x3d, norm_w.reshape(1, D_MODEL).astype(F32), w_q.astype(BF16), k, v, w_o.astype(BF16))


FF_CHUNK = 512


def _ffn_kernel(x_ref, nw_ref, w1_ref, w2_ref, fw_ref, o_ref, *, final_norm):
    x = x_ref[...]
    h = _rms(x, nw_ref[...]).astype(BF16)
    acc = x
    for c0 in range(0, D_FF, FF_CHUNK):
        a = jnp.maximum(_dot(h, w1_ref[:, c0:c0 + FF_CHUNK]), 0.0)
        acc = acc + _dot((a * a).astype(BF16), w2_ref[c0:c0 + FF_CHUNK, :])
    o_ref[...] = _rms(acc, fw_ref[...]) if final_norm else acc


def _ffn(x2d, norm_w, w1, w2, final_w, final_norm):
    t = x2d.shape[0]
    tm = min(TM_ROWS, t)
    return pl.pallas_call(
        functools.partial(_ffn_kernel, final_norm=final_norm),
        out_shape=jax.ShapeDtypeStruct((t, D_MODEL), F32),
        grid=(t // tm,),
        in_specs=[pl.BlockSpec((tm, D_MODEL), lambda i: (i, 0)),
                  pl.BlockSpec((1, D_MODEL), lambda i: (0, 0)),
                  pl.BlockSpec((D_MODEL, D_FF), lambda i: (0, 0), pipeline_mode=pl.Buffered(1)),
                  pl.BlockSpec((D_FF, D_MODEL), lambda i: (0, 0), pipeline_mode=pl.Buffered(1)),
                  pl.BlockSpec((1, D_MODEL), lambda i: (0, 0))],
        out_specs=pl.BlockSpec((tm, D_MODEL), lambda i: (i, 0)),
        compiler_params=_params("parallel"),
        name="ffn",
    )(x2d, norm_w.reshape(1, D_MODEL).astype(F32), w1.astype(BF16), w2.astype(BF16),
      final_w.reshape(1, D_MODEL).astype(F32))


def kernel(x, mem, norm_mix_w, w_in, na_rel_bias, ml_i_bias, ml_f_bias, ml_norm_w, gdn_conv_w, gdn_a_log,
           gdn_dt_bias, gdn_norm_w, w_out, norm_xa_w, norm_mem_w, w_xq, w_xkv, w_xo, norm_ffn_w, w_ff1,
           w_ff2, norm_out_w):
    b, s, d = x.shape
    depth = w_in.shape[0]
    x2d = x.reshape(b * s, d).astype(F32)
    mem2d = mem.reshape(-1, d).astype(F32)
    w_in_p = _permute_w_in(w_in)
    w_out_b, w_xq_b, w_xkv_b, w_xo_b, w_ff1_b, w_ff2_b = (
        w.astype(BF16) for w in (w_out, w_xq, w_xkv, w_xo, w_ff1, w_ff2))
    seq = lambda a: a.reshape(b, s, a.shape[-1])
    flat = lambda a: a.reshape(b * s, a.shape[-1])
    n_mem = mem.shape[1]
    for l in range(depth):
        (gq, gk, gv, na_q, na_k, na_v, ml_q, ml_k, ml_v, ml_o, gd_z, gates) = _in_proj(
            x2d, norm_mix_w[l].astype(F32), w_in_p[l], gdn_conv_w[l], s)
        gate_rows = _gate_rows(seq(gates))
        y_na = _neighbourhood_attention(seq(na_q), seq(na_k), seq(na_v), _na_bias_table(na_rel_bias[l]))
        hf, hb = _mlstm(seq(ml_q), seq(ml_k), seq(ml_v), gate_rows, ml_i_bias[l], ml_f_bias[l])
        of, ob = _gdn(seq(gq), seq(gk), seq(gv), gate_rows, gdn_a_log[l], gdn_dt_bias[l])
        x2d = _out_proj(x2d, flat(y_na), flat(hf), flat(hb), ml_o, ml_norm_w[l], flat(of), flat(ob), gd_z,
                        gdn_norm_w[l], w_out_b[l])
        mk, mv = _mem_kv(mem2d, norm_mem_w[l], w_xkv_b[l])
        x2d = _xattn(x2d.reshape(b, s, d), norm_xa_w[l], w_xq_b[l], mk.reshape(b, n_mem, d),
                     mv.reshape(b, n_mem, d), w_xo_b[l]).reshape(b * s, d)
        x2d = _ffn(x2d, norm_ffn_w[l], w_ff1_b[l], w_ff2_b[l], norm_out_w, final_norm=(l == depth - 1))
    return x2d.reshape(b, s, d).astype(x.dtype)
```

```python
import functools

import numpy as np
import jax
import jax.numpy as jnp
from jax import lax
from jax.experimental import pallas as pl
from jax.experimental.pallas import tpu as pltpu

F32 = jnp.float32
BF16 = jnp.bfloat16

D_MODEL = 1024
HEAD_DIM = 64
GRID_W = 64
NA_HEADS = 6
NA_WIN_ROWS = 8
NA_WIN_COLS = 16
ML_HEADS = 4
GDN_HEADS = 6
CONV_K = 5
N_DIR = 2
XA_HEADS = 4
XA_HEAD_DIM = D_MODEL // XA_HEADS
D_FF = 4 * D_MODEL
NA_WIDTH = NA_HEADS * HEAD_DIM
ML_WIDTH = ML_HEADS * HEAD_DIM
GDN_WIDTH = GDN_HEADS * HEAD_DIM
EPS = 1e-6
CHUNK = 64
PAIR = 2 * HEAD_DIM
NEG = -1e30

IN_SIZES = (NA_WIDTH, NA_WIDTH, NA_WIDTH,
            ML_WIDTH, ML_WIDTH, ML_WIDTH, ML_WIDTH, N_DIR * ML_HEADS, N_DIR * ML_HEADS,
            GDN_WIDTH, GDN_WIDTH, GDN_WIDTH, GDN_WIDTH, N_DIR * GDN_HEADS, N_DIR * GDN_HEADS)

GATE_GD_B = 0
GATE_GD_A = GATE_GD_B + N_DIR * GDN_HEADS
GATE_ML_I = GATE_GD_A + N_DIR * GDN_HEADS
GATE_ML_F = GATE_ML_I + N_DIR * ML_HEADS
GATE_USED = GATE_ML_F + N_DIR * ML_HEADS
GATE_LANES = 128

V7X_VMEM_LIMIT = 56 * 1024 * 1024
MXU_COLS = 256

TM_ROWS = 1024
NA_ROWS_PER_STEP = 8
SEQ_BLOCK = 1024
GDN_SEQ_BLOCK = 1024


def _params(*sem):
    return pltpu.CompilerParams(dimension_semantics=sem, vmem_limit_bytes=V7X_VMEM_LIMIT)


def _dot(a, b):
    return jnp.dot(a, b, preferred_element_type=F32)


def _dot_nt(a, b):
    return lax.dot_general(a, b, (((1,), (1,)), ((), ())), preferred_element_type=F32)


def _dot_tn(a, b):
    return lax.dot_general(a, b, (((0,), (0,)), ((), ())), preferred_element_type=F32)


def _split3(x):
    x1 = x.astype(BF16)
    r1 = x - x1.astype(F32)
    x2 = r1.astype(BF16)
    x3 = (r1 - x2.astype(F32)).astype(BF16)
    return x1, x2, x3


def _sum_right(x, m01):
    return _dot(x.astype(BF16), m01)


def _exact_right(x, m01):
    x1, x2, x3 = _split3(x)
    return _dot(x1, m01) + _dot(x2, m01) + _dot(x3, m01)


def _rms(x, w):
    ms = jnp.mean(x * x, axis=-1, keepdims=True)
    return x * lax.rsqrt(ms + EPS) * w


def _softplus(x):
    return jnp.maximum(x, 0.0) + jnp.log1p(jnp.exp(-jnp.abs(x)))


def _sigmoid(x):
    return 1.0 / (1.0 + jnp.exp(-x))


def _segment_mean_matrix(width):
    r = lax.broadcasted_iota(jnp.int32, (width, width), 0) // HEAD_DIM
    c = lax.broadcasted_iota(jnp.int32, (width, width), 1) // HEAD_DIM
    return jnp.where(r == c, 1.0, 0.0).astype(BF16)


def _head_sumsq(t):
    return _sum_right(t * t, _segment_mean_matrix(t.shape[-1]))


def _aligned(x, m):
    return x if isinstance(x, int) else pl.multiple_of(x, m)


def _pair_masks(rev):
    r = lax.broadcasted_iota(jnp.int32, (PAIR, PAIR), 0)
    c = lax.broadcasted_iota(jnp.int32, (PAIR, PAIR), 1)
    same = (r // HEAD_DIM) == (c // HEAD_DIM)
    t, s = r % HEAD_DIM, c % HEAD_DIM
    if rev:
        return same & (s >= t), same & (s > t)
    return same & (s <= t), same & (s < t)


def _stack_pair(x2):
    lane = lax.broadcasted_iota(jnp.int32, x2.shape, 1)
    zero = jnp.zeros_like(x2)
    return jnp.concatenate([jnp.where(lane < HEAD_DIM, x2, zero),
                            jnp.where(lane >= HEAD_DIM, x2, zero)], axis=0)


GD_PAIRS = GDN_HEADS // 2
CONV_HALO = 8
PREP_ROWS = 128
IN_SEGMENTS = (
    (NA_WIDTH, BF16), (NA_WIDTH, BF16), (NA_WIDTH, BF16),
    (ML_WIDTH, BF16), (ML_WIDTH, BF16), (ML_WIDTH, BF16), (ML_WIDTH, F32),
    (GDN_WIDTH, F32), (GATE_LANES, F32))
CONV_COLS = 3 * GDN_WIDTH
IN_COLS = CONV_COLS + sum(w for w, _ in IN_SEGMENTS)
IN_DOT_COLS = 3 * MXU_COLS


def _in_proj_kernel(x_ref, prev_ref, next_ref, nw_ref, w_ref, cw_ref, gq_ref, gk_ref, gv_ref, *rest,
                    tm, seq_len):
    out_refs, ext_scr = rest[:-1], rest[-1]
    i = pl.program_id(0)
    nw = nw_ref[...]
    h = _rms(x_ref[...], nw).astype(BF16)
    blocks_per_seq = seq_len // tm
    has_prev = (i % blocks_per_seq) > 0
    has_next = (i % blocks_per_seq) < blocks_per_seq - 1
    halo = _rms(jnp.concatenate([prev_ref[...], next_ref[...]], axis=0), nw).astype(BF16)
    w_conv = w_ref[:, 0:CONV_COLS]
    halo_p = _dot(halo, w_conv)
    ext_scr[0:CONV_HALO, :] = jnp.where(has_prev, halo_p[:CONV_HALO], 0.0)
    ext_scr[CONV_HALO + tm:, :] = jnp.where(has_next, halo_p[CONV_HALO:], 0.0)
    ext_scr[CONV_HALO:CONV_HALO + tm, :] = _dot(h, w_conv)

    base = CONV_HALO - CONV_K // 2
    seg = _segment_mean_matrix(PAIR)
    taps = [cw_ref[t:t + 1, :] for t in range(CONV_K)]

    def conv_tiles():
        for rb in range(tm // PREP_ROWS):
            r0 = rb * PREP_ROWS
            for ct in range(CONV_COLS // PAIR):
                cols = slice(ct * PAIR, (ct + 1) * PAIR)
                y = ext_scr[r0 + base:r0 + base + PREP_ROWS, cols] * taps[0][:, cols]
                for t in range(1, CONV_K):
                    y = y + ext_scr[r0 + base + t:r0 + base + t + PREP_ROWS, cols] * taps[t][:, cols]
                y = y * _sigmoid(y)
                which, off = divmod(ct * PAIR, GDN_WIDTH)
                if which < 2:
                    y = y * lax.rsqrt(_sum_right(y * y, seg) + EPS)
                (gq_ref, gk_ref, gv_ref)[which][r0:r0 + PREP_ROWS, off:off + PAIR] = y
                yield

    tiles = conv_tiles()
    n_tiles = (tm // PREP_ROWS) * (CONV_COLS // PAIR)
    starts = CONV_COLS + np.cumsum([0] + [w for w, _ in IN_SEGMENTS])
    chunk_starts = list(range(CONV_COLS, IN_COLS, IN_DOT_COLS))
    for k, c0 in enumerate(chunk_starts):
        c1 = min(c0 + IN_DOT_COLS, IN_COLS)
        acc = _dot(h, w_ref[:, c0:c1])
        for o_ref, s0, s1 in zip(out_refs, starts[:-1], starts[1:]):
            a, b = max(c0, int(s0)), min(c1, int(s1))
            if a < b:
                o_ref[:, a - int(s0):b - int(s0)] = acc[:, a - c0:b - c0].astype(o_ref.dtype)
        for _ in range(-(-n_tiles // len(chunk_starts))):
            next(tiles, None)
    for _ in tiles:
        pass


def _in_proj(x2d, norm_w, w_perm, conv_w, seq_len):
    t = x2d.shape[0]
    tm = min(TM_ROWS, seq_len)
    hb = tm // CONV_HALO
    last = t // CONV_HALO - 1
    gd = jax.ShapeDtypeStruct((t, GDN_WIDTH), F32)
    outs = (gd, gd, gd) + tuple(jax.ShapeDtypeStruct((t, w), dt) for w, dt in IN_SEGMENTS)
    row = lambda w: pl.BlockSpec((tm, w), lambda i: (i, 0))
    return pl.pallas_call(
        functools.partial(_in_proj_kernel, tm=tm, seq_len=seq_len),
        out_shape=outs,
        grid=(t // tm,),
        in_specs=[row(D_MODEL),
                  pl.BlockSpec((CONV_HALO, D_MODEL), lambda i: (jnp.maximum(i * hb - 1, 0), 0)),
                  pl.BlockSpec((CONV_HALO, D_MODEL), lambda i: (jnp.minimum((i + 1) * hb, last), 0)),
                  pl.BlockSpec((1, D_MODEL), lambda i: (0, 0)),
                  pl.BlockSpec((D_MODEL, IN_COLS), lambda i: (0, 0), pipeline_mode=pl.Buffered(1)),
                  pl.BlockSpec((CONV_K, CONV_COLS), lambda i: (0, 0))],
        out_specs=(row(GDN_WIDTH), row(GDN_WIDTH), row(GDN_WIDTH)) + tuple(row(w) for w, _ in IN_SEGMENTS),
        scratch_shapes=[pltpu.VMEM((tm + 2 * CONV_HALO, CONV_COLS), F32)],
        compiler_params=_params("parallel"),
        name="in_proj",
    )(x2d, x2d, x2d, norm_w.reshape(1, D_MODEL), w_perm, conv_w.astype(F32))


def _permute_w_in(w):
    parts = jnp.split(w, np.cumsum(IN_SIZES)[:-1], axis=-1)
    (na_q, na_k, na_v, ml_q, ml_k, ml_v, ml_o, ml_i, ml_f, gd_q, gd_k, gd_v, gd_z, gd_b, gd_a) = parts
    pad = jnp.zeros(w.shape[:-1] + (GATE_LANES - GATE_USED,), w.dtype)
    return jnp.concatenate([gd_q, gd_k, gd_v, na_q, na_k, na_v, ml_q, ml_k, ml_v, ml_o, gd_z,
                            gd_b, gd_a, ml_i, ml_f, pad], axis=-1).astype(BF16)


NA_BIAS_ROWS = 96


def _na_bias_kernel(rb_ref, o_ref):
    n = GRID_W * GRID_W
    dc = lax.broadcasted_iota(jnp.int32, (GATE_LANES, n), 0)
    col = lax.broadcasted_iota(jnp.int32, (GATE_LANES, n), 1)
    q, kc = col // GRID_W, col % GRID_W
    c0 = jnp.clip(q - NA_WIN_COLS // 2, 0, GRID_W - NA_WIN_COLS)
    valid = (kc >= c0) & (kc < c0 + NA_WIN_COLS)
    onehot = jnp.where(valid & (kc - q + (NA_WIN_COLS - 1) == dc), 1.0, 0.0).astype(BF16)
    o_ref[...] = jnp.where(valid[0:1, :], _exact_right(rb_ref[...], onehot), NEG)


def _na_bias_table(rel_bias):
    nh, ndr, ndc = rel_bias.shape
    rb = jnp.zeros((NA_BIAS_ROWS, GATE_LANES), F32).at[:nh * ndr, :ndc].set(
        rel_bias.astype(F32).reshape(nh * ndr, ndc))
    band = pl.pallas_call(
        _na_bias_kernel,
        out_shape=jax.ShapeDtypeStruct((NA_BIAS_ROWS, GRID_W * GRID_W), F32),
        name="na_bias_expand",
    )(rb)
    band = band[:nh * ndr].reshape(nh, ndr, GRID_W, GRID_W)
    first = np.clip(np.arange(ndr + 1) - 1, 0, ndr - 1)
    second = np.clip(np.arange(ndr + 1), 0, ndr - 1)
    tab = jnp.concatenate([band[:, first], band[:, second]], axis=-1)
    tab = tab.reshape(NA_HEADS // 2, 2, ndr + 1, GRID_W, PAIR)
    return jnp.moveaxis(tab, 1, 2).reshape(NA_HEADS // 2, ndr + 1, 2 * GRID_W, PAIR)


NA_ITEM_ROWS = 4
NA_KEY_ROWS = NA_ITEM_ROWS + NA_WIN_ROWS


NA_INNER_BLOCKS = NA_KEY_ROWS // 2 - 1


def _na_inner_bias(bias_tab):
    half = NA_WIN_ROWS // 2
    rows = []
    for i in range(NA_ITEM_ROWS):
        blocks = []
        for jp in range(i // 2, i // 2 + NA_INNER_BLOCKS):
            ok_a, ok_b = 0 <= 2 * jp - i < NA_WIN_ROWS, 0 <= 2 * jp + 1 - i < NA_WIN_ROWS
            entry = min(max(2 * jp - i + half, 0), 2 * NA_WIN_ROWS - 1)
            pen = np.concatenate([np.full(GRID_W, 0.0 if ok_a else NEG), np.full(GRID_W, 0.0 if ok_b else NEG)])
            blocks.append(bias_tab[:, entry] + jnp.asarray(pen, F32))
        rows.append(jnp.concatenate(blocks, axis=-1))
    return jnp.stack(rows, axis=1)


def _na_kernel(q_ref, k_ref, v_ref, bias_ref, inner_ref, o_ref, *, rows, rows_per_step):
    j = pl.program_id(1)
    nkeys = NA_KEY_ROWS * GRID_W
    half = NA_WIN_ROWS // 2
    lane = lax.broadcasted_iota(jnp.int32, (GRID_W, PAIR), 1)
    lane_blk = lax.broadcasted_iota(jnp.int32, (PAIR, PAIR), 1)

    def run(inner):
        items = []
        for g in range(rows_per_step // NA_ITEM_ROWS):
            r_first = j * rows_per_step + g * NA_ITEM_ROWS
            kr0 = jnp.clip(r_first - half, 0, rows - NA_KEY_ROWS)
            koff = pl.multiple_of(kr0 * GRID_W, GRID_W)
            entry, pen = {}, {}
            if not inner:
                for i in range(NA_ITEM_ROWS):
                    r = r_first + i
                    r0 = jnp.clip(r - half, 0, rows - NA_WIN_ROWS)
                    for jp in range(NA_KEY_ROWS // 2):
                        key_a = kr0 + 2 * jp
                        ok_a = (key_a >= r0) & (key_a < r0 + NA_WIN_ROWS)
                        ok_b = (key_a + 1 >= r0) & (key_a + 1 < r0 + NA_WIN_ROWS)
                        entry[i, jp] = jnp.clip(key_a - r + (NA_WIN_ROWS - 1), -1, 2 * NA_WIN_ROWS - 2) + 1
                        pen[i, jp] = jnp.where(lane_blk < GRID_W, jnp.where(ok_a, 0.0, NEG),
                                               jnp.where(ok_b, 0.0, NEG))
            for p in range(NA_HEADS // 2):
                items.append(dict(g=g, p=p, koff=koff, entry=entry, pen=pen, cols=slice(p * PAIR, (p + 1) * PAIR)))
        for it in items:
            q0 = it["g"] * NA_ITEM_ROWS * GRID_W
            q_lhs = jnp.concatenate(
                [_stack_pair(q_ref[q0 + i * GRID_W:q0 + (i + 1) * GRID_W, it["cols"]])
                 for i in range(NA_ITEM_ROWS)], axis=0)
            q_lhs = (q_lhs.astype(F32) * (HEAD_DIM ** -0.5)).astype(BF16)
            it["s"] = _dot_nt(q_lhs, k_ref[pl.ds(it["koff"], nkeys), it["cols"]])
        for it in items:
            s = it.pop("s")
            blocks = []
            for i in range(NA_ITEM_ROWS):
                if inner:
                    b0 = (i // 2) * PAIR
                    blocks.append(s[i * PAIR:(i + 1) * PAIR, b0:b0 + NA_INNER_BLOCKS * PAIR] + inner_ref[it["p"], i])
                else:
                    blocks.append(jnp.concatenate(
                        [s[i * PAIR:(i + 1) * PAIR, jp * PAIR:(jp + 1) * PAIR]
                         + bias_ref[it["p"], it["entry"][i, jp]] + it["pen"][i, jp]
                         for jp in range(NA_KEY_ROWS // 2)], axis=1))
            s = jnp.concatenate(blocks, axis=0)
            e = jnp.exp(s - jnp.max(s, axis=-1, keepdims=True))
            it["l"] = jnp.sum(e, axis=-1, keepdims=True)
            e = e.astype(BF16)
            if inner:
                zero = jnp.zeros((PAIR, PAIR), BF16)
                e = jnp.concatenate(
                    [jnp.concatenate(([zero] if i // 2 else []) + [e[i * PAIR:(i + 1) * PAIR]]
                                     + ([] if i // 2 else [zero]), axis=1) for i in range(NA_ITEM_ROWS)], axis=0)
            it["o"] = _dot(e, v_ref[pl.ds(it["koff"], nkeys), it["cols"]])
        for it in items:
            o = it.pop("o") * (1.0 / it.pop("l"))
            q0 = it["g"] * NA_ITEM_ROWS * GRID_W
            for i in range(NA_ITEM_ROWS):
                blk = o[i * PAIR:(i + 1) * PAIR]
                o_ref[q0 + i * GRID_W:q0 + (i + 1) * GRID_W, it["cols"]] = jnp.where(
                    lane < HEAD_DIM, blk[:GRID_W], blk[GRID_W:]).astype(o_ref.dtype)

    first_row = j * rows_per_step
    inner = (first_row >= half) & (first_row + rows_per_step - NA_ITEM_ROWS <= rows - NA_WIN_ROWS)

    @pl.when(inner)
    def _():
        run(True)

    @pl.when(jnp.logical_not(inner))
    def _():
        run(False)


def _neighbourhood_attention(q, k, v, bias_tab):
    b, s, _ = q.shape
    rows = s // GRID_W
    assert rows >= NA_KEY_ROWS
    rps = min(NA_ROWS_PER_STEP, rows)
    tq = rps * GRID_W
    inner_tab = _na_inner_bias(bias_tab)
    return pl.pallas_call(
        functools.partial(_na_kernel, rows=rows, rows_per_step=rps),
        out_shape=jax.ShapeDtypeStruct((b, s, NA_WIDTH), BF16),
        grid=(b, rows // rps),
        in_specs=[pl.BlockSpec((None, tq, NA_WIDTH), lambda bi, j: (bi, j, 0)),
                  pl.BlockSpec((None, s, NA_WIDTH), lambda bi, j: (bi, 0, 0)),
                  pl.BlockSpec((None, s, NA_WIDTH), lambda bi, j: (bi, 0, 0)),
                  pl.BlockSpec(bias_tab.shape, lambda bi, j: (0, 0, 0, 0)),
                  pl.BlockSpec(inner_tab.shape, lambda bi, j: (0, 0, 0, 0))],
        out_specs=pl.BlockSpec((None, tq, NA_WIDTH), lambda bi, j: (bi, j, 0)),
        compiler_params=_params("parallel", "arbitrary"),
        name="neighbourhood_attention",
    )(q, k, v, bias_tab, inner_tab)


def _gate_rows(gates):
    b, s, _ = gates.shape
    g = gates[:, :, :GATE_USED].reshape(b, s // CHUNK, CHUNK, GATE_USED // 2, 2)
    return jnp.transpose(g, (0, 3, 1, 4, 2)).reshape(b, GATE_USED // 2, s // CHUNK, PAIR)


def _gate_row_spec(lane0, heads, d, chunks, chunk_block):
    pairs = heads // 2
    unit_block = (lane0 // 2 + d * pairs) // pairs
    assert unit_block * pairs == lane0 // 2 + d * pairs
    return pl.BlockSpec((None, pairs, chunks, PAIR), lambda bi, j: (bi, unit_block, chunk_block(j), 0))


def _param_rows(p, heads):
    return jnp.repeat(p.astype(F32).reshape(N_DIR, heads // 2, 2), HEAD_DIM, axis=-1).reshape(
        N_DIR, heads // 2, 1, PAIR)


def _pair_cumsum_matrix(rev):
    r = lax.broadcasted_iota(jnp.int32, (PAIR, PAIR), 0)
    c = lax.broadcasted_iota(jnp.int32, (PAIR, PAIR), 1)
    same = (r // HEAD_DIM) == (c // HEAD_DIM)
    sp, s = r % HEAD_DIM, c % HEAD_DIM
    return jnp.where(same & ((sp >= s) if rev else (sp <= s)), 1.0, 0.0).astype(BF16)


ML_PAIRS = ML_HEADS // 2
ML_UNITS = N_DIR * ML_PAIRS
ML_CHUNKS_PER_TRIP = 4
ML_STEPS_PER_TRIP = 4
OS_ROWS = 16


def _outer_sum_operands(col_term, row_term):
    ones = jnp.ones((3, PAIR), BF16)
    zeros = jnp.zeros((OS_ROWS - 6, PAIR), BF16)
    a = jnp.concatenate(list(_split3(col_term)) + [ones, zeros], axis=0)
    b = jnp.concatenate([ones] + list(_split3(row_term)) + [zeros], axis=0)
    return a, b


def _mlstm_kernel(qf_ref, kf_ref, vf_ref, if_ref, ff_ref,
                  qb_ref, kb_ref, vb_ref, ib_ref, fb_ref,
                  ibr_ref, fbr_ref,
                  hf_ref, hb_ref,
                  c_scr, m_scr, cs_scr, rt_scr, ai_scr, em_scr, we_scr, a_scr, g_scr, vt_scr, ut_scr, *, chunks):
    j = pl.program_id(1)

    @pl.when(j == 0)
    def _():
        c_scr[...] = jnp.zeros_like(c_scr)
        m_scr[...] = jnp.zeros_like(m_scr)

    lane = lax.broadcasted_iota(jnp.int32, (chunks, PAIR), 1)
    pos = lane % HEAD_DIM
    lo = lane < HEAD_DIM
    for d, (i_ref, f_ref) in enumerate(((if_ref, ff_ref), (ib_ref, fb_ref))):
        rev = bool(d)
        ucum = _pair_cumsum_matrix(rev)
        last = 0 if rev else CHUNK - 1
        for p in range(ML_PAIRS):
            u = d * ML_PAIRS + p
            logf = -_softplus(-(f_ref[p] + fbr_ref[d, p]))
            bcum = _exact_right(logf, ucum)
            cs = i_ref[p] + ibr_ref[d, p] - bcum
            bl = jnp.where(lo, bcum[:, last:last + 1], bcum[:, HEAD_DIM + last:HEAD_DIM + last + 1])
            cm = cs
            for k in (1, 2, 4, 8, 16, 32):
                if rev:
                    cm = jnp.where(pos < HEAD_DIM - k, jnp.maximum(cm, pltpu.roll(cm, PAIR - k, axis=1)), cm)
                else:
                    cm = jnp.where(pos >= k, jnp.maximum(cm, pltpu.roll(cm, k, axis=1)), cm)
            w = bl + cs
            m_loc = jnp.where(lo, jnp.max(jnp.where(lo, w, NEG), axis=-1, keepdims=True),
                              jnp.max(jnp.where(lo, NEG, w), axis=-1, keepdims=True))
            m = m_scr[u]
            m_prev, a_rows, g_rows = [None] * chunks, [None] * chunks, [None] * chunks
            for c in (range(chunks - 1, -1, -1) if rev else range(chunks)):
                m_prev[c] = m
                m_new = jnp.maximum(bl[c:c + 1] + m, m_loc[c:c + 1])
                a_rows[c] = jnp.exp(bl[c:c + 1] + m - m_new)
                g_rows[c] = jnp.exp(m_loc[c:c + 1] - m_new)
                m = m_new
            m_scr[u] = m
            inter = bcum + jnp.concatenate(m_prev, axis=0)
            m_t = jnp.maximum(bcum + cm, inter)
            cs_scr[u] = cs
            rt_scr[u] = bcum - m_t
            ai_scr[u] = jnp.exp(inter - m_t)
            em_scr[u] = jnp.exp(-m_t)
            we_scr[u] = jnp.exp(w - m_loc)
            a_scr[u] = jnp.concatenate(a_rows, axis=0)
            g_scr[u] = jnp.concatenate(g_rows, axis=0)

    dirs = ((qf_ref, kf_ref, vf_ref, hf_ref), (qb_ref, kb_ref, vb_ref, hb_ref))
    units = [(d, p) for d in range(N_DIR) for p in range(ML_PAIRS)]
    lane64 = lax.broadcasted_iota(jnp.int32, (CHUNK, PAIR), 1)
    one_hi = jnp.where(lane64 == HEAD_DIM, 1.0, 0.0).astype(BF16)
    one_lo = jnp.where(lane64 == 0, 1.0, 0.0).astype(BF16)
    lane_row = lax.broadcasted_iota(jnp.int32, (1, PAIR), 1)

    def key_tile(d, p, r0):
        k_pair = _stack_pair(dirs[d][1][pl.ds(r0, CHUNK), p * PAIR:(p + 1) * PAIR])
        return (k_pair.astype(F32) * (HEAD_DIM ** -0.5)).astype(BF16)

    def contributions(t, carry):
        for cc in range(ML_CHUNKS_PER_TRIP):
            c = t * ML_CHUNKS_PER_TRIP + cc
            r0 = pl.multiple_of(c * CHUNK, CHUNK)
            for u, (d, p) in enumerate(units):
                v2 = dirs[d][2][pl.ds(r0, CHUNK), p * PAIR:(p + 1) * PAIR]
                v_ext = jnp.concatenate([jnp.where(lane64 < HEAD_DIM, v2, one_hi),
                                         jnp.where(lane64 >= HEAD_DIM, v2, one_lo)], axis=0)
                v_t = v_ext.astype(F32).T
                vt_scr[u, c] = v_t.astype(BF16)
                ut_scr[u, c] = _dot((v_t * we_scr[u, pl.ds(c, 1), :]).astype(BF16), key_tile(d, p, r0))
        return carry

    lax.fori_loop(0, chunks // ML_CHUNKS_PER_TRIP, contributions, 0)

    def outputs(t, carry):
        masks = [_pair_masks(True)[0], _pair_masks(False)[0]]
        items = []
        for u, (d, p) in enumerate(units):
            ct = c_scr[u]
            for k in range(ML_STEPS_PER_TRIP):
                step = t * ML_STEPS_PER_TRIP + k
                c = (chunks - 1 - step) if d else step
                r0 = pl.multiple_of(c * CHUNK, CHUNK)
                row = pl.ds(c, 1)
                items.append(dict(u=u, d=d, p=p, c=c, r0=r0, row=row, ct=ct.astype(BF16)))
                ct = a_scr[u, row, :] * ct + g_scr[u, row, :] * ut_scr[u, c]
            c_scr[u] = ct
        for it in items:
            u, d, p, r0, row = it["u"], it["d"], it["p"], it["r0"], it["row"]
            q_pair = _stack_pair(dirs[d][0][pl.ds(r0, CHUNK), p * PAIR:(p + 1) * PAIR])
            os_a, os_b = _outer_sum_operands(cs_scr[u, row, :], rt_scr[u, row, :])
            it["gram"] = _dot_nt(key_tile(d, p, r0), q_pair)
            it["osum"] = _dot_tn(os_a, os_b)
            it["st"] = _dot_nt(it.pop("ct"), q_pair)
        for it in items:
            u, c = it["u"], it["c"]
            s_t = it.pop("gram") * jnp.exp(jnp.where(masks[it["d"]], it.pop("osum"), NEG))
            it["intra"] = jnp.sum(s_t, axis=0, keepdims=True)
            it["num"] = _dot(vt_scr[u, c], s_t.astype(BF16))
        for it in items:
            u, d, p, row = it["u"], it["d"], it["p"], it["row"]
            a_inter = ai_scr[u, row, :]
            st = it.pop("st")
            den = a_inter * jnp.where(lane_row < HEAD_DIM, st[HEAD_DIM:HEAD_DIM + 1], st[0:1]) + it.pop("intra")
            out_t = (it.pop("num") + a_inter * st) * (1.0 / jnp.maximum(jnp.abs(den), em_scr[u, row, :]))
            out = out_t.T
            dirs[d][3][pl.ds(it["r0"], CHUNK), p * PAIR:(p + 1) * PAIR] = jnp.where(
                lane64 < HEAD_DIM, out[:CHUNK], out[CHUNK:])
        return carry

    lax.fori_loop(0, chunks // ML_STEPS_PER_TRIP, outputs, 0)


def _mlstm(q, k, v, gate_rows, i_bias, f_bias):
    b, s, _ = q.shape
    tb = min(SEQ_BLOCK, s)
    nb = s // tb
    chunks = tb // CHUNK
    ibr = _param_rows(i_bias, ML_HEADS)
    fbr = _param_rows(f_bias, ML_HEADS)

    def fwd(bi, j):
        return (bi, j, 0)

    def bwd(bi, j):
        return (bi, nb - 1 - j, 0)

    seq = lambda w, im: pl.BlockSpec((None, tb, w), im)
    rows = lambda lane0, d: _gate_row_spec(lane0, ML_HEADS, d, chunks, (lambda j: nb - 1 - j) if d else (lambda j: j))
    full = lambda a: pl.BlockSpec(a.shape, lambda bi, j: (0,) * a.ndim)
    out = jax.ShapeDtypeStruct((b, s, ML_WIDTH), F32)
    rows_scr = pltpu.VMEM((ML_UNITS, chunks, PAIR), F32)
    return pl.pallas_call(
        functools.partial(_mlstm_kernel, chunks=chunks),
        out_shape=(out, out),
        grid=(b, nb),
        in_specs=[seq(ML_WIDTH, fwd), seq(ML_WIDTH, fwd), seq(ML_WIDTH, fwd), rows(GATE_ML_I, 0), rows(GATE_ML_F, 0),
                  seq(ML_WIDTH, bwd), seq(ML_WIDTH, bwd), seq(ML_WIDTH, bwd), rows(GATE_ML_I, 1), rows(GATE_ML_F, 1),
                  full(ibr), full(fbr)],
        out_specs=(seq(ML_WIDTH, fwd), seq(ML_WIDTH, bwd)),
        scratch_shapes=[pltpu.VMEM((ML_UNITS, PAIR, PAIR), F32),
                        pltpu.VMEM((ML_UNITS, 1, PAIR), F32),
                        rows_scr, rows_scr, rows_scr, rows_scr, rows_scr, rows_scr, rows_scr,
                        pltpu.VMEM((ML_UNITS, chunks, PAIR, PAIR), BF16),
                        pltpu.VMEM((ML_UNITS, chunks, PAIR, PAIR), F32)],
        compiler_params=_params("parallel", "arbitrary"),
        name="mlstm",
    )(q, k, v, gate_rows, gate_rows, q, k, v, gate_rows, gate_rows, ibr, fbr)


GD_CHUNKS_PER_TRIP = 2
GD_INV_BLOCK = 16
GD_UNITS = N_DIR * GD_PAIRS


def _gdn_kernel(qf_ref, kf_ref, vf_ref, af_ref, bf_ref,
                qb_ref, kb_ref, vb_ref, ab_ref, bb_ref,
                alr_ref, dtr_ref,
                of_ref, ob_ref,
                s_scr, gc_scr, beta_scr, egc_scr, qsc_scr, fb_scr, eg_scr,
                u_scr, wq_scr, at_scr, ks_scr, *, chunks):
    j = pl.program_id(1)

    @pl.when(j == 0)
    def _():
        s_scr[...] = jnp.zeros_like(s_scr)

    lane_row = lax.broadcasted_iota(jnp.int32, (chunks, PAIR), 1)
    for d, (a_ref, b_ref) in enumerate(((af_ref, bf_ref), (ab_ref, bb_ref))):
        ucum = _pair_cumsum_matrix(rev=bool(d))
        last = 0 if d else CHUNK - 1
        for p in range(GD_PAIRS):
            u = d * GD_PAIRS + p
            g = -jnp.exp(alr_ref[d, p]) * _softplus(a_ref[p] + dtr_ref[d, p])
            gc = _exact_right(g, ucum)
            beta = _sigmoid(b_ref[p])
            g_last = jnp.where(lane_row < HEAD_DIM, gc[:, last:last + 1],
                               gc[:, HEAD_DIM + last:HEAD_DIM + last + 1])
            gc_scr[u] = gc
            beta_scr[u] = beta
            egc_scr[u] = jnp.exp(gc)
            qsc_scr[u] = jnp.exp(gc) * (HEAD_DIM ** -0.5)
            fb_scr[u] = jnp.exp(g_last - gc) * beta
            eg_scr[u] = jnp.exp(g_last)

    r = lax.broadcasted_iota(jnp.int32, (PAIR, PAIR), 0)
    cidx = lax.broadcasted_iota(jnp.int32, (PAIR, PAIR), 1)
    is_diag = r == cidx
    same16 = (r // GD_INV_BLOCK) == (cidx // GD_INV_BLOCK)
    eye_s = jnp.where(lax.broadcasted_iota(jnp.int32, (GD_INV_BLOCK, PAIR), 0)
                      == lax.broadcasted_iota(jnp.int32, (GD_INV_BLOCK, PAIR), 1) % GD_INV_BLOCK, 1.0, 0.0)
    dirs = ((qf_ref, kf_ref, vf_ref, of_ref), (qb_ref, kb_ref, vb_ref, ob_ref))
    units = [(d, p) for d in range(N_DIR) for p in range(GD_PAIRS)]

    def strip_of(bd):
        s = bd[0:GD_INV_BLOCK]
        for b in range(1, PAIR // GD_INV_BLOCK):
            s = s + bd[b * GD_INV_BLOCK:(b + 1) * GD_INV_BLOCK]
        return s

    def blockdiag_of(s):
        return jnp.where(same16, jnp.concatenate([s] * (PAIR // GD_INV_BLOCK), axis=0), jnp.zeros((), s.dtype))

    def precompute(t):
        masks = [_pair_masks(False), _pair_masks(True)]
        items = []
        for cc in range(GD_CHUNKS_PER_TRIP):
            step = t * GD_CHUNKS_PER_TRIP + cc
            for d in range(N_DIR):
                c = (chunks - 1 - step) if d else step
                for p in range(GD_PAIRS):
                    items.append(dict(d=d, p=p, c=c, r0=_aligned(c * CHUNK, CHUNK), row=pl.ds(c, 1),
                                      u=d * GD_PAIRS + p, cols=slice(p * PAIR, (p + 1) * PAIR)))
        for it in items:
            u, row = it["u"], it["row"]
            q_ref, k_ref = dirs[it["d"]][0], dirs[it["d"]][1]
            q_b = _stack_pair(q_ref[pl.ds(it["r0"], CHUNK), it["cols"]]).astype(BF16)
            k_b = _stack_pair(k_ref[pl.ds(it["r0"], CHUNK), it["cols"]]).astype(BF16)
            it["grams"] = _dot_nt(jnp.concatenate([q_b, k_b], axis=0), k_b)
            gc = gc_scr[u, row, :]
            it["osum"] = _dot_tn(*_outer_sum_operands(gc, -gc))
        yield
        for it in items:
            u, row = it["u"], it["row"]
            valid, strict = masks[it["d"]]
            decay = jnp.exp(jnp.where(valid, it.pop("osum"), NEG)) * beta_scr[u, row, :]
            grams = it.pop("grams")
            attn = (grams[:PAIR] * (HEAD_DIM ** -0.5) * decay).astype(BF16)
            q_scale = jnp.where(is_diag, qsc_scr[u, row, :], 0.0).astype(BF16)
            at_scr[u, it["c"]] = jnp.concatenate([attn, q_scale], axis=1)
            neg_a = jnp.where(strict, -(grams[PAIR:] * decay), 0.0)
            nd = jnp.where(same16, neg_a, 0.0)
            it["nl"] = (neg_a - nd).astype(BF16)
            nd_s = strip_of(nd)
            it["t_s"] = eye_s + nd_s
            it["pw_s"] = _dot(nd_s.astype(BF16), nd.astype(BF16))
        yield
        for _ in range(2):
            for it in items:
                pw_s = it["pw_s"].astype(BF16)
                both = _dot(jnp.concatenate([it["t_s"].astype(BF16), pw_s], axis=0), blockdiag_of(pw_s))
                it["t_s"] = it["t_s"] + both[:GD_INV_BLOCK]
                it["pw_s"] = both[GD_INV_BLOCK:]
        yield
        for it in items:
            pw_s = it.pop("pw_s").astype(BF16)
            t_s = it.pop("t_s")
            it["t_inv"] = blockdiag_of(t_s + _dot(t_s.astype(BF16), blockdiag_of(pw_s)))
        yield
        for it in items:
            it["x_b"] = it["t_inv"].astype(BF16)
            it["m"] = _dot(it["x_b"], it.pop("nl"))
        yield
        for it in items:
            m_b = it.pop("m").astype(BF16)
            both = _dot(m_b, jnp.concatenate([m_b, it.pop("x_b")], axis=1))
            it["m2"] = both[:, :PAIR]
            it["t_inv"] = it["t_inv"] + both[:, PAIR:]
        yield
        for it in items:
            it["t_inv"] = it["t_inv"] + _dot(it.pop("m2").astype(BF16), it["t_inv"].astype(BF16))
        yield
        for it in items:
            u, c, row = it["u"], it["c"], it["row"]
            q_ref, k_ref, v_ref = dirs[it["d"]][:3]
            rows = pl.ds(it["r0"], CHUNK)
            q_b = _stack_pair(q_ref[rows, it["cols"]]).astype(BF16)
            k_pair = _stack_pair(k_ref[rows, it["cols"]])
            v_b = _stack_pair(v_ref[rows, it["cols"]]).astype(BF16)
            t_inv = it.pop("t_inv")
            u_scr[u, c] = _dot(t_inv.astype(BF16), v_b)
            w = _dot((t_inv * egc_scr[u, row, :]).astype(BF16), k_pair.astype(BF16))
            wq_scr[u, c] = jnp.concatenate([w.astype(BF16), q_b], axis=0)
            ks_scr[u, c] = (k_pair.T * fb_scr[u, row, :]).astype(BF16)

    def scan(i):
        cs = (i, chunks - 1 - i)
        state = [s_scr[u] for u in range(len(units))]
        ws = [_dot(wq_scr[u, cs[d]], state[u].astype(BF16)) for u, (d, p) in enumerate(units)]
        yield
        v_qs = [jnp.concatenate([u_scr[u, cs[d]] - ws[u][:PAIR], ws[u][PAIR:]], axis=0).astype(BF16)
                for u, (d, p) in enumerate(units)]
        for u, (d, p) in enumerate(units):
            s_scr[u] = state[u] * eg_scr[u, pl.ds(cs[d], 1), :] + _dot(ks_scr[u, cs[d]], v_qs[u][:PAIR])
        yield
        for u, (d, p) in enumerate(units):
            o = _dot(at_scr[u, cs[d]], v_qs[u])
            r0 = _aligned(cs[d] * CHUNK, CHUNK)
            dirs[d][3][pl.ds(r0, CHUNK), p * PAIR:(p + 1) * PAIR] = o[:CHUNK] + o[CHUNK:]

    trips = chunks // GD_CHUNKS_PER_TRIP

    def scan_trip(t):
        for cc in range(GD_CHUNKS_PER_TRIP):
            yield from scan(t * GD_CHUNKS_PER_TRIP + cc)
            yield

    def fused(t, carry):
        filler = scan_trip(t - 1)
        for _ in precompute(t):
            next(filler, None)
        for _ in filler:
            pass
        return carry

    for _ in precompute(0):
        pass
    lax.fori_loop(1, trips, fused, 0)
    for _ in scan_trip(trips - 1):
        pass


def _gdn(q, k, v, gate_rows, a_log, dt_bias):
    b, s, _ = q.shape
    tb = min(GDN_SEQ_BLOCK, s)
    nb = s // tb
    chunks = tb // CHUNK
    alr = _param_rows(a_log, GDN_HEADS)
    dtr = _param_rows(dt_bias, GDN_HEADS)

    def fwd(bi, j):
        return (bi, j, 0)

    def bwd(bi, j):
        return (bi, nb - 1 - j, 0)

    seq = lambda w, im: pl.BlockSpec((None, tb, w), im)
    rows = lambda lane0, d: _gate_row_spec(lane0, GDN_HEADS, d, chunks, (lambda j: nb - 1 - j) if d else (lambda j: j))
    full = lambda a: pl.BlockSpec(a.shape, lambda bi, j: (0,) * a.ndim)
    out = jax.ShapeDtypeStruct((b, s, GDN_WIDTH), F32)
    rows_scr = pltpu.VMEM((GD_UNITS, chunks, PAIR), F32)
    return pl.pallas_call(
        functools.partial(_gdn_kernel, chunks=chunks),
        out_shape=(out, out),
        grid=(b, nb),
        in_specs=[seq(GDN_WIDTH, fwd), seq(GDN_WIDTH, fwd), seq(GDN_WIDTH, fwd), rows(GATE_GD_A, 0), rows(GATE_GD_B, 0),
                  seq(GDN_WIDTH, bwd), seq(GDN_WIDTH, bwd), seq(GDN_WIDTH, bwd), rows(GATE_GD_A, 1), rows(GATE_GD_B, 1),
                  full(alr), full(dtr)],
        out_specs=(seq(GDN_WIDTH, fwd), seq(GDN_WIDTH, bwd)),
        scratch_shapes=[pltpu.VMEM((GD_UNITS, PAIR, PAIR), F32),
                        rows_scr, rows_scr, rows_scr, rows_scr, rows_scr, rows_scr,
                        pltpu.VMEM((GD_UNITS, chunks, PAIR, PAIR), F32),
                        pltpu.VMEM((GD_UNITS, chunks, 2 * PAIR, PAIR), BF16),
                        pltpu.VMEM((GD_UNITS, chunks, PAIR, 2 * PAIR), BF16),
                        pltpu.VMEM((GD_UNITS, chunks, PAIR, PAIR), BF16)],
        compiler_params=_params("parallel", "arbitrary"),
        name="gated_deltanet",
    )(q, k, v, gate_rows, gate_rows, q, k, v, gate_rows, gate_rows, alr, dtr)


def _out_proj_kernel(x_ref, na_ref, hf_ref, hb_ref, mo_ref, mw_ref, of_ref, ob_ref, gz_ref, gw_ref,
                     w_ref, o_ref):
    hs = hf_ref[...] + hb_ref[...]
    y_ml = hs * lax.rsqrt(_head_sumsq(hs) * (1.0 / HEAD_DIM) + EPS) * mw_ref[...] * _sigmoid(mo_ref[...])
    os_ = of_ref[...] + ob_ref[...]
    z = gz_ref[...]
    y_gd = os_ * lax.rsqrt(_head_sumsq(os_) * (1.0 / HEAD_DIM) + EPS) * gw_ref[...] * (z * _sigmoid(z))
    acc = x_ref[...] + _dot(na_ref[...], w_ref[0:NA_WIDTH, :])
    acc = acc + _dot(y_ml.astype(BF16), w_ref[NA_WIDTH:NA_WIDTH + ML_WIDTH, :])
    acc = acc + _dot(y_gd.astype(BF16), w_ref[NA_WIDTH + ML_WIDTH:, :])
    o_ref[...] = acc


def _out_proj(x2d, y_na, hf, hb, ml_o, ml_norm_w, of, ob, gd_z, gdn_norm_w, w_out):
    t = x2d.shape[0]
    tm = min(TM_ROWS, t)
    row = lambda w: pl.BlockSpec((tm, w), lambda i: (i, 0))
    const = lambda r, c: pl.BlockSpec((r, c), lambda i: (0, 0))
    return pl.pallas_call(
        _out_proj_kernel,
        out_shape=jax.ShapeDtypeStruct((t, D_MODEL), F32),
        grid=(t // tm,),
        in_specs=[row(D_MODEL), row(NA_WIDTH), row(ML_WIDTH), row(ML_WIDTH), row(ML_WIDTH), const(1, ML_WIDTH),
                  row(GDN_WIDTH), row(GDN_WIDTH), row(GDN_WIDTH), const(1, GDN_WIDTH),
                  const(D_MODEL, D_MODEL)],
        out_specs=row(D_MODEL),
        compiler_params=_params("parallel"),
        name="out_proj",
    )(x2d, y_na, hf, hb, ml_o, ml_norm_w.reshape(1, ML_WIDTH).astype(F32), of, ob, gd_z,
      gdn_norm_w.reshape(1, GDN_WIDTH).astype(F32), w_out.astype(BF16))


def _mem_kv_kernel(m_ref, nw_ref, w_ref, k_ref, v_ref):
    h = _rms(m_ref[...], nw_ref[...]).astype(BF16)
    k_ref[...] = _dot(h, w_ref[:, :D_MODEL]).astype(BF16)
    v_ref[...] = _dot(h, w_ref[:, D_MODEL:]).astype(BF16)


def _mem_kv(mem2d, norm_w, w_kv):
    t = mem2d.shape[0]
    tm = min(TM_ROWS, t)
    out = jax.ShapeDtypeStruct((t, D_MODEL), BF16)
    return pl.pallas_call(
        _mem_kv_kernel,
        out_shape=(out, out),
        grid=(t // tm,),
        in_specs=[pl.BlockSpec((tm, D_MODEL), lambda i: (i, 0)),
                  pl.BlockSpec((1, D_MODEL), lambda i: (0, 0)),
                  pl.BlockSpec((D_MODEL, 2 * D_MODEL), lambda i: (0, 0))],
        out_specs=(pl.BlockSpec((tm, D_MODEL), lambda i: (i, 0)), pl.BlockSpec((tm, D_MODEL), lambda i: (i, 0))),
        compiler_params=_params("parallel"),
        name="mem_kv",
    )(mem2d, norm_w.reshape(1, D_MODEL).astype(F32), w_kv.astype(BF16))


def _xattn_kernel(x_ref, nw_ref, wq_ref, k_ref, v_ref, wo_ref, o_ref):
    x = x_ref[...]
    q = _dot(_rms(x, nw_ref[...]).astype(BF16), wq_ref[...]).astype(BF16)
    heads = [slice(h * XA_HEAD_DIM, (h + 1) * XA_HEAD_DIM) for h in range(XA_HEADS)]
    s = [_dot_nt(q[:, cols], k_ref[:, cols]) * (XA_HEAD_DIM ** -0.5) for cols in heads]
    e = [jnp.exp(s_h - jnp.max(s_h, axis=-1, keepdims=True)) for s_h in s]
    inv_l = [1.0 / jnp.sum(e_h, axis=-1, keepdims=True) for e_h in e]
    o = [(_dot(e_h.astype(BF16), v_ref[:, cols]) * il).astype(BF16) for e_h, il, cols in zip(e, inv_l, heads)]
    o_ref[...] = x + _dot(jnp.concatenate(o, axis=1), wo_ref[...])


def _xattn(x3d, norm_w, w_q, k, v, w_o):
    b, s, _ = x3d.shape
    tm = min(TM_ROWS, s)
    n_mem = k.shape[1]
    return pl.pallas_call(
        _xattn_kernel,
        out_shape=jax.ShapeDtypeStruct(x3d.shape, F32),
        grid=(b, s // tm),
        in_specs=[pl.BlockSpec((None, tm, D_MODEL), lambda bi, i: (bi, i, 0)),
                  pl.BlockSpec((1, D_MODEL), lambda bi, i: (0, 0)),
                  pl.BlockSpec((D_MODEL, D_MODEL), lambda bi, i: (0, 0)),
                  pl.BlockSpec((None, n_mem, D_MODEL), lambda bi, i: (bi, 0, 0)),
                  pl.BlockSpec((None, n_mem, D_MODEL), lambda bi, i: (bi, 0, 0)),
                  pl.BlockSpec((D_MODEL, D_MODEL), lambda bi, i: (0, 0))],
        out_specs=pl.BlockSpec((None, tm, D_MODEL), lambda bi, i: (bi, i, 0)),
        compiler_params=_params("parallel", "parallel"),
        name="cross_attention",
    )(x3d, norm_w.reshape(1, D_MODEL).astype(F32), w_q.astype(BF16), k, v, w_o.astype(BF16))


FF_CHUNK = 512


def _ffn_kernel(x_ref, nw_ref, w1_ref, w2_ref, fw_ref, o_ref, *, final_norm):
    x = x_ref[...]
    h = _rms(x, nw_ref[...]).astype(BF16)
    acc = x
    for c0 in range(0, D_FF, FF_CHUNK):
        a = jnp.maximum(_dot(h, w1_ref[:, c0:c0 + FF_CHUNK]), 0.0)
        acc = acc + _dot((a * a).astype(BF16), w2_ref[c0:c0 + FF_CHUNK, :])
    o_ref[...] = _rms(acc, fw_ref[...]) if final_norm else acc


def _ffn(x2d, norm_w, w1, w2, final_w, final_norm):
    t = x2d.shape[0]
    tm = min(TM_ROWS, t)
    return pl.pallas_call(
        functools.partial(_ffn_kernel, final_norm=final_norm),
        out_shape=jax.ShapeDtypeStruct((t, D_MODEL), F32),
        grid=(t // tm,),
        in_specs=[pl.BlockSpec((tm, D_MODEL), lambda i: (i, 0)),
                  pl.BlockSpec((1, D_MODEL), lambda i: (0, 0)),
                  pl.BlockSpec((D_MODEL, D_FF), lambda i: (0, 0), pipeline_mode=pl.Buffered(1)),
                  pl.BlockSpec((D_FF, D_MODEL), lambda i: (0, 0), pipeline_mode=pl.Buffered(1)),
                  pl.BlockSpec((1, D_MODEL), lambda i: (0, 0))],
        out_specs=pl.BlockSpec((tm, D_MODEL), lambda i: (i, 0)),
        compiler_params=_params("parallel"),
        name="ffn",
    )(x2d, norm_w.reshape(1, D_MODEL).astype(F32), w1.astype(BF16), w2.astype(BF16),
      final_w.reshape(1, D_MODEL).astype(F32))


def kernel(x, mem, norm_mix_w, w_in, na_rel_bias, ml_i_bias, ml_f_bias, ml_norm_w, gdn_conv_w, gdn_a_log,
           gdn_dt_bias, gdn_norm_w, w_out, norm_xa_w, norm_mem_w, w_xq, w_xkv, w_xo, norm_ffn_w, w_ff1,
           w_ff2, norm_out_w):
    b, s, d = x.shape
    depth = w_in.shape[0]
    x2d = x.reshape(b * s, d).astype(F32)
    mem2d = mem.reshape(-1, d).astype(F32)
    w_in_p = _permute_w_in(w_in)
    w_out_b, w_xq_b, w_xkv_b, w_xo_b, w_ff1_b, w_ff2_b = (
        w.astype(BF16) for w in (w_out, w_xq, w_xkv, w_xo, w_ff1, w_ff2))
    seq = lambda a: a.reshape(b, s, a.shape[-1])
    flat = lambda a: a.reshape(b * s, a.shape[-1])
    n_mem = mem.shape[1]
    for l in range(depth):
        (gq, gk, gv, na_q, na_k, na_v, ml_q, ml_k, ml_v, ml_o, gd_z, gates) = _in_proj(
            x2d, norm_mix_w[l].astype(F32), w_in_p[l], gdn_conv_w[l], s)
        gate_rows = _gate_rows(seq(gates))
        y_na = _neighbourhood_attention(seq(na_q), seq(na_k), seq(na_v), _na_bias_table(na_rel_bias[l]))
        hf, hb = _mlstm(seq(ml_q), seq(ml_k), seq(ml_v), gate_rows, ml_i_bias[l], ml_f_bias[l])
        of, ob = _gdn(seq(gq), seq(gk), seq(gv), gate_rows, gdn_a_log[l], gdn_dt_bias[l])
        x2d = _out_proj(x2d, flat(y_na), flat(hf), flat(hb), ml_o, ml_norm_w[l], flat(of), flat(ob), gd_z,
                        gdn_norm_w[l], w_out_b[l])
        mk, mv = _mem_kv(mem2d, norm_mem_w[l], w_xkv_b[l])
        x2d = _xattn(x2d.reshape(b, s, d), norm_xa_w[l], w_xq_b[l], mk.reshape(b, n_mem, d),
                     mv.reshape(b, n_mem, d), w_xo_b[l]).reshape(b * s, d)
        x2d = _ffn(x2d, norm_ffn_w[l], w_ff1_b[l], w_ff2_b[l], norm_out_w, final_norm=(l == depth - 1))
    return x2d.reshape(b, s, d).astype(x.dtype)
```

```python
import functools

import numpy as np
import jax
import jax.numpy as jnp
from jax import lax
from jax.experimental import pallas as pl
from jax.experimental.pallas import tpu as pltpu

F32 = jnp.float32
BF16 = jnp.bfloat16

D_MODEL = 1024
HEAD_DIM = 64
GRID_W = 64
NA_HEADS = 6
NA_WIN_ROWS = 8
NA_WIN_COLS = 16
ML_HEADS = 4
GDN_HEADS = 6
CONV_K = 5
N_DIR = 2
XA_HEADS = 4
XA_HEAD_DIM = D_MODEL // XA_HEADS
D_FF = 4 * D_MODEL
NA_WIDTH = NA_HEADS * HEAD_DIM
ML_WIDTH = ML_HEADS * HEAD_DIM
GDN_WIDTH = GDN_HEADS * HEAD_DIM
EPS = 1e-6
CHUNK = 64
PAIR = 2 * HEAD_DIM
NEG = -1e30

IN_SIZES = (NA_WIDTH, NA_WIDTH, NA_WIDTH,
            ML_WIDTH, ML_WIDTH, ML_WIDTH, ML_WIDTH, N_DIR * ML_HEADS, N_DIR * ML_HEADS,
            GDN_WIDTH, GDN_WIDTH, GDN_WIDTH, GDN_WIDTH, N_DIR * GDN_HEADS, N_DIR * GDN_HEADS)

GATE_GD_B = 0
GATE_GD_A = GATE_GD_B + N_DIR * GDN_HEADS
GATE_ML_I = GATE_GD_A + N_DIR * GDN_HEADS
GATE_ML_F = GATE_ML_I + N_DIR * ML_HEADS
GATE_USED = GATE_ML_F + N_DIR * ML_HEADS
GATE_LANES = 128

V7X_VMEM_LIMIT = 56 * 1024 * 1024
MXU_COLS = 256

TM_ROWS = 1024
NA_ROWS_PER_STEP = 8
SEQ_BLOCK = 1024
GDN_SEQ_BLOCK = 1024


def _params(*sem):
    return pltpu.CompilerParams(dimension_semantics=sem, vmem_limit_bytes=V7X_VMEM_LIMIT)


def _dot(a, b):
    return jnp.dot(a, b, preferred_element_type=F32)


def _dot_nt(a, b):
    return lax.dot_general(a, b, (((1,), (1,)), ((), ())), preferred_element_type=F32)


def _dot_tn(a, b):
    return lax.dot_general(a, b, (((0,), (0,)), ((), ())), preferred_element_type=F32)


def _split3(x):
    x1 = x.astype(BF16)
    r1 = x - x1.astype(F32)
    x2 = r1.astype(BF16)
    x3 = (r1 - x2.astype(F32)).astype(BF16)
    return x1, x2, x3


def _sum_right(x, m01):
    return _dot(x.astype(BF16), m01)


def _exact_right(x, m01):
    x1, x2, x3 = _split3(x)
    return _dot(x1, m01) + _dot(x2, m01) + _dot(x3, m01)


def _rms(x, w):
    ms = jnp.mean(x * x, axis=-1, keepdims=True)
    return x * lax.rsqrt(ms + EPS) * w


def _softplus(x):
    return jnp.maximum(x, 0.0) + jnp.log1p(jnp.exp(-jnp.abs(x)))


def _sigmoid(x):
    return 1.0 / (1.0 + jnp.exp(-x))


def _segment_mean_matrix(width):
    r = lax.broadcasted_iota(jnp.int32, (width, width), 0) // HEAD_DIM
    c = lax.broadcasted_iota(jnp.int32, (width, width), 1) // HEAD_DIM
    return jnp.where(r == c, 1.0, 0.0).astype(BF16)


def _head_sumsq(t):
    return _sum_right(t * t, _segment_mean_matrix(t.shape[-1]))


def _aligned(x, m):
    return x if isinstance(x, int) else pl.multiple_of(x, m)


def _pair_masks(rev):
    r = lax.broadcasted_iota(jnp.int32, (PAIR, PAIR), 0)
    c = lax.broadcasted_iota(jnp.int32, (PAIR, PAIR), 1)
    same = (r // HEAD_DIM) == (c // HEAD_DIM)
    t, s = r % HEAD_DIM, c % HEAD_DIM
    if rev:
        return same & (s >= t), same & (s > t)
    return same & (s <= t), same & (s < t)


def _stack_pair(x2):
    lane = lax.broadcasted_iota(jnp.int32, x2.shape, 1)
    zero = jnp.zeros_like(x2)
    return jnp.concatenate([jnp.where(lane < HEAD_DIM, x2, zero),
                            jnp.where(lane >= HEAD_DIM, x2, zero)], axis=0)


GD_PAIRS = GDN_HEADS // 2
CONV_HALO = 8
PREP_ROWS = 128
IN_SEGMENTS = (
    (NA_WIDTH, BF16), (NA_WIDTH, BF16), (NA_WIDTH, BF16),
    (ML_WIDTH, BF16), (ML_WIDTH, BF16), (ML_WIDTH, BF16), (ML_WIDTH, F32),
    (GDN_WIDTH, F32), (GATE_LANES, F32))
CONV_COLS = 3 * GDN_WIDTH
IN_COLS = CONV_COLS + sum(w for w, _ in IN_SEGMENTS)
IN_DOT_COLS = 3 * MXU_COLS


def _in_proj_kernel(x_ref, prev_ref, next_ref, nw_ref, w_ref, cw_ref, gq_ref, gk_ref, gv_ref, *rest,
                    tm, seq_len):
    out_refs, ext_scr = rest[:-1], rest[-1]
    i = pl.program_id(0)
    nw = nw_ref[...]
    h = _rms(x_ref[...], nw).astype(BF16)
    blocks_per_seq = seq_len // tm
    has_prev = (i % blocks_per_seq) > 0
    has_next = (i % blocks_per_seq) < blocks_per_seq - 1
    halo = _rms(jnp.concatenate([prev_ref[...], next_ref[...]], axis=0), nw).astype(BF16)
    w_conv = w_ref[:, 0:CONV_COLS]
    halo_p = _dot(halo, w_conv)
    ext_scr[0:CONV_HALO, :] = jnp.where(has_prev, halo_p[:CONV_HALO], 0.0)
    ext_scr[CONV_HALO + tm:, :] = jnp.where(has_next, halo_p[CONV_HALO:], 0.0)
    ext_scr[CONV_HALO:CONV_HALO + tm, :] = _dot(h, w_conv)

    base = CONV_HALO - CONV_K // 2
    seg = _segment_mean_matrix(PAIR)
    taps = [cw_ref[t:t + 1, :] for t in range(CONV_K)]

    def conv_tiles():
        for rb in range(tm // PREP_ROWS):
            r0 = rb * PREP_ROWS
            for ct in range(CONV_COLS // PAIR):
                cols = slice(ct * PAIR, (ct + 1) * PAIR)
                y = ext_scr[r0 + base:r0 + base + PREP_ROWS, cols] * taps[0][:, cols]
                for t in range(1, CONV_K):
                    y = y + ext_scr[r0 + base + t:r0 + base + t + PREP_ROWS, cols] * taps[t][:, cols]
                y = y * _sigmoid(y)
                which, off = divmod(ct * PAIR, GDN_WIDTH)
                if which < 2:
                    y = y * lax.rsqrt(_sum_right(y * y, seg) + EPS)
                (gq_ref, gk_ref, gv_ref)[which][r0:r0 + PREP_ROWS, off:off + PAIR] = y
                yield

    tiles = conv_tiles()
    n_tiles = (tm // PREP_ROWS) * (CONV_COLS // PAIR)
    starts = CONV_COLS + np.cumsum([0] + [w for w, _ in IN_SEGMENTS])
    chunk_starts = list(range(CONV_COLS, IN_COLS, IN_DOT_COLS))
    for k, c0 in enumerate(chunk_starts):
        c1 = min(c0 + IN_DOT_COLS, IN_COLS)
        acc = _dot(h, w_ref[:, c0:c1])
        for o_ref, s0, s1 in zip(out_refs, starts[:-1], starts[1:]):
            a, b = max(c0, int(s0)), min(c1, int(s1))
            if a < b:
                o_ref[:, a - int(s0):b - int(s0)] = acc[:, a - c0:b - c0].astype(o_ref.dtype)
        for _ in range(-(-n_tiles // len(chunk_starts))):
            next(tiles, None)
    for _ in tiles:
        pass


def _in_proj(x2d, norm_w, w_perm, conv_w, seq_len):
    t = x2d.shape[0]
    tm = min(TM_ROWS, seq_len)
    hb = tm // CONV_HALO
    last = t // CONV_HALO - 1
    gd = jax.ShapeDtypeStruct((t, GDN_WIDTH), F32)
    outs = (gd, gd, gd) + tuple(jax.ShapeDtypeStruct((t, w), dt) for w, dt in IN_SEGMENTS)
    row = lambda w: pl.BlockSpec((tm, w), lambda i: (i, 0))
    return pl.pallas_call(
        functools.partial(_in_proj_kernel, tm=tm, seq_len=seq_len),
        out_shape=outs,
        grid=(t // tm,),
        in_specs=[row(D_MODEL),
                  pl.BlockSpec((CONV_HALO, D_MODEL), lambda i: (jnp.maximum(i * hb - 1, 0), 0)),
                  pl.BlockSpec((CONV_HALO, D_MODEL), lambda i: (jnp.minimum((i + 1) * hb, last), 0)),
                  pl.BlockSpec((1, D_MODEL), lambda i: (0, 0)),
                  pl.BlockSpec((D_MODEL, IN_COLS), lambda i: (0, 0), pipeline_mode=pl.Buffered(1)),
                  pl.BlockSpec((CONV_K, CONV_COLS), lambda i: (0, 0))],
        out_specs=(row(GDN_WIDTH), row(GDN_WIDTH), row(GDN_WIDTH)) + tuple(row(w) for w, _ in IN_SEGMENTS),
        scratch_shapes=[pltpu.VMEM((tm + 2 * CONV_HALO, CONV_COLS), F32)],
        compiler_params=_params("parallel"),
        name="in_proj",
    )(x2d, x2d, x2d, norm_w.reshape(1, D_MODEL), w_perm, conv_w.astype(F32))


def _permute_w_in(w):
    parts = jnp.split(w, np.cumsum(IN_SIZES)[:-1], axis=-1)
    (na_q, na_k, na_v, ml_q, ml_k, ml_v, ml_o, ml_i, ml_f, gd_q, gd_k, gd_v, gd_z, gd_b, gd_a) = parts
    pad = jnp.zeros(w.shape[:-1] + (GATE_LANES - GATE_USED,), w.dtype)
    return jnp.concatenate([gd_q, gd_k, gd_v, na_q, na_k, na_v, ml_q, ml_k, ml_v, ml_o, gd_z,
                            gd_b, gd_a, ml_i, ml_f, pad], axis=-1).astype(BF16)


NA_BIAS_ROWS = 96


def _na_bias_kernel(rb_ref, o_ref):
    n = GRID_W * GRID_W
    dc = lax.broadcasted_iota(jnp.int32, (GATE_LANES, n), 0)
    col = lax.broadcasted_iota(jnp.int32, (GATE_LANES, n), 1)
    q, kc = col // GRID_W, col % GRID_W
    c0 = jnp.clip(q - NA_WIN_COLS // 2, 0, GRID_W - NA_WIN_COLS)
    valid = (kc >= c0) & (kc < c0 + NA_WIN_COLS)
    onehot = jnp.where(valid & (kc - q + (NA_WIN_COLS - 1) == dc), 1.0, 0.0).astype(BF16)
    o_ref[...] = jnp.where(valid[0:1, :], _exact_right(rb_ref[...], onehot), NEG)


def _na_bias_table(rel_bias):
    nh, ndr, ndc = rel_bias.shape
    rb = jnp.zeros((NA_BIAS_ROWS, GATE_LANES), F32).at[:nh * ndr, :ndc].set(
        rel_bias.astype(F32).reshape(nh * ndr, ndc))
    band = pl.pallas_call(
        _na_bias_kernel,
        out_shape=jax.ShapeDtypeStruct((NA_BIAS_ROWS, GRID_W * GRID_W), F32),
        name="na_bias_expand",
    )(rb)
    band = band[:nh * ndr].reshape(nh, ndr, GRID_W, GRID_W)
    first = np.clip(np.arange(ndr + 1) - 1, 0, ndr - 1)
    second = np.clip(np.arange(ndr + 1), 0, ndr - 1)
    tab = jnp.concatenate([band[:, first], band[:, second]], axis=-1)
    tab = tab.reshape(NA_HEADS // 2, 2, ndr + 1, GRID_W, PAIR)
    return jnp.moveaxis(tab, 1, 2).reshape(NA_HEADS // 2, ndr + 1, 2 * GRID_W, PAIR)


NA_ITEM_ROWS = 4
NA_KEY_ROWS = NA_ITEM_ROWS + NA_WIN_ROWS


NA_INNER_BLOCKS = NA_KEY_ROWS // 2 - 1


def _na_inner_bias(bias_tab):
    half = NA_WIN_ROWS // 2
    rows = []
    for i in range(NA_ITEM_ROWS):
        blocks = []
        for jp in range(i // 2, i // 2 + NA_INNER_BLOCKS):
            ok_a, ok_b = 0 <= 2 * jp - i < NA_WIN_ROWS, 0 <= 2 * jp + 1 - i < NA_WIN_ROWS
            entry = min(max(2 * jp - i + half, 0), 2 * NA_WIN_ROWS - 1)
            pen = np.concatenate([np.full(GRID_W, 0.0 if ok_a else NEG), np.full(GRID_W, 0.0 if ok_b else NEG)])
            blocks.append(bias_tab[:, entry] + jnp.asarray(pen, F32))
        rows.append(jnp.concatenate(blocks, axis=-1))
    return jnp.stack(rows, axis=1)


def _na_kernel(q_ref, k_ref, v_ref, bias_ref, inner_ref, o_ref, *, rows, rows_per_step):
    j = pl.program_id(1)
    nkeys = NA_KEY_ROWS * GRID_W
    half = NA_WIN_ROWS // 2
    lane = lax.broadcasted_iota(jnp.int32, (GRID_W, PAIR), 1)
    lane_blk = lax.broadcasted_iota(jnp.int32, (PAIR, PAIR), 1)

    def run(inner):
        items = []
        for g in range(rows_per_step // NA_ITEM_ROWS):
            r_first = j * rows_per_step + g * NA_ITEM_ROWS
            kr0 = jnp.clip(r_first - half, 0, rows - NA_KEY_ROWS)
            koff = pl.multiple_of(kr0 * GRID_W, GRID_W)
            entry, pen = {}, {}
            if not inner:
                for i in range(NA_ITEM_ROWS):
                    r = r_first + i
                    r0 = jnp.clip(r - half, 0, rows - NA_WIN_ROWS)
                    for jp in range(NA_KEY_ROWS // 2):
                        key_a = kr0 + 2 * jp
                        ok_a = (key_a >= r0) & (key_a < r0 + NA_WIN_ROWS)
                        ok_b = (key_a + 1 >= r0) & (key_a + 1 < r0 + NA_WIN_ROWS)
                        entry[i, jp] = jnp.clip(key_a - r + (NA_WIN_ROWS - 1), -1, 2 * NA_WIN_ROWS - 2) + 1
                        pen[i, jp] = jnp.where(lane_blk < GRID_W, jnp.where(ok_a, 0.0, NEG),
                                               jnp.where(ok_b, 0.0, NEG))
            for p in range(NA_HEADS // 2):
                items.append(dict(g=g, p=p, koff=koff, entry=entry, pen=pen, cols=slice(p * PAIR, (p + 1) * PAIR)))
        for it in items:
            q0 = it["g"] * NA_ITEM_ROWS * GRID_W
            q_lhs = jnp.concatenate(
                [_stack_pair(q_ref[q0 + i * GRID_W:q0 + (i + 1) * GRID_W, it["cols"]])
                 for i in range(NA_ITEM_ROWS)], axis=0)
            q_lhs = (q_lhs.astype(F32) * (HEAD_DIM ** -0.5)).astype(BF16)
            it["s"] = _dot_nt(q_lhs, k_ref[pl.ds(it["koff"], nkeys), it["cols"]])
        for it in items:
            s = it.pop("s")
            blocks = []
            for i in range(NA_ITEM_ROWS):
                if inner:
                    b0 = (i // 2) * PAIR
                    blocks.append(s[i * PAIR:(i + 1) * PAIR, b0:b0 + NA_INNER_BLOCKS * PAIR] + inner_ref[it["p"], i])
                else:
                    blocks.append(jnp.concatenate(
                        [s[i * PAIR:(i + 1) * PAIR, jp * PAIR:(jp + 1) * PAIR]
                         + bias_ref[it["p"], it["entry"][i, jp]] + it["pen"][i, jp]
                         for jp in range(NA_KEY_ROWS // 2)], axis=1))
            s = jnp.concatenate(blocks, axis=0)
            e = jnp.exp(s - jnp.max(s, axis=-1, keepdims=True))
            it["l"] = jnp.sum(e, axis=-1, keepdims=True)
            e = e.astype(BF16)
            if inner:
                zero = jnp.zeros((PAIR, PAIR), BF16)
                e = jnp.concatenate(
                    [jnp.concatenate(([zero] if i // 2 else []) + [e[i * PAIR:(i + 1) * PAIR]]
                                     + ([] if i // 2 else [zero]), axis=1) for i in range(NA_ITEM_ROWS)], axis=0)
            it["o"] = _dot(e, v_ref[pl.ds(it["koff"], nkeys), it["cols"]])
        for it in items:
            o = it.pop("o") * (1.0 / it.pop("l"))
            q0 = it["g"] * NA_ITEM_ROWS * GRID_W
            for i in range(NA_ITEM_ROWS):
                blk = o[i * PAIR:(i + 1) * PAIR]
                o_ref[q0 + i * GRID_W:q0 + (i + 1) * GRID_W, it["cols"]] = jnp.where(
                    lane < HEAD_DIM, blk[:GRID_W], blk[GRID_W:]).astype(o_ref.dtype)

    first_row = j * rows_per_step
    inner = (first_row >= half) & (first_row + rows_per_step - NA_ITEM_ROWS <= rows - NA_WIN_ROWS)

    @pl.when(inner)
    def _():
        run(True)

    @pl.when(jnp.logical_not(inner))
    def _():
        run(False)


def _neighbourhood_attention(q, k, v, bias_tab):
    b, s, _ = q.shape
    rows = s // GRID_W
    assert rows >= NA_KEY_ROWS
    rps = min(NA_ROWS_PER_STEP, rows)
    tq = rps * GRID_W
    inner_tab = _na_inner_bias(bias_tab)
    return pl.pallas_call(
        functools.partial(_na_kernel, rows=rows, rows_per_step=rps),
        out_shape=jax.ShapeDtypeStruct((b, s, NA_WIDTH), BF16),
        grid=(b, rows // rps),
        in_specs=[pl.BlockSpec((None, tq, NA_WIDTH), lambda bi, j: (bi, j, 0)),
                  pl.BlockSpec((None, s, NA_WIDTH), lambda bi, j: (bi, 0, 0)),
                  pl.BlockSpec((None, s, NA_WIDTH), lambda bi, j: (bi, 0, 0)),
                  pl.BlockSpec(bias_tab.shape, lambda bi, j: (0, 0, 0, 0)),
                  pl.BlockSpec(inner_tab.shape, lambda bi, j: (0, 0, 0, 0))],
        out_specs=pl.BlockSpec((None, tq, NA_WIDTH), lambda bi, j: (bi, j, 0)),
        compiler_params=_params("parallel", "arbitrary"),
        name="neighbourhood_attention",
    )(q, k, v, bias_tab, inner_tab)


def _gate_rows(gates):
    b, s, _ = gates.shape
    g = gates[:, :, :GATE_USED].reshape(b, s // CHUNK, CHUNK, GATE_USED // 2, 2)
    return jnp.transpose(g, (0, 3, 1, 4, 2)).reshape(b, GATE_USED // 2, s // CHUNK, PAIR)


def _gate_row_spec(lane0, heads, d, chunks, chunk_block):
    pairs = heads // 2
    unit_block = (lane0 // 2 + d * pairs) // pairs
    assert unit_block * pairs == lane0 // 2 + d * pairs
    return pl.BlockSpec((None, pairs, chunks, PAIR), lambda bi, j: (bi, unit_block, chunk_block(j), 0))


def _param_rows(p, heads):
    return jnp.repeat(p.astype(F32).reshape(N_DIR, heads // 2, 2), HEAD_DIM, axis=-1).reshape(
        N_DIR, heads // 2, 1, PAIR)


def _pair_cumsum_matrix(rev):
    r = lax.broadcasted_iota(jnp.int32, (PAIR, PAIR), 0)
    c = lax.broadcasted_iota(jnp.int32, (PAIR, PAIR), 1)
    same = (r // HEAD_DIM) == (c // HEAD_DIM)
    sp, s = r % HEAD_DIM, c % HEAD_DIM
    return jnp.where(same & ((sp >= s) if rev else (sp <= s)), 1.0, 0.0).astype(BF16)


ML_PAIRS = ML_HEADS // 2
ML_UNITS = N_DIR * ML_PAIRS
ML_CHUNKS_PER_TRIP = 4
ML_STEPS_PER_TRIP = 8
OS_ROWS = 16


def _outer_sum_operands(col_term, row_term):
    ones = jnp.ones((3, PAIR), BF16)
    zeros = jnp.zeros((OS_ROWS - 6, PAIR), BF16)
    a = jnp.concatenate(list(_split3(col_term)) + [ones, zeros], axis=0)
    b = jnp.concatenate([ones] + list(_split3(row_term)) + [zeros], axis=0)
    return a, b


def _mlstm_kernel(qf_ref, kf_ref, vf_ref, if_ref, ff_ref,
                  qb_ref, kb_ref, vb_ref, ib_ref, fb_ref,
                  ibr_ref, fbr_ref,
                  hf_ref, hb_ref,
                  c_scr, m_scr, cs_scr, rt_scr, ai_scr, em_scr, we_scr, a_scr, g_scr, vt_scr, ut_scr, *, chunks):
    j = pl.program_id(1)

    @pl.when(j == 0)
    def _():
        c_scr[...] = jnp.zeros_like(c_scr)
        m_scr[...] = jnp.zeros_like(m_scr)

    lane = lax.broadcasted_iota(jnp.int32, (chunks, PAIR), 1)
    pos = lane % HEAD_DIM
    lo = lane < HEAD_DIM
    for d, (i_ref, f_ref) in enumerate(((if_ref, ff_ref), (ib_ref, fb_ref))):
        rev = bool(d)
        ucum = _pair_cumsum_matrix(rev)
        last = 0 if rev else CHUNK - 1
        for p in range(ML_PAIRS):
            u = d * ML_PAIRS + p
            logf = -_softplus(-(f_ref[p] + fbr_ref[d, p]))
            bcum = _exact_right(logf, ucum)
            cs = i_ref[p] + ibr_ref[d, p] - bcum
            bl = jnp.where(lo, bcum[:, last:last + 1], bcum[:, HEAD_DIM + last:HEAD_DIM + last + 1])
            cm = cs
            for k in (1, 2, 4, 8, 16, 32):
                if rev:
                    cm = jnp.where(pos < HEAD_DIM - k, jnp.maximum(cm, pltpu.roll(cm, PAIR - k, axis=1)), cm)
                else:
                    cm = jnp.where(pos >= k, jnp.maximum(cm, pltpu.roll(cm, k, axis=1)), cm)
            w = bl + cs
            m_loc = jnp.where(lo, jnp.max(jnp.where(lo, w, NEG), axis=-1, keepdims=True),
                              jnp.max(jnp.where(lo, NEG, w), axis=-1, keepdims=True))
            m = m_scr[u]
            m_prev, a_rows, g_rows = [None] * chunks, [None] * chunks, [None] * chunks
            for c in (range(chunks - 1, -1, -1) if rev else range(chunks)):
                m_prev[c] = m
                m_new = jnp.maximum(bl[c:c + 1] + m, m_loc[c:c + 1])
                a_rows[c] = jnp.exp(bl[c:c + 1] + m - m_new)
                g_rows[c] = jnp.exp(m_loc[c:c + 1] - m_new)
                m = m_new
            m_scr[u] = m
            inter = bcum + jnp.concatenate(m_prev, axis=0)
            m_t = jnp.maximum(bcum + cm, inter)
            cs_scr[u] = cs
            rt_scr[u] = bcum - m_t
            ai_scr[u] = jnp.exp(inter - m_t)
            em_scr[u] = jnp.exp(-m_t)
            we_scr[u] = jnp.exp(w - m_loc)
            a_scr[u] = jnp.concatenate(a_rows, axis=0)
            g_scr[u] = jnp.concatenate(g_rows, axis=0)

    dirs = ((qf_ref, kf_ref, vf_ref, hf_ref), (qb_ref, kb_ref, vb_ref, hb_ref))
    units = [(d, p) for d in range(N_DIR) for p in range(ML_PAIRS)]
    lane64 = lax.broadcasted_iota(jnp.int32, (CHUNK, PAIR), 1)
    one_hi = jnp.where(lane64 == HEAD_DIM, 1.0, 0.0).astype(BF16)
    one_lo = jnp.where(lane64 == 0, 1.0, 0.0).astype(BF16)
    lane_row = lax.broadcasted_iota(jnp.int32, (1, PAIR), 1)

    def key_tile(d, p, r0):
        k_pair = _stack_pair(dirs[d][1][pl.ds(r0, CHUNK), p * PAIR:(p + 1) * PAIR])
        return (k_pair.astype(F32) * (HEAD_DIM ** -0.5)).astype(BF16)

    def contributions(t, carry):
        for cc in range(ML_CHUNKS_PER_TRIP):
            c = t * ML_CHUNKS_PER_TRIP + cc
            r0 = pl.multiple_of(c * CHUNK, CHUNK)
            for u, (d, p) in enumerate(units):
                v2 = dirs[d][2][pl.ds(r0, CHUNK), p * PAIR:(p + 1) * PAIR]
                v_ext = jnp.concatenate([jnp.where(lane64 < HEAD_DIM, v2, one_hi),
                                         jnp.where(lane64 >= HEAD_DIM, v2, one_lo)], axis=0)
                v_t = v_ext.astype(F32).T
                vt_scr[u, c] = v_t.astype(BF16)
                ut_scr[u, c] = _dot((v_t * we_scr[u, pl.ds(c, 1), :]).astype(BF16), key_tile(d, p, r0))
        return carry

    lax.fori_loop(0, chunks // ML_CHUNKS_PER_TRIP, contributions, 0)

    def outputs(t, carry):
        masks = [_pair_masks(True)[0], _pair_masks(False)[0]]
        items = []
        for u, (d, p) in enumerate(units):
            ct = c_scr[u]
            for k in range(ML_STEPS_PER_TRIP):
                step = t * ML_STEPS_PER_TRIP + k
                c = (chunks - 1 - step) if d else step
                r0 = pl.multiple_of(c * CHUNK, CHUNK)
                row = pl.ds(c, 1)
                items.append(dict(u=u, d=d, p=p, c=c, r0=r0, row=row, ct=ct.astype(BF16)))
                ct = a_scr[u, row, :] * ct + g_scr[u, row, :] * ut_scr[u, c]
            c_scr[u] = ct
        for it in items:
            u, d, p, r0, row = it["u"], it["d"], it["p"], it["r0"], it["row"]
            q_pair = _stack_pair(dirs[d][0][pl.ds(r0, CHUNK), p * PAIR:(p + 1) * PAIR])
            os_a, os_b = _outer_sum_operands(cs_scr[u, row, :], rt_scr[u, row, :])
            it["gram"] = _dot_nt(key_tile(d, p, r0), q_pair)
            it["osum"] = _dot_tn(os_a, os_b)
            it["st"] = _dot_nt(it.pop("ct"), q_pair)
        for it in items:
            u, c = it["u"], it["c"]
            s_t = it.pop("gram") * jnp.exp(jnp.where(masks[it["d"]], it.pop("osum"), NEG))
            it["intra"] = jnp.sum(s_t, axis=0, keepdims=True)
            it["num"] = _dot(vt_scr[u, c], s_t.astype(BF16))
        for it in items:
            u, d, p, row = it["u"], it["d"], it["p"], it["row"]
            a_inter = ai_scr[u, row, :]
            st = it.pop("st")
            den = a_inter * jnp.where(lane_row < HEAD_DIM, st[HEAD_DIM:HEAD_DIM + 1], st[0:1]) + it.pop("intra")
            out_t = (it.pop("num") + a_inter * st) * (1.0 / jnp.maximum(jnp.abs(den), em_scr[u, row, :]))
            out = out_t.T
            dirs[d][3][pl.ds(it["r0"], CHUNK), p * PAIR:(p + 1) * PAIR] = jnp.where(
                lane64 < HEAD_DIM, out[:CHUNK], out[CHUNK:])
        return carry

    lax.fori_loop(0, chunks // ML_STEPS_PER_TRIP, outputs, 0)


def _mlstm(q, k, v, gate_rows, i_bias, f_bias):
    b, s, _ = q.shape
    tb = min(SEQ_BLOCK, s)
    nb = s // tb
    chunks = tb // CHUNK
    assert s % tb == 0 and chunks % ML_CHUNKS_PER_TRIP == 0 and chunks % ML_STEPS_PER_TRIP == 0
    ibr = _param_rows(i_bias, ML_HEADS)
    fbr = _param_rows(f_bias, ML_HEADS)

    def fwd(bi, j):
        return (bi, j, 0)

    def bwd(bi, j):
        return (bi, nb - 1 - j, 0)

    seq = lambda w, im: pl.BlockSpec((None, tb, w), im)
    rows = lambda lane0, d: _gate_row_spec(lane0, ML_HEADS, d, chunks, (lambda j: nb - 1 - j) if d else (lambda j: j))
    full = lambda a: pl.BlockSpec(a.shape, lambda bi, j: (0,) * a.ndim)
    out = jax.ShapeDtypeStruct((b, s, ML_WIDTH), F32)
    rows_scr = pltpu.VMEM((ML_UNITS, chunks, PAIR), F32)
    return pl.pallas_call(
        functools.partial(_mlstm_kernel, chunks=chunks),
        out_shape=(out, out),
        grid=(b, nb),
        in_specs=[seq(ML_WIDTH, fwd), seq(ML_WIDTH, fwd), seq(ML_WIDTH, fwd), rows(GATE_ML_I, 0), rows(GATE_ML_F, 0),
                  seq(ML_WIDTH, bwd), seq(ML_WIDTH, bwd), seq(ML_WIDTH, bwd), rows(GATE_ML_I, 1), rows(GATE_ML_F, 1),
                  full(ibr), full(fbr)],
        out_specs=(seq(ML_WIDTH, fwd), seq(ML_WIDTH, bwd)),
        scratch_shapes=[pltpu.VMEM((ML_UNITS, PAIR, PAIR), F32),
                        pltpu.VMEM((ML_UNITS, 1, PAIR), F32),
                        rows_scr, rows_scr, rows_scr, rows_scr, rows_scr, rows_scr, rows_scr,
                        pltpu.VMEM((ML_UNITS, chunks, PAIR, PAIR), BF16),
                        pltpu.VMEM((ML_UNITS, chunks, PAIR, PAIR), F32)],
        compiler_params=_params("parallel", "arbitrary"),
        name="mlstm",
    )(q, k, v, gate_rows, gate_rows, q, k, v, gate_rows, gate_rows, ibr, fbr)


GD_CHUNKS_PER_TRIP = 2
GD_INV_BLOCK = 16
GD_UNITS = N_DIR * GD_PAIRS


def _gdn_kernel(qf_ref, kf_ref, vf_ref, af_ref, bf_ref,
                qb_ref, kb_ref, vb_ref, ab_ref, bb_ref,
                alr_ref, dtr_ref,
                of_ref, ob_ref,
                s_scr, gc_scr, beta_scr, egc_scr, qsc_scr, fb_scr, eg_scr,
                u_scr, wq_scr, at_scr, ks_scr, *, chunks):
    j = pl.program_id(1)

    @pl.when(j == 0)
    def _():
        s_scr[...] = jnp.zeros_like(s_scr)

    lane_row = lax.broadcasted_iota(jnp.int32, (chunks, PAIR), 1)
    for d, (a_ref, b_ref) in enumerate(((af_ref, bf_ref), (ab_ref, bb_ref))):
        ucum = _pair_cumsum_matrix(rev=bool(d))
        last = 0 if d else CHUNK - 1
        for p in range(GD_PAIRS):
            u = d * GD_PAIRS + p
            g = -jnp.exp(alr_ref[d, p]) * _softplus(a_ref[p] + dtr_ref[d, p])
            gc = _exact_right(g, ucum)
            beta = _sigmoid(b_ref[p])
            g_last = jnp.where(lane_row < HEAD_DIM, gc[:, last:last + 1],
                               gc[:, HEAD_DIM + last:HEAD_DIM + last + 1])
            gc_scr[u] = gc
            beta_scr[u] = beta
            egc_scr[u] = jnp.exp(gc)
            qsc_scr[u] = jnp.exp(gc) * (HEAD_DIM ** -0.5)
            fb_scr[u] = jnp.exp(g_last - gc) * beta
            eg_scr[u] = jnp.exp(g_last)

    r = lax.broadcasted_iota(jnp.int32, (PAIR, PAIR), 0)
    cidx = lax.broadcasted_iota(jnp.int32, (PAIR, PAIR), 1)
    is_diag = r == cidx
    same16 = (r // GD_INV_BLOCK) == (cidx // GD_INV_BLOCK)
    eye_s = jnp.where(lax.broadcasted_iota(jnp.int32, (GD_INV_BLOCK, PAIR), 0)
                      == lax.broadcasted_iota(jnp.int32, (GD_INV_BLOCK, PAIR), 1) % GD_INV_BLOCK, 1.0, 0.0)
    dirs = ((qf_ref, kf_ref, vf_ref, of_ref), (qb_ref, kb_ref, vb_ref, ob_ref))
    units = [(d, p) for d in range(N_DIR) for p in range(GD_PAIRS)]

    def strip_of(bd):
        s = bd[0:GD_INV_BLOCK]
        for b in range(1, PAIR // GD_INV_BLOCK):
            s = s + bd[b * GD_INV_BLOCK:(b + 1) * GD_INV_BLOCK]
        return s

    def blockdiag_of(s):
        return jnp.where(same16, jnp.concatenate([s] * (PAIR // GD_INV_BLOCK), axis=0), jnp.zeros((), s.dtype))

    def precompute(t):
        masks = [_pair_masks(False), _pair_masks(True)]
        items = []
        for cc in range(GD_CHUNKS_PER_TRIP):
            step = t * GD_CHUNKS_PER_TRIP + cc
            for d in range(N_DIR):
                c = (chunks - 1 - step) if d else step
                for p in range(GD_PAIRS):
                    items.append(dict(d=d, p=p, c=c, r0=_aligned(c * CHUNK, CHUNK), row=pl.ds(c, 1),
                                      u=d * GD_PAIRS + p, cols=slice(p * PAIR, (p + 1) * PAIR)))
        for it in items:
            u, row = it["u"], it["row"]
            q_ref, k_ref = dirs[it["d"]][0], dirs[it["d"]][1]
            q_b = _stack_pair(q_ref[pl.ds(it["r0"], CHUNK), it["cols"]]).astype(BF16)
            k_b = _stack_pair(k_ref[pl.ds(it["r0"], CHUNK), it["cols"]]).astype(BF16)
            it["grams"] = _dot_nt(jnp.concatenate([q_b, k_b], axis=0), k_b)
            gc = gc_scr[u, row, :]
            it["osum"] = _dot_tn(*_outer_sum_operands(gc, -gc))
        yield
        for it in items:
            u, row = it["u"], it["row"]
            valid, strict = masks[it["d"]]
            decay = jnp.exp(jnp.where(valid, it.pop("osum"), NEG)) * beta_scr[u, row, :]
            grams = it.pop("grams")
            attn = (grams[:PAIR] * (HEAD_DIM ** -0.5) * decay).astype(BF16)
            q_scale = jnp.where(is_diag, qsc_scr[u, row, :], 0.0).astype(BF16)
            at_scr[u, it["c"]] = jnp.concatenate([attn, q_scale], axis=1)
            neg_a = jnp.where(strict, -(grams[PAIR:] * decay), 0.0)
            nd = jnp.where(same16, neg_a, 0.0)
            it["nl"] = (neg_a - nd).astype(BF16)
            nd_s = strip_of(nd)
            it["t_s"] = eye_s + nd_s
            it["pw_s"] = _dot(nd_s.astype(BF16), nd.astype(BF16))
        yield
        for _ in range(2):
            for it in items:
                pw_s = it["pw_s"].astype(BF16)
                both = _dot(jnp.concatenate([it["t_s"].astype(BF16), pw_s], axis=0), blockdiag_of(pw_s))
                it["t_s"] = it["t_s"] + both[:GD_INV_BLOCK]
                it["pw_s"] = both[GD_INV_BLOCK:]
        yield
        for it in items:
            pw_s = it.pop("pw_s").astype(BF16)
            t_s = it.pop("t_s")
            it["t_inv"] = blockdiag_of(t_s + _dot(t_s.astype(BF16), blockdiag_of(pw_s)))
        yield
        for it in items:
            it["x_b"] = it["t_inv"].astype(BF16)
            it["m"] = _dot(it["x_b"], it.pop("nl"))
        yield
        for it in items:
            m_b = it.pop("m").astype(BF16)
            both = _dot(m_b, jnp.concatenate([m_b, it.pop("x_b")], axis=1))
            it["m2"] = both[:, :PAIR]
            it["t_inv"] = it["t_inv"] + both[:, PAIR:]
        yield
        for it in items:
            it["t_inv"] = it["t_inv"] + _dot(it.pop("m2").astype(BF16), it["t_inv"].astype(BF16))
        yield
        for it in items:
            u, c, row = it["u"], it["c"], it["row"]
            q_ref, k_ref, v_ref = dirs[it["d"]][:3]
            rows = pl.ds(it["r0"], CHUNK)
            q_b = _stack_pair(q_ref[rows, it["cols"]]).astype(BF16)
            k_pair = _stack_pair(k_ref[rows, it["cols"]])
            v_b = _stack_pair(v_ref[rows, it["cols"]]).astype(BF16)
            t_inv = it.pop("t_inv")
            u_scr[u, c] = _dot(t_inv.astype(BF16), v_b)
            w = _dot((t_inv * egc_scr[u, row, :]).astype(BF16), k_pair.astype(BF16))
            wq_scr[u, c] = jnp.concatenate([w.astype(BF16), q_b], axis=0)
            ks_scr[u, c] = (k_pair.T * fb_scr[u, row, :]).astype(BF16)

    def scan(i):
        cs = (i, chunks - 1 - i)
        state = [s_scr[u] for u in range(len(units))]
        ws = [_dot(wq_scr[u, cs[d]], state[u].astype(BF16)) for u, (d, p) in enumerate(units)]
        yield
        v_qs = [jnp.concatenate([u_scr[u, cs[d]] - ws[u][:PAIR], ws[u][PAIR:]], axis=0).astype(BF16)
                for u, (d, p) in enumerate(units)]
        for u, (d, p) in enumerate(units):
            s_scr[u] = state[u] * eg_scr[u, pl.ds(cs[d], 1), :] + _dot(ks_scr[u, cs[d]], v_qs[u][:PAIR])
        for u, (d, p) in enumerate(units):
            o = _dot(at_scr[u, cs[d]], v_qs[u])
            r0 = _aligned(cs[d] * CHUNK, CHUNK)
            dirs[d][3][pl.ds(r0, CHUNK), p * PAIR:(p + 1) * PAIR] = o[:CHUNK] + o[CHUNK:]

    trips = chunks // GD_CHUNKS_PER_TRIP

    def scan_trip(t):
        for cc in range(GD_CHUNKS_PER_TRIP):
            yield from scan(t * GD_CHUNKS_PER_TRIP + cc)
            yield

    def fused(t, carry):
        filler = scan_trip(t - 1)
        for _ in precompute(t):
            next(filler, None)
        for _ in filler:
            pass
        return carry

    for _ in precompute(0):
        pass
    lax.fori_loop(1, trips, fused, 0)
    for _ in scan_trip(trips - 1):
        pass


def _gdn(q, k, v, gate_rows, a_log, dt_bias):
    b, s, _ = q.shape
    tb = min(GDN_SEQ_BLOCK, s)
    nb = s // tb
    chunks = tb // CHUNK
    assert s % tb == 0 and chunks % GD_CHUNKS_PER_TRIP == 0
    alr = _param_rows(a_log, GDN_HEADS)
    dtr = _param_rows(dt_bias, GDN_HEADS)

    def fwd(bi, j):
        return (bi, j, 0)

    def bwd(bi, j):
        return (bi, nb - 1 - j, 0)

    seq = lambda w, im: pl.BlockSpec((None, tb, w), im)
    rows = lambda lane0, d: _gate_row_spec(lane0, GDN_HEADS, d, chunks, (lambda j: nb - 1 - j) if d else (lambda j: j))
    full = lambda a: pl.BlockSpec(a.shape, lambda bi, j: (0,) * a.ndim)
    out = jax.ShapeDtypeStruct((b, s, GDN_WIDTH), F32)
    rows_scr = pltpu.VMEM((GD_UNITS, chunks, PAIR), F32)
    return pl.pallas_call(
        functools.partial(_gdn_kernel, chunks=chunks),
        out_shape=(out, out),
        grid=(b, nb),
        in_specs=[seq(GDN_WIDTH, fwd), seq(GDN_WIDTH, fwd), seq(GDN_WIDTH, fwd), rows(GATE_GD_A, 0), rows(GATE_GD_B, 0),
                  seq(GDN_WIDTH, bwd), seq(GDN_WIDTH, bwd), seq(GDN_WIDTH, bwd), rows(GATE_GD_A, 1), rows(GATE_GD_B, 1),
                  full(alr), full(dtr)],
        out_specs=(seq(GDN_WIDTH, fwd), seq(GDN_WIDTH, bwd)),
        scratch_shapes=[pltpu.VMEM((GD_UNITS, PAIR, PAIR), F32),
                        rows_scr, rows_scr, rows_scr, rows_scr, rows_scr, rows_scr,
                        pltpu.VMEM((GD_UNITS, chunks, PAIR, PAIR), F32),
                        pltpu.VMEM((GD_UNITS, chunks, 2 * PAIR, PAIR), BF16),
                        pltpu.VMEM((GD_UNITS, chunks, PAIR, 2 * PAIR), BF16),
                        pltpu.VMEM((GD_UNITS, chunks, PAIR, PAIR), BF16)],
        compiler_params=_params("parallel", "arbitrary"),
        name="gated_deltanet",
    )(q, k, v, gate_rows, gate_rows, q, k, v, gate_rows, gate_rows, alr, dtr)


def _out_proj_kernel(x_ref, na_ref, hf_ref, hb_ref, mo_ref, mw_ref, of_ref, ob_ref, gz_ref, gw_ref,
                     w_ref, o_ref):
    hs = hf_ref[...] + hb_ref[...]
    y_ml = hs * lax.rsqrt(_head_sumsq(hs) * (1.0 / HEAD_DIM) + EPS) * mw_ref[...] * _sigmoid(mo_ref[...])
    os_ = of_ref[...] + ob_ref[...]
    z = gz_ref[...]
    y_gd = os_ * lax.rsqrt(_head_sumsq(os_) * (1.0 / HEAD_DIM) + EPS) * gw_ref[...] * (z * _sigmoid(z))
    acc = x_ref[...] + _dot(na_ref[...], w_ref[0:NA_WIDTH, :])
    acc = acc + _dot(y_ml.astype(BF16), w_ref[NA_WIDTH:NA_WIDTH + ML_WIDTH, :])
    acc = acc + _dot(y_gd.astype(BF16), w_ref[NA_WIDTH + ML_WIDTH:, :])
    o_ref[...] = acc


def _out_proj(x2d, y_na, hf, hb, ml_o, ml_norm_w, of, ob, gd_z, gdn_norm_w, w_out):
    t = x2d.shape[0]
    tm = min(TM_ROWS, t)
    row = lambda w: pl.BlockSpec((tm, w), lambda i: (i, 0))
    const = lambda r, c: pl.BlockSpec((r, c), lambda i: (0, 0))
    return pl.pallas_call(
        _out_proj_kernel,
        out_shape=jax.ShapeDtypeStruct((t, D_MODEL), F32),
        grid=(t // tm,),
        in_specs=[row(D_MODEL), row(NA_WIDTH), row(ML_WIDTH), row(ML_WIDTH), row(ML_WIDTH), const(1, ML_WIDTH),
                  row(GDN_WIDTH), row(GDN_WIDTH), row(GDN_WIDTH), const(1, GDN_WIDTH),
                  const(D_MODEL, D_MODEL)],
        out_specs=row(D_MODEL),
        compiler_params=_params("parallel"),
        name="out_proj",
    )(x2d, y_na, hf, hb, ml_o, ml_norm_w.reshape(1, ML_WIDTH).astype(F32), of, ob, gd_z,
      gdn_norm_w.reshape(1, GDN_WIDTH).astype(F32), w_out.astype(BF16))


def _mem_kv_kernel(m_ref, nw_ref, w_ref, k_ref, v_ref):
    h = _rms(m_ref[...], nw_ref[...]).astype(BF16)
    k_ref[...] = _dot(h, w_ref[:, :D_MODEL]).astype(BF16)
    v_ref[...] = _dot(h, w_ref[:, D_MODEL:]).astype(BF16)


def _mem_kv(mem2d, norm_w, w_kv):
    t = mem2d.shape[0]
    tm = min(TM_ROWS, t)
    out = jax.ShapeDtypeStruct((t, D_MODEL), BF16)
    return pl.pallas_call(
        _mem_kv_kernel,
        out_shape=(out, out),
        grid=(t // tm,),
        in_specs=[pl.BlockSpec((tm, D_MODEL), lambda i: (i, 0)),
                  pl.BlockSpec((1, D_MODEL), lambda i: (0, 0)),
                  pl.BlockSpec((D_MODEL, 2 * D_MODEL), lambda i: (0, 0))],
        out_specs=(pl.BlockSpec((tm, D_MODEL), lambda i: (i, 0)), pl.BlockSpec((tm, D_MODEL), lambda i: (i, 0))),
        compiler_params=_params("parallel"),
        name="mem_kv",
    )(mem2d, norm_w.reshape(1, D_MODEL).astype(F32), w_kv.astype(BF16))


def _xattn_kernel(x_ref, nw_ref, wq_ref, k_ref, v_ref, wo_ref, o_ref):
    x = x_ref[...]
    q = _dot(_rms(x, nw_ref[...]).astype(BF16), wq_ref[...]).astype(BF16)
    heads = [slice(h * XA_HEAD_DIM, (h + 1) * XA_HEAD_DIM) for h in range(XA_HEADS)]
    s = [_dot_nt(q[:, cols], k_ref[:, cols]) * (XA_HEAD_DIM ** -0.5) for cols in heads]
    e = [jnp.exp(s_h - jnp.max(s_h, axis=-1, keepdims=True)) for s_h in s]
    inv_l = [1.0 / jnp.sum(e_h, axis=-1, keepdims=True) for e_h in e]
    o = [(_dot(e_h.astype(BF16), v_ref[:, cols]) * il).astype(BF16) for e_h, il, cols in zip(e, inv_l, heads)]
    o_ref[...] = x + _dot(jnp.concatenate(o, axis=1), wo_ref[...])


def _xattn(x3d, norm_w, w_q, k, v, w_o):
    b, s, _ = x3d.shape
    tm = min(TM_ROWS, s)
    n_mem = k.shape[1]
    return pl.pallas_call(
        _xattn_kernel,
        out_shape=jax.ShapeDtypeStruct(x3d.shape, F32),
        grid=(b, s // tm),
        in_specs=[pl.BlockSpec((None, tm, D_MODEL), lambda bi, i: (bi, i, 0)),
                  pl.BlockSpec((1, D_MODEL), lambda bi, i: (0, 0)),
                  pl.BlockSpec((D_MODEL, D_MODEL), lambda bi, i: (0, 0)),
                  pl.BlockSpec((None, n_mem, D_MODEL), lambda bi, i: (bi, 0, 0)),
                  pl.BlockSpec((None, n_mem, D_MODEL), lambda bi, i: (bi, 0, 0)),
                  pl.BlockSpec((D_MODEL, D_MODEL), lambda bi, i: (0, 0))],
        out_specs=pl.BlockSpec((None, tm, D_MODEL), lambda bi, i: (bi, i, 0)),
        compiler_params=_params("parallel", "parallel"),
        name="cross_attention",
    )(x3d, norm_w.reshape(1, D_MODEL).astype(F32), w_q.astype(BF16), k, v, w_o.astype(BF16))


FF_CHUNK = 512


def _ffn_kernel(x_ref, nw_ref, w1_ref, w2_ref, fw_ref, o_ref, *, final_norm):
    x = x_ref[...]
    h = _rms(x, nw_ref[...]).astype(BF16)
    acc = x
    for c0 in range(0, D_FF, FF_CHUNK):
        a = jnp.maximum(_dot(h, w1_ref[:, c0:c0 + FF_CHUNK]), 0.0)
        acc = acc + _dot((a * a).astype(BF16), w2_ref[c0:c0 + FF_CHUNK, :])
    o_ref[...] = _rms(acc, fw_ref[...]) if final_norm else acc


def _ffn(x2d, norm_w, w1, w2, final_w, final_norm):
    t = x2d.shape[0]
    tm = min(TM_ROWS, t)
    return pl.pallas_call(
        functools.partial(_ffn_kernel, final_norm=final_norm),
        out_shape=jax.ShapeDtypeStruct((t, D_MODEL), F32),
        grid=(t // tm,),
        in_specs=[pl.BlockSpec((tm, D_MODEL), lambda i: (i, 0)),
                  pl.BlockSpec((1, D_MODEL), lambda i: (0, 0)),
                  pl.BlockSpec((D_MODEL, D_FF), lambda i: (0, 0), pipeline_mode=pl.Buffered(1)),
                  pl.BlockSpec((D_FF, D_MODEL), lambda i: (0, 0), pipeline_mode=pl.Buffered(1)),
                  pl.BlockSpec((1, D_MODEL), lambda i: (0, 0))],
        out_specs=pl.BlockSpec((tm, D_MODEL), lambda i: (i, 0)),
        compiler_params=_params("parallel"),
        name="ffn",
    )(x2d, norm_w.reshape(1, D_MODEL).astype(F32), w1.astype(BF16), w2.astype(BF16),
      final_w.reshape(1, D_MODEL).astype(F32))


def kernel(x, mem, norm_mix_w, w_in, na_rel_bias, ml_i_bias, ml_f_bias, ml_norm_w, gdn_conv_w, gdn_a_log,
           gdn_dt_bias, gdn_norm_w, w_out, norm_xa_w, norm_mem_w, w_xq, w_xkv, w_xo, norm_ffn_w, w_ff1,
           w_ff2, norm_out_w):
    b, s, d = x.shape
    depth = w_in.shape[0]
    x2d = x.reshape(b * s, d).astype(F32)
    mem2d = mem.reshape(-1, d).astype(F32)
    w_in_p = _permute_w_in(w_in)
    w_out_b, w_xq_b, w_xkv_b, w_xo_b, w_ff1_b, w_ff2_b = (
        w.astype(BF16) for w in (w_out, w_xq, w_xkv, w_xo, w_ff1, w_ff2))
    seq = lambda a: a.reshape(b, s, a.shape[-1])
    flat = lambda a: a.reshape(b * s, a.shape[-1])
    n_mem = mem.shape[1]
    for l in range(depth):
        (gq, gk, gv, na_q, na_k, na_v, ml_q, ml_k, ml_v, ml_o, gd_z, gates) = _in_proj(
            x2d, norm_mix_w[l].astype(F32), w_in_p[l], gdn_conv_w[l], s)
        gate_rows = _gate_rows(seq(gates))
        y_na = _neighbourhood_attention(seq(na_q), seq(na_k), seq(na_v), _na_bias_table(na_rel_bias[l]))
        hf, hb = _mlstm(seq(ml_q), seq(ml_k), seq(ml_v), gate_rows, ml_i_bias[l], ml_f_bias[l])
        of, ob = _gdn(seq(gq), seq(gk), seq(gv), gate_rows, gdn_a_log[l], gdn_dt_bias[l])
        x2d = _out_proj(x2d, flat(y_na), flat(hf), flat(hb), ml_o, ml_norm_w[l], flat(of), flat(ob), gd_z,
                        gdn_norm_w[l], w_out_b[l])
        mk, mv = _mem_kv(mem2d, norm_mem_w[l], w_xkv_b[l])
        x2d = _xattn(x2d.reshape(b, s, d), norm_xa_w[l], w_xq_b[l], mk.reshape(b, n_mem, d),
                     mv.reshape(b, n_mem, d), w_xo_b[l]).reshape(b * s, d)
        x2d = _ffn(x2d, norm_ffn_w[l], w_ff1_b[l], w_ff2_b[l], norm_out_w, final_norm=(l == depth - 1))
    return x2d.reshape(b, s, d).astype(x.dtype)
```

```python
import functools

import numpy as np
import jax
import jax.numpy as jnp
from jax import lax
from jax.experimental import pallas as pl
from jax.experimental.pallas import tpu as pltpu

F32 = jnp.float32
BF16 = jnp.bfloat16

D_MODEL = 1024
HEAD_DIM = 64
GRID_W = 64
NA_HEADS = 6
NA_WIN_ROWS = 8
NA_WIN_COLS = 16
ML_HEADS = 4
GDN_HEADS = 6
CONV_K = 5
N_DIR = 2
XA_HEADS = 4
XA_HEAD_DIM = D_MODEL // XA_HEADS
D_FF = 4 * D_MODEL
NA_WIDTH = NA_HEADS * HEAD_DIM
ML_WIDTH = ML_HEADS * HEAD_DIM
GDN_WIDTH = GDN_HEADS * HEAD_DIM
EPS = 1e-6
CHUNK = 64
PAIR = 2 * HEAD_DIM
NEG = -1e30

IN_SIZES = (NA_WIDTH, NA_WIDTH, NA_WIDTH,
            ML_WIDTH, ML_WIDTH, ML_WIDTH, ML_WIDTH, N_DIR * ML_HEADS, N_DIR * ML_HEADS,
            GDN_WIDTH, GDN_WIDTH, GDN_WIDTH, GDN_WIDTH, N_DIR * GDN_HEADS, N_DIR * GDN_HEADS)

GATE_GD_B = 0
GATE_GD_A = GATE_GD_B + N_DIR * GDN_HEADS
GATE_ML_I = GATE_GD_A + N_DIR * GDN_HEADS
GATE_ML_F = GATE_ML_I + N_DIR * ML_HEADS
GATE_USED = GATE_ML_F + N_DIR * ML_HEADS
GATE_LANES = 128

V7X_VMEM_LIMIT = 56 * 1024 * 1024
MXU_COLS = 256

TM_ROWS = 1024
NA_ROWS_PER_STEP = 8
SEQ_BLOCK = 1024
GDN_SEQ_BLOCK = 1024


def _params(*sem):
    return pltpu.CompilerParams(dimension_semantics=sem, vmem_limit_bytes=V7X_VMEM_LIMIT)


def _dot(a, b):
    return jnp.dot(a, b, preferred_element_type=F32)


def _dot_nt(a, b):
    return lax.dot_general(a, b, (((1,), (1,)), ((), ())), preferred_element_type=F32)


def _dot_tn(a, b):
    return lax.dot_general(a, b, (((0,), (0,)), ((), ())), preferred_element_type=F32)


def _split3(x):
    x1 = x.astype(BF16)
    r1 = x - x1.astype(F32)
    x2 = r1.astype(BF16)
    x3 = (r1 - x2.astype(F32)).astype(BF16)
    return x1, x2, x3


def _sum_right(x, m01):
    return _dot(x.astype(BF16), m01)


def _exact_right(x, m01):
    x1, x2, x3 = _split3(x)
    return _dot(x1, m01) + _dot(x2, m01) + _dot(x3, m01)


def _rms(x, w):
    ms = jnp.mean(x * x, axis=-1, keepdims=True)
    return x * lax.rsqrt(ms + EPS) * w


def _softplus(x):
    return jnp.maximum(x, 0.0) + jnp.log1p(jnp.exp(-jnp.abs(x)))


def _sigmoid(x):
    return 1.0 / (1.0 + jnp.exp(-x))


def _segment_mean_matrix(width):
    r = lax.broadcasted_iota(jnp.int32, (width, width), 0) // HEAD_DIM
    c = lax.broadcasted_iota(jnp.int32, (width, width), 1) // HEAD_DIM
    return jnp.where(r == c, 1.0, 0.0).astype(BF16)


def _head_sumsq(t):
    return _sum_right(t * t, _segment_mean_matrix(t.shape[-1]))


def _aligned(x, m):
    return x if isinstance(x, int) else pl.multiple_of(x, m)


def _pair_masks(rev):
    r = lax.broadcasted_iota(jnp.int32, (PAIR, PAIR), 0)
    c = lax.broadcasted_iota(jnp.int32, (PAIR, PAIR), 1)
    same = (r // HEAD_DIM) == (c // HEAD_DIM)
    t, s = r % HEAD_DIM, c % HEAD_DIM
    if rev:
        return same & (s >= t), same & (s > t)
    return same & (s <= t), same & (s < t)


def _stack_pair(x2):
    lane = lax.broadcasted_iota(jnp.int32, x2.shape, 1)
    zero = jnp.zeros_like(x2)
    return jnp.concatenate([jnp.where(lane < HEAD_DIM, x2, zero),
                            jnp.where(lane >= HEAD_DIM, x2, zero)], axis=0)


GD_PAIRS = GDN_HEADS // 2
CONV_HALO = 8
PREP_ROWS = 128
IN_SEGMENTS = (
    (NA_WIDTH, BF16), (NA_WIDTH, BF16), (NA_WIDTH, BF16),
    (ML_WIDTH, BF16), (ML_WIDTH, BF16), (ML_WIDTH, BF16), (ML_WIDTH, F32),
    (GDN_WIDTH, F32), (GATE_LANES, F32))
CONV_COLS = 3 * GDN_WIDTH
IN_COLS = CONV_COLS + sum(w for w, _ in IN_SEGMENTS)
IN_DOT_COLS = 3 * MXU_COLS


def _in_proj_kernel(x_ref, prev_ref, next_ref, nw_ref, w_ref, cw_ref, gq_ref, gk_ref, gv_ref, *rest,
                    tm, seq_len):
    out_refs, ext_scr = rest[:-1], rest[-1]
    i = pl.program_id(0)
    nw = nw_ref[...]
    h = _rms(x_ref[...], nw).astype(BF16)
    blocks_per_seq = seq_len // tm
    has_prev = (i % blocks_per_seq) > 0
    has_next = (i % blocks_per_seq) < blocks_per_seq - 1
    halo = _rms(jnp.concatenate([prev_ref[...], next_ref[...]], axis=0), nw).astype(BF16)
    w_conv = w_ref[:, 0:CONV_COLS]
    halo_p = _dot(halo, w_conv)
    ext_scr[0:CONV_HALO, :] = jnp.where(has_prev, halo_p[:CONV_HALO], 0.0)
    ext_scr[CONV_HALO + tm:, :] = jnp.where(has_next, halo_p[CONV_HALO:], 0.0)
    ext_scr[CONV_HALO:CONV_HALO + tm, :] = _dot(h, w_conv)

    base = CONV_HALO - CONV_K // 2
    seg = _segment_mean_matrix(PAIR)
    taps = [cw_ref[t:t + 1, :] for t in range(CONV_K)]

    def conv_tiles():
        for rb in range(tm // PREP_ROWS):
            r0 = rb * PREP_ROWS
            for ct in range(CONV_COLS // PAIR):
                cols = slice(ct * PAIR, (ct + 1) * PAIR)
                y = ext_scr[r0 + base:r0 + base + PREP_ROWS, cols] * taps[0][:, cols]
                for t in range(1, CONV_K):
                    y = y + ext_scr[r0 + base + t:r0 + base + t + PREP_ROWS, cols] * taps[t][:, cols]
                y = y * _sigmoid(y)
                which, off = divmod(ct * PAIR, GDN_WIDTH)
                if which < 2:
                    y = y * lax.rsqrt(_sum_right(y * y, seg) + EPS)
                (gq_ref, gk_ref, gv_ref)[which][r0:r0 + PREP_ROWS, off:off + PAIR] = y
                yield

    tiles = conv_tiles()
    n_tiles = (tm // PREP_ROWS) * (CONV_COLS // PAIR)
    starts = CONV_COLS + np.cumsum([0] + [w for w, _ in IN_SEGMENTS])
    chunk_starts = list(range(CONV_COLS, IN_COLS, IN_DOT_COLS))
    for k, c0 in enumerate(chunk_starts):
        c1 = min(c0 + IN_DOT_COLS, IN_COLS)
        acc = _dot(h, w_ref[:, c0:c1])
        for o_ref, s0, s1 in zip(out_refs, starts[:-1], starts[1:]):
            a, b = max(c0, int(s0)), min(c1, int(s1))
            if a < b:
                o_ref[:, a - int(s0):b - int(s0)] = acc[:, a - c0:b - c0].astype(o_ref.dtype)
        for _ in range(-(-n_tiles // len(chunk_starts))):
            next(tiles, None)
    for _ in tiles:
        pass


def _in_proj(x2d, norm_w, w_perm, conv_w, seq_len):
    t = x2d.shape[0]
    tm = min(TM_ROWS, seq_len)
    hb = tm // CONV_HALO
    last = t // CONV_HALO - 1
    gd = jax.ShapeDtypeStruct((t, GDN_WIDTH), F32)
    outs = (gd, gd, gd) + tuple(jax.ShapeDtypeStruct((t, w), dt) for w, dt in IN_SEGMENTS)
    row = lambda w: pl.BlockSpec((tm, w), lambda i: (i, 0))
    return pl.pallas_call(
        functools.partial(_in_proj_kernel, tm=tm, seq_len=seq_len),
        out_shape=outs,
        grid=(t // tm,),
        in_specs=[row(D_MODEL),
                  pl.BlockSpec((CONV_HALO, D_MODEL), lambda i: (jnp.maximum(i * hb - 1, 0), 0)),
                  pl.BlockSpec((CONV_HALO, D_MODEL), lambda i: (jnp.minimum((i + 1) * hb, last), 0)),
                  pl.BlockSpec((1, D_MODEL), lambda i: (0, 0)),
                  pl.BlockSpec((D_MODEL, IN_COLS), lambda i: (0, 0), pipeline_mode=pl.Buffered(1)),
                  pl.BlockSpec((CONV_K, CONV_COLS), lambda i: (0, 0))],
        out_specs=(row(GDN_WIDTH), row(GDN_WIDTH), row(GDN_WIDTH)) + tuple(row(w) for w, _ in IN_SEGMENTS),
        scratch_shapes=[pltpu.VMEM((tm + 2 * CONV_HALO, CONV_COLS), F32)],
        compiler_params=_params("parallel"),
        name="in_proj",
    )(x2d, x2d, x2d, norm_w.reshape(1, D_MODEL), w_perm, conv_w.astype(F32))


def _permute_w_in(w):
    parts = jnp.split(w, np.cumsum(IN_SIZES)[:-1], axis=-1)
    (na_q, na_k, na_v, ml_q, ml_k, ml_v, ml_o, ml_i, ml_f, gd_q, gd_k, gd_v, gd_z, gd_b, gd_a) = parts
    pad = jnp.zeros(w.shape[:-1] + (GATE_LANES - GATE_USED,), w.dtype)
    return jnp.concatenate([gd_q, gd_k, gd_v, na_q, na_k, na_v, ml_q, ml_k, ml_v, ml_o, gd_z,
                            gd_b, gd_a, ml_i, ml_f, pad], axis=-1).astype(BF16)


NA_BIAS_ROWS = 96


def _na_bias_kernel(rb_ref, o_ref):
    n = GRID_W * GRID_W
    dc = lax.broadcasted_iota(jnp.int32, (GATE_LANES, n), 0)
    col = lax.broadcasted_iota(jnp.int32, (GATE_LANES, n), 1)
    q, kc = col // GRID_W, col % GRID_W
    c0 = jnp.clip(q - NA_WIN_COLS // 2, 0, GRID_W - NA_WIN_COLS)
    valid = (kc >= c0) & (kc < c0 + NA_WIN_COLS)
    onehot = jnp.where(valid & (kc - q + (NA_WIN_COLS - 1) == dc), 1.0, 0.0).astype(BF16)
    o_ref[...] = jnp.where(valid[0:1, :], _exact_right(rb_ref[...], onehot), NEG)


def _na_bias_table(rel_bias):
    nh, ndr, ndc = rel_bias.shape
    rb = jnp.zeros((NA_BIAS_ROWS, GATE_LANES), F32).at[:nh * ndr, :ndc].set(
        rel_bias.astype(F32).reshape(nh * ndr, ndc))
    band = pl.pallas_call(
        _na_bias_kernel,
        out_shape=jax.ShapeDtypeStruct((NA_BIAS_ROWS, GRID_W * GRID_W), F32),
        name="na_bias_expand",
    )(rb)
    band = band[:nh * ndr].reshape(nh, ndr, GRID_W, GRID_W)
    first = np.clip(np.arange(ndr + 1) - 1, 0, ndr - 1)
    second = np.clip(np.arange(ndr + 1), 0, ndr - 1)
    tab = jnp.concatenate([band[:, first], band[:, second]], axis=-1)
    tab = tab.reshape(NA_HEADS // 2, 2, ndr + 1, GRID_W, PAIR)
    return jnp.moveaxis(tab, 1, 2).reshape(NA_HEADS // 2, ndr + 1, 2 * GRID_W, PAIR)


NA_ITEM_ROWS = 4
NA_KEY_ROWS = NA_ITEM_ROWS + NA_WIN_ROWS


NA_INNER_BLOCKS = NA_KEY_ROWS // 2 - 1


def _na_inner_bias(bias_tab):
    half = NA_WIN_ROWS // 2
    rows = []
    for i in range(NA_ITEM_ROWS):
        blocks = []
        for jp in range(i // 2, i // 2 + NA_INNER_BLOCKS):
            ok_a, ok_b = 0 <= 2 * jp - i < NA_WIN_ROWS, 0 <= 2 * jp + 1 - i < NA_WIN_ROWS
            entry = min(max(2 * jp - i + half, 0), 2 * NA_WIN_ROWS - 1)
            pen = np.concatenate([np.full(GRID_W, 0.0 if ok_a else NEG), np.full(GRID_W, 0.0 if ok_b else NEG)])
            blocks.append(bias_tab[:, entry] + jnp.asarray(pen, F32))
        rows.append(jnp.concatenate(blocks, axis=-1))
    return jnp.stack(rows, axis=1)


def _na_kernel(q_ref, k_ref, v_ref, bias_ref, inner_ref, o_ref, *, rows, rows_per_step):
    j = pl.program_id(1)
    nkeys = NA_KEY_ROWS * GRID_W
    half = NA_WIN_ROWS // 2
    lane = lax.broadcasted_iota(jnp.int32, (GRID_W, PAIR), 1)
    lane_blk = lax.broadcasted_iota(jnp.int32, (PAIR, PAIR), 1)

    def run(inner):
        items = []
        for g in range(rows_per_step // NA_ITEM_ROWS):
            r_first = j * rows_per_step + g * NA_ITEM_ROWS
            kr0 = jnp.clip(r_first - half, 0, rows - NA_KEY_ROWS)
            koff = pl.multiple_of(kr0 * GRID_W, GRID_W)
            entry, pen = {}, {}
            if not inner:
                for i in range(NA_ITEM_ROWS):
                    r = r_first + i
                    r0 = jnp.clip(r - half, 0, rows - NA_WIN_ROWS)
                    for jp in range(NA_KEY_ROWS // 2):
                        key_a = kr0 + 2 * jp
                        ok_a = (key_a >= r0) & (key_a < r0 + NA_WIN_ROWS)
                        ok_b = (key_a + 1 >= r0) & (key_a + 1 < r0 + NA_WIN_ROWS)
                        entry[i, jp] = jnp.clip(key_a - r + (NA_WIN_ROWS - 1), -1, 2 * NA_WIN_ROWS - 2) + 1
                        pen[i, jp] = jnp.where(lane_blk < GRID_W, jnp.where(ok_a, 0.0, NEG),
                                               jnp.where(ok_b, 0.0, NEG))
            for p in range(NA_HEADS // 2):
                items.append(dict(g=g, p=p, koff=koff, entry=entry, pen=pen, cols=slice(p * PAIR, (p + 1) * PAIR)))
        for it in items:
            q0 = it["g"] * NA_ITEM_ROWS * GRID_W
            q_lhs = jnp.concatenate(
                [_stack_pair(q_ref[q0 + i * GRID_W:q0 + (i + 1) * GRID_W, it["cols"]])
                 for i in range(NA_ITEM_ROWS)], axis=0)
            q_lhs = (q_lhs.astype(F32) * (HEAD_DIM ** -0.5)).astype(BF16)
            it["s"] = _dot_nt(q_lhs, k_ref[pl.ds(it["koff"], nkeys), it["cols"]])
        for it in items:
            s = it.pop("s")
            blocks = []
            for i in range(NA_ITEM_ROWS):
                if inner:
                    b0 = (i // 2) * PAIR
                    blocks.append(s[i * PAIR:(i + 1) * PAIR, b0:b0 + NA_INNER_BLOCKS * PAIR] + inner_ref[it["p"], i])
                else:
                    blocks.append(jnp.concatenate(
                        [s[i * PAIR:(i + 1) * PAIR, jp * PAIR:(jp + 1) * PAIR]
                         + bias_ref[it["p"], it["entry"][i, jp]] + it["pen"][i, jp]
                         for jp in range(NA_KEY_ROWS // 2)], axis=1))
            s = jnp.concatenate(blocks, axis=0)
            e = jnp.exp(s - jnp.max(s, axis=-1, keepdims=True))
            it["l"] = jnp.sum(e, axis=-1, keepdims=True)
            e = e.astype(BF16)
            if inner:
                zero = jnp.zeros((PAIR, PAIR), BF16)
                e = jnp.concatenate(
                    [jnp.concatenate(([zero] if i // 2 else []) + [e[i * PAIR:(i + 1) * PAIR]]
                                     + ([] if i // 2 else [zero]), axis=1) for i in range(NA_ITEM_ROWS)], axis=0)
            it["o"] = _dot(e, v_ref[pl.ds(it["koff"], nkeys), it["cols"]])
        for it in items:
            o = it.pop("o") * (1.0 / it.pop("l"))
            q0 = it["g"] * NA_ITEM_ROWS * GRID_W
            for i in range(NA_ITEM_ROWS):
                blk = o[i * PAIR:(i + 1) * PAIR]
                o_ref[q0 + i * GRID_W:q0 + (i + 1) * GRID_W, it["cols"]] = jnp.where(
                    lane < HEAD_DIM, blk[:GRID_W], blk[GRID_W:]).astype(o_ref.dtype)

    first_row = j * rows_per_step
    inner = (first_row >= half) & (first_row + rows_per_step - NA_ITEM_ROWS <= rows - NA_WIN_ROWS)

    @pl.when(inner)
    def _():
        run(True)

    @pl.when(jnp.logical_not(inner))
    def _():
        run(False)


def _neighbourhood_attention(q, k, v, bias_tab):
    b, s, _ = q.shape
    rows = s // GRID_W
    assert rows >= NA_KEY_ROWS
    rps = min(NA_ROWS_PER_STEP, rows)
    tq = rps * GRID_W
    inner_tab = _na_inner_bias(bias_tab)
    return pl.pallas_call(
        functools.partial(_na_kernel, rows=rows, rows_per_step=rps),
        out_shape=jax.ShapeDtypeStruct((b, s, NA_WIDTH), BF16),
        grid=(b, rows // rps),
        in_specs=[pl.BlockSpec((None, tq, NA_WIDTH), lambda bi, j: (bi, j, 0)),
                  pl.BlockSpec((None, s, NA_WIDTH), lambda bi, j: (bi, 0, 0)),
                  pl.BlockSpec((None, s, NA_WIDTH), lambda bi, j: (bi, 0, 0)),
                  pl.BlockSpec(bias_tab.shape, lambda bi, j: (0, 0, 0, 0)),
                  pl.BlockSpec(inner_tab.shape, lambda bi, j: (0, 0, 0, 0))],
        out_specs=pl.BlockSpec((None, tq, NA_WIDTH), lambda bi, j: (bi, j, 0)),
        compiler_params=_params("parallel", "arbitrary"),
        name="neighbourhood_attention",
    )(q, k, v, bias_tab, inner_tab)


def _gate_rows(gates):
    b, s, _ = gates.shape
    g = gates[:, :, :GATE_USED].reshape(b, s // CHUNK, CHUNK, GATE_USED // 2, 2)
    return jnp.transpose(g, (0, 3, 1, 4, 2)).reshape(b, GATE_USED // 2, s // CHUNK, PAIR)


def _gate_row_spec(lane0, heads, d, chunks, chunk_block):
    pairs = heads // 2
    unit_block = (lane0 // 2 + d * pairs) // pairs
    assert unit_block * pairs == lane0 // 2 + d * pairs
    return pl.BlockSpec((None, pairs, chunks, PAIR), lambda bi, j: (bi, unit_block, chunk_block(j), 0))


def _param_rows(p, heads):
    return jnp.repeat(p.astype(F32).reshape(N_DIR, heads // 2, 2), HEAD_DIM, axis=-1).reshape(
        N_DIR, heads // 2, 1, PAIR)


def _pair_cumsum_matrix(rev):
    r = lax.broadcasted_iota(jnp.int32, (PAIR, PAIR), 0)
    c = lax.broadcasted_iota(jnp.int32, (PAIR, PAIR), 1)
    same = (r // HEAD_DIM) == (c // HEAD_DIM)
    sp, s = r % HEAD_DIM, c % HEAD_DIM
    return jnp.where(same & ((sp >= s) if rev else (sp <= s)), 1.0, 0.0).astype(BF16)


ML_PAIRS = ML_HEADS // 2
ML_UNITS = N_DIR * ML_PAIRS
ML_STEPS_PER_TRIP = 8
OS_ROWS = 16


def _outer_sum_operands(col_term, row_term):
    ones = jnp.ones((3, PAIR), BF16)
    zeros = jnp.zeros((OS_ROWS - 6, PAIR), BF16)
    a = jnp.concatenate(list(_split3(col_term)) + [ones, zeros], axis=0)
    b = jnp.concatenate([ones] + list(_split3(row_term)) + [zeros], axis=0)
    return a, b


def _mlstm_kernel(qf_ref, kf_ref, vf_ref, if_ref, ff_ref,
                  qb_ref, kb_ref, vb_ref, ib_ref, fb_ref,
                  ibr_ref, fbr_ref,
                  hf_ref, hb_ref,
                  c_scr, m_scr, cs_scr, rt_scr, ai_scr, em_scr, we_scr, a_scr, g_scr, vt_scr, ut_scr, *, chunks):
    j = pl.program_id(1)

    @pl.when(j == 0)
    def _():
        c_scr[...] = jnp.zeros_like(c_scr)
        m_scr[...] = jnp.zeros_like(m_scr)

    lane = lax.broadcasted_iota(jnp.int32, (chunks, PAIR), 1)
    pos = lane % HEAD_DIM
    lo = lane < HEAD_DIM
    for d, (i_ref, f_ref) in enumerate(((if_ref, ff_ref), (ib_ref, fb_ref))):
        rev = bool(d)
        ucum = _pair_cumsum_matrix(rev)
        last = 0 if rev else CHUNK - 1
        for p in range(ML_PAIRS):
            u = d * ML_PAIRS + p
            logf = -_softplus(-(f_ref[p] + fbr_ref[d, p]))
            bcum = _exact_right(logf, ucum)
            cs = i_ref[p] + ibr_ref[d, p] - bcum
            bl = jnp.where(lo, bcum[:, last:last + 1], bcum[:, HEAD_DIM + last:HEAD_DIM + last + 1])
            cm = cs
            for k in (1, 2, 4, 8, 16, 32):
                if rev:
                    cm = jnp.where(pos < HEAD_DIM - k, jnp.maximum(cm, pltpu.roll(cm, PAIR - k, axis=1)), cm)
                else:
                    cm = jnp.where(pos >= k, jnp.maximum(cm, pltpu.roll(cm, k, axis=1)), cm)
            w = bl + cs
            m_loc = jnp.where(lo, jnp.max(jnp.where(lo, w, NEG), axis=-1, keepdims=True),
                              jnp.max(jnp.where(lo, NEG, w), axis=-1, keepdims=True))
            m = m_scr[u]
            m_prev, a_rows, g_rows = [None] * chunks, [None] * chunks, [None] * chunks
            for c in (range(chunks - 1, -1, -1) if rev else range(chunks)):
                m_prev[c] = m
                m_new = jnp.maximum(bl[c:c + 1] + m, m_loc[c:c + 1])
                a_rows[c] = jnp.exp(bl[c:c + 1] + m - m_new)
                g_rows[c] = jnp.exp(m_loc[c:c + 1] - m_new)
                m = m_new
            m_scr[u] = m
            inter = bcum + jnp.concatenate(m_prev, axis=0)
            m_t = jnp.maximum(bcum + cm, inter)
            cs_scr[u] = cs
            rt_scr[u] = bcum - m_t
            ai_scr[u] = jnp.exp(inter - m_t)
            em_scr[u] = jnp.exp(-m_t)
            we_scr[u] = jnp.exp(w - m_loc)
            a_scr[u] = jnp.concatenate(a_rows, axis=0)
            g_scr[u] = jnp.concatenate(g_rows, axis=0)

    dirs = ((qf_ref, kf_ref, vf_ref, hf_ref), (qb_ref, kb_ref, vb_ref, hb_ref))
    units = [(d, p) for d in range(N_DIR) for p in range(ML_PAIRS)]
    lane64 = lax.broadcasted_iota(jnp.int32, (CHUNK, PAIR), 1)
    one_hi = jnp.where(lane64 == HEAD_DIM, 1.0, 0.0).astype(BF16)
    one_lo = jnp.where(lane64 == 0, 1.0, 0.0).astype(BF16)
    lane_row = lax.broadcasted_iota(jnp.int32, (1, PAIR), 1)

    def key_tile(d, p, r0):
        k_pair = _stack_pair(dirs[d][1][pl.ds(r0, CHUNK), p * PAIR:(p + 1) * PAIR])
        return (k_pair.astype(F32) * (HEAD_DIM ** -0.5)).astype(BF16)

    def contributions(t):
        for k in range(ML_STEPS_PER_TRIP):
            step = t * ML_STEPS_PER_TRIP + k
            for u, (d, p) in enumerate(units):
                c = (chunks - 1 - step) if d else step
                r0 = _aligned(c * CHUNK, CHUNK)
                v2 = dirs[d][2][pl.ds(r0, CHUNK), p * PAIR:(p + 1) * PAIR]
                v_ext = jnp.concatenate([jnp.where(lane64 < HEAD_DIM, v2, one_hi),
                                         jnp.where(lane64 >= HEAD_DIM, v2, one_lo)], axis=0)
                v_t = v_ext.astype(F32).T
                vt_scr[u, c] = v_t.astype(BF16)
                ut_scr[u, c] = _dot((v_t * we_scr[u, pl.ds(c, 1), :]).astype(BF16), key_tile(d, p, r0))
            yield

    def outputs(t):
        masks = [_pair_masks(True)[0], _pair_masks(False)[0]]
        items = []
        for u, (d, p) in enumerate(units):
            ct = c_scr[u]
            for k in range(ML_STEPS_PER_TRIP):
                step = t * ML_STEPS_PER_TRIP + k
                c = (chunks - 1 - step) if d else step
                r0 = _aligned(c * CHUNK, CHUNK)
                row = pl.ds(c, 1)
                items.append(dict(u=u, d=d, p=p, c=c, r0=r0, row=row, ct=ct.astype(BF16)))
                ct = a_scr[u, row, :] * ct + g_scr[u, row, :] * ut_scr[u, c]
            c_scr[u] = ct
        yield
        for n, it in enumerate(items):
            u, d, p, r0, row = it["u"], it["d"], it["p"], it["r0"], it["row"]
            q_pair = _stack_pair(dirs[d][0][pl.ds(r0, CHUNK), p * PAIR:(p + 1) * PAIR])
            os_a, os_b = _outer_sum_operands(cs_scr[u, row, :], rt_scr[u, row, :])
            it["gram"] = _dot_nt(key_tile(d, p, r0), q_pair)
            it["osum"] = _dot_tn(os_a, os_b)
            it["st"] = _dot_nt(it.pop("ct"), q_pair)
            if n % ML_UNITS == ML_UNITS - 1:
                yield
        for n, it in enumerate(items):
            u, c = it["u"], it["c"]
            s_t = it.pop("gram") * jnp.exp(jnp.where(masks[it["d"]], it.pop("osum"), NEG))
            it["intra"] = jnp.sum(s_t, axis=0, keepdims=True)
            it["num"] = _dot(vt_scr[u, c], s_t.astype(BF16))
            if n % ML_UNITS == ML_UNITS - 1:
                yield
        for n, it in enumerate(items):
            u, d, p, row = it["u"], it["d"], it["p"], it["row"]
            a_inter = ai_scr[u, row, :]
            st = it.pop("st")
            den = a_inter * jnp.where(lane_row < HEAD_DIM, st[HEAD_DIM:HEAD_DIM + 1], st[0:1]) + it.pop("intra")
            out_t = (it.pop("num") + a_inter * st) * (1.0 / jnp.maximum(jnp.abs(den), em_scr[u, row, :]))
            out = out_t.T
            dirs[d][3][pl.ds(it["r0"], CHUNK), p * PAIR:(p + 1) * PAIR] = jnp.where(
                lane64 < HEAD_DIM, out[:CHUNK], out[CHUNK:])
            if n % ML_UNITS == ML_UNITS - 1:
                yield

    trips = chunks // ML_STEPS_PER_TRIP

    def fused(t, carry):
        filler = contributions(t)
        for _ in outputs(t - 1):
            next(filler, None)
        for _ in filler:
            pass
        return carry

    for _ in contributions(0):
        pass
    lax.fori_loop(1, trips, fused, 0)
    for _ in outputs(trips - 1):
        pass


def _mlstm(q, k, v, gate_rows, i_bias, f_bias):
    b, s, _ = q.shape
    tb = min(SEQ_BLOCK, s)
    nb = s // tb
    chunks = tb // CHUNK
    assert s % tb == 0 and chunks % ML_STEPS_PER_TRIP == 0
    ibr = _param_rows(i_bias, ML_HEADS)
    fbr = _param_rows(f_bias, ML_HEADS)

    def fwd(bi, j):
        return (bi, j, 0)

    def bwd(bi, j):
        return (bi, nb - 1 - j, 0)

    seq = lambda w, im: pl.BlockSpec((None, tb, w), im)
    rows = lambda lane0, d: _gate_row_spec(lane0, ML_HEADS, d, chunks, (lambda j: nb - 1 - j) if d else (lambda j: j))
    full = lambda a: pl.BlockSpec(a.shape, lambda bi, j: (0,) * a.ndim)
    out = jax.ShapeDtypeStruct((b, s, ML_WIDTH), F32)
    rows_scr = pltpu.VMEM((ML_UNITS, chunks, PAIR), F32)
    return pl.pallas_call(
        functools.partial(_mlstm_kernel, chunks=chunks),
        out_shape=(out, out),
        grid=(b, nb),
        in_specs=[seq(ML_WIDTH, fwd), seq(ML_WIDTH, fwd), seq(ML_WIDTH, fwd), rows(GATE_ML_I, 0), rows(GATE_ML_F, 0),
                  seq(ML_WIDTH, bwd), seq(ML_WIDTH, bwd), seq(ML_WIDTH, bwd), rows(GATE_ML_I, 1), rows(GATE_ML_F, 1),
                  full(ibr), full(fbr)],
        out_specs=(seq(ML_WIDTH, fwd), seq(ML_WIDTH, bwd)),
        scratch_shapes=[pltpu.VMEM((ML_UNITS, PAIR, PAIR), F32),
                        pltpu.VMEM((ML_UNITS, 1, PAIR), F32),
                        rows_scr, rows_scr, rows_scr, rows_scr, rows_scr, rows_scr, rows_scr,
                        pltpu.VMEM((ML_UNITS, chunks, PAIR, PAIR), BF16),
                        pltpu.VMEM((ML_UNITS, chunks, PAIR, PAIR), F32)],
        compiler_params=_params("parallel", "arbitrary"),
        name="mlstm",
    )(q, k, v, gate_rows, gate_rows, q, k, v, gate_rows, gate_rows, ibr, fbr)


GD_CHUNKS_PER_TRIP = 2
GD_INV_BLOCK = 16
GD_UNITS = N_DIR * GD_PAIRS


def _gdn_kernel(qf_ref, kf_ref, vf_ref, af_ref, bf_ref,
                qb_ref, kb_ref, vb_ref, ab_ref, bb_ref,
                alr_ref, dtr_ref,
                of_ref, ob_ref,
                s_scr, gc_scr, beta_scr, egc_scr, qsc_scr, fb_scr, eg_scr,
                u_scr, wq_scr, at_scr, ks_scr, *, chunks):
    j = pl.program_id(1)

    @pl.when(j == 0)
    def _():
        s_scr[...] = jnp.zeros_like(s_scr)

    lane_row = lax.broadcasted_iota(jnp.int32, (chunks, PAIR), 1)
    for d, (a_ref, b_ref) in enumerate(((af_ref, bf_ref), (ab_ref, bb_ref))):
        ucum = _pair_cumsum_matrix(rev=bool(d))
        last = 0 if d else CHUNK - 1
        for p in range(GD_PAIRS):
            u = d * GD_PAIRS + p
            g = -jnp.exp(alr_ref[d, p]) * _softplus(a_ref[p] + dtr_ref[d, p])
            gc = _exact_right(g, ucum)
            beta = _sigmoid(b_ref[p])
            g_last = jnp.where(lane_row < HEAD_DIM, gc[:, last:last + 1],
                               gc[:, HEAD_DIM + last:HEAD_DIM + last + 1])
            gc_scr[u] = gc
            beta_scr[u] = beta
            egc_scr[u] = jnp.exp(gc)
            qsc_scr[u] = jnp.exp(gc) * (HEAD_DIM ** -0.5)
            fb_scr[u] = jnp.exp(g_last - gc) * beta
            eg_scr[u] = jnp.exp(g_last)

    r = lax.broadcasted_iota(jnp.int32, (PAIR, PAIR), 0)
    cidx = lax.broadcasted_iota(jnp.int32, (PAIR, PAIR), 1)
    is_diag = r == cidx
    same16 = (r // GD_INV_BLOCK) == (cidx // GD_INV_BLOCK)
    eye_s = jnp.where(lax.broadcasted_iota(jnp.int32, (GD_INV_BLOCK, PAIR), 0)
                      == lax.broadcasted_iota(jnp.int32, (GD_INV_BLOCK, PAIR), 1) % GD_INV_BLOCK, 1.0, 0.0)
    dirs = ((qf_ref, kf_ref, vf_ref, of_ref), (qb_ref, kb_ref, vb_ref, ob_ref))
    units = [(d, p) for d in range(N_DIR) for p in range(GD_PAIRS)]

    def strip_of(bd):
        s = bd[0:GD_INV_BLOCK]
        for b in range(1, PAIR // GD_INV_BLOCK):
            s = s + bd[b * GD_INV_BLOCK:(b + 1) * GD_INV_BLOCK]
        return s

    def blockdiag_of(s):
        return jnp.where(same16, jnp.concatenate([s] * (PAIR // GD_INV_BLOCK), axis=0), jnp.zeros((), s.dtype))

    def precompute(t):
        masks = [_pair_masks(False), _pair_masks(True)]
        items = []
        for cc in range(GD_CHUNKS_PER_TRIP):
            step = t * GD_CHUNKS_PER_TRIP + cc
            for d in range(N_DIR):
                c = (chunks - 1 - step) if d else step
                for p in range(GD_PAIRS):
                    items.append(dict(d=d, p=p, c=c, r0=_aligned(c * CHUNK, CHUNK), row=pl.ds(c, 1),
                                      u=d * GD_PAIRS + p, cols=slice(p * PAIR, (p + 1) * PAIR)))
        for it in items:
            u, row = it["u"], it["row"]
            q_ref, k_ref = dirs[it["d"]][0], dirs[it["d"]][1]
            q_b = _stack_pair(q_ref[pl.ds(it["r0"], CHUNK), it["cols"]]).astype(BF16)
            k_b = _stack_pair(k_ref[pl.ds(it["r0"], CHUNK), it["cols"]]).astype(BF16)
            it["grams"] = _dot_nt(jnp.concatenate([q_b, k_b], axis=0), k_b)
            gc = gc_scr[u, row, :]
            it["osum"] = _dot_tn(*_outer_sum_operands(gc, -gc))
        yield
        for it in items:
            u, row = it["u"], it["row"]
            valid, strict = masks[it["d"]]
            decay = jnp.exp(jnp.where(valid, it.pop("osum"), NEG)) * beta_scr[u, row, :]
            grams = it.pop("grams")
            attn = (grams[:PAIR] * (HEAD_DIM ** -0.5) * decay).astype(BF16)
            q_scale = jnp.where(is_diag, qsc_scr[u, row, :], 0.0).astype(BF16)
            at_scr[u, it["c"]] = jnp.concatenate([attn, q_scale], axis=1)
            neg_a = jnp.where(strict, -(grams[PAIR:] * decay), 0.0)
            nd = jnp.where(same16, neg_a, 0.0)
            it["nl"] = (neg_a - nd).astype(BF16)
            nd_s = strip_of(nd)
            it["t_s"] = eye_s + nd_s
            it["pw_s"] = _dot(nd_s.astype(BF16), nd.astype(BF16))
        yield
        for _ in range(2):
            for it in items:
                pw_s = it["pw_s"].astype(BF16)
                both = _dot(jnp.concatenate([it["t_s"].astype(BF16), pw_s], axis=0), blockdiag_of(pw_s))
                it["t_s"] = it["t_s"] + both[:GD_INV_BLOCK]
                it["pw_s"] = both[GD_INV_BLOCK:]
        yield
        for it in items:
            pw_s = it.pop("pw_s").astype(BF16)
            t_s = it.pop("t_s")
            it["t_inv"] = blockdiag_of(t_s + _dot(t_s.astype(BF16), blockdiag_of(pw_s)))
        yield
        for it in items:
            it["x_b"] = it["t_inv"].astype(BF16)
            it["m"] = _dot(it["x_b"], it.pop("nl"))
        yield
        for it in items:
            m_b = it.pop("m").astype(BF16)
            both = _dot(m_b, jnp.concatenate([m_b, it.pop("x_b")], axis=1))
            it["m2"] = both[:, :PAIR]
            it["t_inv"] = it["t_inv"] + both[:, PAIR:]
        yield
        for it in items:
            it["t_inv"] = it["t_inv"] + _dot(it.pop("m2").astype(BF16), it["t_inv"].astype(BF16))
        yield
        for it in items:
            u, c, row = it["u"], it["c"], it["row"]
            q_ref, k_ref, v_ref = dirs[it["d"]][:3]
            rows = pl.ds(it["r0"], CHUNK)
            q_b = _stack_pair(q_ref[rows, it["cols"]]).astype(BF16)
            k_pair = _stack_pair(k_ref[rows, it["cols"]])
            v_b = _stack_pair(v_ref[rows, it["cols"]]).astype(BF16)
            t_inv = it.pop("t_inv")
            u_scr[u, c] = _dot(t_inv.astype(BF16), v_b)
            w = _dot((t_inv * egc_scr[u, row, :]).astype(BF16), k_pair.astype(BF16))
            wq_scr[u, c] = jnp.concatenate([w.astype(BF16), q_b], axis=0)
            ks_scr[u, c] = (k_pair.T * fb_scr[u, row, :]).astype(BF16)

    def scan(i):
        cs = (i, chunks - 1 - i)
        state = [s_scr[u] for u in range(len(units))]
        ws = [_dot(wq_scr[u, cs[d]], state[u].astype(BF16)) for u, (d, p) in enumerate(units)]
        yield
        v_qs = [jnp.concatenate([u_scr[u, cs[d]] - ws[u][:PAIR], ws[u][PAIR:]], axis=0).astype(BF16)
                for u, (d, p) in enumerate(units)]
        for u, (d, p) in enumerate(units):
            s_scr[u] = state[u] * eg_scr[u, pl.ds(cs[d], 1), :] + _dot(ks_scr[u, cs[d]], v_qs[u][:PAIR])
        for u, (d, p) in enumerate(units):
            o = _dot(at_scr[u, cs[d]], v_qs[u])
            r0 = _aligned(cs[d] * CHUNK, CHUNK)
            dirs[d][3][pl.ds(r0, CHUNK), p * PAIR:(p + 1) * PAIR] = o[:CHUNK] + o[CHUNK:]

    trips = chunks // GD_CHUNKS_PER_TRIP

    def scan_trip(t):
        for cc in range(GD_CHUNKS_PER_TRIP):
            yield from scan(t * GD_CHUNKS_PER_TRIP + cc)
            yield

    def fused(t, carry):
        filler = scan_trip(t - 1)
        for _ in precompute(t):
            next(filler, None)
        for _ in filler:
            pass
        return carry

    for _ in precompute(0):
        pass
    lax.fori_loop(1, trips, fused, 0)
    for _ in scan_trip(trips - 1):
        pass


def _gdn(q, k, v, gate_rows, a_log, dt_bias):
    b, s, _ = q.shape
    tb = min(GDN_SEQ_BLOCK, s)
    nb = s // tb
    chunks = tb // CHUNK
    assert s % tb == 0 and chunks % GD_CHUNKS_PER_TRIP == 0
    alr = _param_rows(a_log, GDN_HEADS)
    dtr = _param_rows(dt_bias, GDN_HEADS)

    def fwd(bi, j):
        return (bi, j, 0)

    def bwd(bi, j):
        return (bi, nb - 1 - j, 0)

    seq = lambda w, im: pl.BlockSpec((None, tb, w), im)
    rows = lambda lane0, d: _gate_row_spec(lane0, GDN_HEADS, d, chunks, (lambda j: nb - 1 - j) if d else (lambda j: j))
    full = lambda a: pl.BlockSpec(a.shape, lambda bi, j: (0,) * a.ndim)
    out = jax.ShapeDtypeStruct((b, s, GDN_WIDTH), F32)
    rows_scr = pltpu.VMEM((GD_UNITS, chunks, PAIR), F32)
    return pl.pallas_call(
        functools.partial(_gdn_kernel, chunks=chunks),
        out_shape=(out, out),
        grid=(b, nb),
        in_specs=[seq(GDN_WIDTH, fwd), seq(GDN_WIDTH, fwd), seq(GDN_WIDTH, fwd), rows(GATE_GD_A, 0), rows(GATE_GD_B, 0),
                  seq(GDN_WIDTH, bwd), seq(GDN_WIDTH, bwd), seq(GDN_WIDTH, bwd), rows(GATE_GD_A, 1), rows(GATE_GD_B, 1),
                  full(alr), full(dtr)],
        out_specs=(seq(GDN_WIDTH, fwd), seq(GDN_WIDTH, bwd)),
        scratch_shapes=[pltpu.VMEM((GD_UNITS, PAIR, PAIR), F32),
                        rows_scr, rows_scr, rows_scr, rows_scr, rows_scr, rows_scr,
                        pltpu.VMEM((GD_UNITS, chunks, PAIR, PAIR), F32),
                        pltpu.VMEM((GD_UNITS, chunks, 2 * PAIR, PAIR), BF16),
                        pltpu.VMEM((GD_UNITS, chunks, PAIR, 2 * PAIR), BF16),
                        pltpu.VMEM((GD_UNITS, chunks, PAIR, PAIR), BF16)],
        compiler_params=_params("parallel", "arbitrary"),
        name="gated_deltanet",
    )(q, k, v, gate_rows, gate_rows, q, k, v, gate_rows, gate_rows, alr, dtr)


def _out_proj_kernel(x_ref, na_ref, hf_ref, hb_ref, mo_ref, mw_ref, of_ref, ob_ref, gz_ref, gw_ref,
                     w_ref, o_ref):
    hs = hf_ref[...] + hb_ref[...]
    y_ml = hs * lax.rsqrt(_head_sumsq(hs) * (1.0 / HEAD_DIM) + EPS) * mw_ref[...] * _sigmoid(mo_ref[...])
    os_ = of_ref[...] + ob_ref[...]
    z = gz_ref[...]
    y_gd = os_ * lax.rsqrt(_head_sumsq(os_) * (1.0 / HEAD_DIM) + EPS) * gw_ref[...] * (z * _sigmoid(z))
    acc = x_ref[...] + _dot(na_ref[...], w_ref[0:NA_WIDTH, :])
    acc = acc + _dot(y_ml.astype(BF16), w_ref[NA_WIDTH:NA_WIDTH + ML_WIDTH, :])
    acc = acc + _dot(y_gd.astype(BF16), w_ref[NA_WIDTH + ML_WIDTH:, :])
    o_ref[...] = acc


def _out_proj(x2d, y_na, hf, hb, ml_o, ml_norm_w, of, ob, gd_z, gdn_norm_w, w_out):
    t = x2d.shape[0]
    tm = min(TM_ROWS, t)
    row = lambda w: pl.BlockSpec((tm, w), lambda i: (i, 0))
    const = lambda r, c: pl.BlockSpec((r, c), lambda i: (0, 0))
    return pl.pallas_call(
        _out_proj_kernel,
        out_shape=jax.ShapeDtypeStruct((t, D_MODEL), F32),
        grid=(t // tm,),
        in_specs=[row(D_MODEL), row(NA_WIDTH), row(ML_WIDTH), row(ML_WIDTH), row(ML_WIDTH), const(1, ML_WIDTH),
                  row(GDN_WIDTH), row(GDN_WIDTH), row(GDN_WIDTH), const(1, GDN_WIDTH),
                  const(D_MODEL, D_MODEL)],
        out_specs=row(D_MODEL),
        compiler_params=_params("parallel"),
        name="out_proj",
    )(x2d, y_na, hf, hb, ml_o, ml_norm_w.reshape(1, ML_WIDTH).astype(F32), of, ob, gd_z,
      gdn_norm_w.reshape(1, GDN_WIDTH).astype(F32), w_out.astype(BF16))


def _mem_kv_kernel(m_ref, nw_ref, w_ref, k_ref, v_ref):
    h = _rms(m_ref[...], nw_ref[...]).astype(BF16)
    k_ref[...] = _dot(h, w_ref[:, :D_MODEL]).astype(BF16)
    v_ref[...] = _dot(h, w_ref[:, D_MODEL:]).astype(BF16)


def _mem_kv(mem2d, norm_w, w_kv):
    t = mem2d.shape[0]
    tm = min(TM_ROWS, t)
    out = jax.ShapeDtypeStruct((t, D_MODEL), BF16)
    return pl.pallas_call(
        _mem_kv_kernel,
        out_shape=(out, out),
        grid=(t // tm,),
        in_specs=[pl.BlockSpec((tm, D_MODEL), lambda i: (i, 0)),
                  pl.BlockSpec((1, D_MODEL), lambda i: (0, 0)),
                  pl.BlockSpec((D_MODEL, 2 * D_MODEL), lambda i: (0, 0))],
        out_specs=(pl.BlockSpec((tm, D_MODEL), lambda i: (i, 0)), pl.BlockSpec((tm, D_MODEL), lambda i: (i, 0))),
        compiler_params=_params("parallel"),
        name="mem_kv",
    )(mem2d, norm_w.reshape(1, D_MODEL).astype(F32), w_kv.astype(BF16))


def _xattn_kernel(x_ref, nw_ref, wq_ref, k_ref, v_ref, wo_ref, o_ref):
    x = x_ref[...]
    q = _dot(_rms(x, nw_ref[...]).astype(BF16), wq_ref[...]).astype(BF16)
    heads = [slice(h * XA_HEAD_DIM, (h + 1) * XA_HEAD_DIM) for h in range(XA_HEADS)]
    s = [_dot_nt(q[:, cols], k_ref[:, cols]) * (XA_HEAD_DIM ** -0.5) for cols in heads]
    e = [jnp.exp(s_h - jnp.max(s_h, axis=-1, keepdims=True)) for s_h in s]
    inv_l = [1.0 / jnp.sum(e_h, axis=-1, keepdims=True) for e_h in e]
    o = [(_dot(e_h.astype(BF16), v_ref[:, cols]) * il).astype(BF16) for e_h, il, cols in zip(e, inv_l, heads)]
    o_ref[...] = x + _dot(jnp.concatenate(o, axis=1), wo_ref[...])


def _xattn(x3d, norm_w, w_q, k, v, w_o):
    b, s, _ = x3d.shape
    tm = min(TM_ROWS, s)
    n_mem = k.shape[1]
    return pl.pallas_call(
        _xattn_kernel,
        out_shape=jax.ShapeDtypeStruct(x3d.shape, F32),
        grid=(b, s // tm),
        in_specs=[pl.BlockSpec((None, tm, D_MODEL), lambda bi, i: (bi, i, 0)),
                  pl.BlockSpec((1, D_MODEL), lambda bi, i: (0, 0)),
                  pl.BlockSpec((D_MODEL, D_MODEL), lambda bi, i: (0, 0)),
                  pl.BlockSpec((None, n_mem, D_MODEL), lambda bi, i: (bi, 0, 0)),
                  pl.BlockSpec((None, n_mem, D_MODEL), lambda bi, i: (bi, 0, 0)),
                  pl.BlockSpec((D_MODEL, D_MODEL), lambda bi, i: (0, 0))],
        out_specs=pl.BlockSpec((None, tm, D_MODEL), lambda bi, i: (bi, i, 0)),
        compiler_params=_params("parallel", "parallel"),
        name="cross_attention",
    )(x3d, norm_w.reshape(1, D_MODEL).astype(F32), w_q.astype(BF16), k, v, w_o.astype(BF16))


FF_CHUNK = 512


def _ffn_kernel(x_ref, nw_ref, w1_ref, w2_ref, fw_ref, o_ref, *, final_norm):
    x = x_ref[...]
    h = _rms(x, nw_ref[...]).astype(BF16)
    acc = x
    for c0 in range(0, D_FF, FF_CHUNK):
        a = jnp.maximum(_dot(h, w1_ref[:, c0:c0 + FF_CHUNK]), 0.0)
        acc = acc + _dot((a * a).astype(BF16), w2_ref[c0:c0 + FF_CHUNK, :])
    o_ref[...] = _rms(acc, fw_ref[...]) if final_norm else acc


def _ffn(x2d, norm_w, w1, w2, final_w, final_norm):
    t = x2d.shape[0]
    tm = min(TM_ROWS, t)
    return pl.pallas_call(
        functools.partial(_ffn_kernel, final_norm=final_norm),
        out_shape=jax.ShapeDtypeStruct((t, D_MODEL), F32),
        grid=(t // tm,),
        in_specs=[pl.BlockSpec((tm, D_MODEL), lambda i: (i, 0)),
                  pl.BlockSpec((1, D_MODEL), lambda i: (0, 0)),
                  pl.BlockSpec((D_MODEL, D_FF), lambda i: (0, 0), pipeline_mode=pl.Buffered(1)),
                  pl.BlockSpec((D_FF, D_MODEL), lambda i: (0, 0), pipeline_mode=pl.Buffered(1)),
                  pl.BlockSpec((1, D_MODEL), lambda i: (0, 0))],
        out_specs=pl.BlockSpec((tm, D_MODEL), lambda i: (i, 0)),
        compiler_params=_params("parallel"),
        name="ffn",
    )(x2d, norm_w.reshape(1, D_MODEL).astype(F32), w1.astype(BF16), w2.astype(BF16),
      final_w.reshape(1, D_MODEL).astype(F32))


def kernel(x, mem, norm_mix_w, w_in, na_rel_bias, ml_i_bias, ml_f_bias, ml_norm_w, gdn_conv_w, gdn_a_log,
           gdn_dt_bias, gdn_norm_w, w_out, norm_xa_w, norm_mem_w, w_xq, w_xkv, w_xo, norm_ffn_w, w_ff1,
           w_ff2, norm_out_w):
    b, s, d = x.shape
    depth = w_in.shape[0]
    x2d = x.reshape(b * s, d).astype(F32)
    mem2d = mem.reshape(-1, d).astype(F32)
    w_in_p = _permute_w_in(w_in)
    w_out_b, w_xq_b, w_xkv_b, w_xo_b, w_ff1_b, w_ff2_b = (
        w.astype(BF16) for w in (w_out, w_xq, w_xkv, w_xo, w_ff1, w_ff2))
    seq = lambda a: a.reshape(b, s, a.shape[-1])
    flat = lambda a: a.reshape(b * s, a.shape[-1])
    n_mem = mem.shape[1]
    for l in range(depth):
        (gq, gk, gv, na_q, na_k, na_v, ml_q, ml_k, ml_v, ml_o, gd_z, gates) = _in_proj(
            x2d, norm_mix_w[l].astype(F32), w_in_p[l], gdn_conv_w[l], s)
        gate_rows = _gate_rows(seq(gates))
        y_na = _neighbourhood_attention(seq(na_q), seq(na_k), seq(na_v), _na_bias_table(na_rel_bias[l]))
        hf, hb = _mlstm(seq(ml_q), seq(ml_k), seq(ml_v), gate_rows, ml_i_bias[l], ml_f_bias[l])
        of, ob = _gdn(seq(gq), seq(gk), seq(gv), gate_rows, gdn_a_log[l], gdn_dt_bias[l])
        x2d = _out_proj(x2d, flat(y_na), flat(hf), flat(hb), ml_o, ml_norm_w[l], flat(of), flat(ob), gd_z,
                        gdn_norm_w[l], w_out_b[l])
        mk, mv = _mem_kv(mem2d, norm_mem_w[l], w_xkv_b[l])
        x2d = _xattn(x2d.reshape(b, s, d), norm_xa_w[l], w_xq_b[l], mk.reshape(b, n_mem, d),
                     mv.reshape(b, n_mem, d), w_xo_b[l]).reshape(b * s, d)
        x2d = _ffn(x2d, norm_ffn_w[l], w_ff1_b[l], w_ff2_b[l], norm_out_w, final_norm=(l == depth - 1))
    return x2d.reshape(b, s, d).astype(x.dtype)
```

```python
import functools

import numpy as np
import jax
import jax.numpy as jnp
from jax import lax
from jax.experimental import pallas as pl
from jax.experimental.pallas import tpu as pltpu

F32 = jnp.float32
BF16 = jnp.bfloat16

D_MODEL = 1024
HEAD_DIM = 64
GRID_W = 64
NA_HEADS = 6
NA_WIN_ROWS = 8
NA_WIN_COLS = 16
ML_HEADS = 4
GDN_HEADS = 6
CONV_K = 5
N_DIR = 2
XA_HEADS = 4
XA_HEAD_DIM = D_MODEL // XA_HEADS
D_FF = 4 * D_MODEL
NA_WIDTH = NA_HEADS * HEAD_DIM
ML_WIDTH = ML_HEADS * HEAD_DIM
GDN_WIDTH = GDN_HEADS * HEAD_DIM
EPS = 1e-6
CHUNK = 64
PAIR = 2 * HEAD_DIM
NEG = -1e30

IN_SIZES = (NA_WIDTH, NA_WIDTH, NA_WIDTH,
            ML_WIDTH, ML_WIDTH, ML_WIDTH, ML_WIDTH, N_DIR * ML_HEADS, N_DIR * ML_HEADS,
            GDN_WIDTH, GDN_WIDTH, GDN_WIDTH, GDN_WIDTH, N_DIR * GDN_HEADS, N_DIR * GDN_HEADS)

GATE_GD_B = 0
GATE_GD_A = GATE_GD_B + N_DIR * GDN_HEADS
GATE_ML_I = GATE_GD_A + N_DIR * GDN_HEADS
GATE_ML_F = GATE_ML_I + N_DIR * ML_HEADS
GATE_USED = GATE_ML_F + N_DIR * ML_HEADS
GATE_LANES = 128

V7X_VMEM_LIMIT = 56 * 1024 * 1024
MXU_COLS = 256

TM_ROWS = 1024
NA_ROWS_PER_STEP = 8
SEQ_BLOCK = 1024
GDN_SEQ_BLOCK = 1024


def _params(*sem):
    return pltpu.CompilerParams(dimension_semantics=sem, vmem_limit_bytes=V7X_VMEM_LIMIT)


def _dot(a, b):
    return jnp.dot(a, b, preferred_element_type=F32)


def _dot_nt(a, b):
    return lax.dot_general(a, b, (((1,), (1,)), ((), ())), preferred_element_type=F32)


def _dot_tn(a, b):
    return lax.dot_general(a, b, (((0,), (0,)), ((), ())), preferred_element_type=F32)


def _split3(x):
    x1 = x.astype(BF16)
    r1 = x - x1.astype(F32)
    x2 = r1.astype(BF16)
    x3 = (r1 - x2.astype(F32)).astype(BF16)
    return x1, x2, x3


def _sum_right(x, m01):
    return _dot(x.astype(BF16), m01)


def _exact_right(x, m01):
    x1, x2, x3 = _split3(x)
    return _dot(x1, m01) + _dot(x2, m01) + _dot(x3, m01)


def _rms(x, w):
    ms = jnp.mean(x * x, axis=-1, keepdims=True)
    return x * lax.rsqrt(ms + EPS) * w


def _softplus(x):
    return jnp.maximum(x, 0.0) + jnp.log1p(jnp.exp(-jnp.abs(x)))


def _sigmoid(x):
    return 1.0 / (1.0 + jnp.exp(-x))


def _segment_mean_matrix(width):
    r = lax.broadcasted_iota(jnp.int32, (width, width), 0) // HEAD_DIM
    c = lax.broadcasted_iota(jnp.int32, (width, width), 1) // HEAD_DIM
    return jnp.where(r == c, 1.0, 0.0).astype(BF16)


def _head_sumsq(t):
    return _sum_right(t * t, _segment_mean_matrix(t.shape[-1]))


def _aligned(x, m):
    return x if isinstance(x, int) else pl.multiple_of(x, m)


def _pair_masks(rev):
    r = lax.broadcasted_iota(jnp.int32, (PAIR, PAIR), 0)
    c = lax.broadcasted_iota(jnp.int32, (PAIR, PAIR), 1)
    same = (r // HEAD_DIM) == (c // HEAD_DIM)
    t, s = r % HEAD_DIM, c % HEAD_DIM
    if rev:
        return same & (s >= t), same & (s > t)
    return same & (s <= t), same & (s < t)


def _stack_pair(x2):
    lane = lax.broadcasted_iota(jnp.int32, x2.shape, 1)
    zero = jnp.zeros_like(x2)
    return jnp.concatenate([jnp.where(lane < HEAD_DIM, x2, zero),
                            jnp.where(lane >= HEAD_DIM, x2, zero)], axis=0)


GD_PAIRS = GDN_HEADS // 2
CONV_HALO = 8
PREP_ROWS = 128
IN_SEGMENTS = (
    (NA_WIDTH, BF16), (NA_WIDTH, BF16), (NA_WIDTH, BF16),
    (ML_WIDTH, BF16), (ML_WIDTH, BF16), (ML_WIDTH, BF16), (ML_WIDTH, F32),
    (GDN_WIDTH, F32), (GATE_LANES, F32))
CONV_COLS = 3 * GDN_WIDTH
IN_COLS = CONV_COLS + sum(w for w, _ in IN_SEGMENTS)
IN_DOT_COLS = 3 * MXU_COLS


def _in_proj_kernel(x_ref, prev_ref, next_ref, nw_ref, w_ref, cw_ref, gq_ref, gk_ref, gv_ref, *rest,
                    tm, seq_len):
    out_refs, ext_scr = rest[:-1], rest[-1]
    i = pl.program_id(0)
    nw = nw_ref[...]
    h = _rms(x_ref[...], nw).astype(BF16)
    blocks_per_seq = seq_len // tm
    has_prev = (i % blocks_per_seq) > 0
    has_next = (i % blocks_per_seq) < blocks_per_seq - 1
    halo = _rms(jnp.concatenate([prev_ref[...], next_ref[...]], axis=0), nw).astype(BF16)
    w_conv = w_ref[:, 0:CONV_COLS]
    halo_p = _dot(halo, w_conv)
    ext_scr[0:CONV_HALO, :] = jnp.where(has_prev, halo_p[:CONV_HALO], 0.0)
    ext_scr[CONV_HALO + tm:, :] = jnp.where(has_next, halo_p[CONV_HALO:], 0.0)
    ext_scr[CONV_HALO:CONV_HALO + tm, :] = _dot(h, w_conv)

    base = CONV_HALO - CONV_K // 2
    seg = _segment_mean_matrix(PAIR)
    taps = [cw_ref[t:t + 1, :] for t in range(CONV_K)]

    def conv_tiles():
        for rb in range(tm // PREP_ROWS):
            r0 = rb * PREP_ROWS
            for ct in range(CONV_COLS // PAIR):
                cols = slice(ct * PAIR, (ct + 1) * PAIR)
                y = ext_scr[r0 + base:r0 + base + PREP_ROWS, cols] * taps[0][:, cols]
                for t in range(1, CONV_K):
                    y = y + ext_scr[r0 + base + t:r0 + base + t + PREP_ROWS, cols] * taps[t][:, cols]
                y = y * _sigmoid(y)
                which, off = divmod(ct * PAIR, GDN_WIDTH)
                if which < 2:
                    y = y * lax.rsqrt(_sum_right(y * y, seg) + EPS)
                (gq_ref, gk_ref, gv_ref)[which][r0:r0 + PREP_ROWS, off:off + PAIR] = y
                yield

    tiles = conv_tiles()
    n_tiles = (tm // PREP_ROWS) * (CONV_COLS // PAIR)
    starts = CONV_COLS + np.cumsum([0] + [w for w, _ in IN_SEGMENTS])
    chunk_starts = list(range(CONV_COLS, IN_COLS, IN_DOT_COLS))
    for k, c0 in enumerate(chunk_starts):
        c1 = min(c0 + IN_DOT_COLS, IN_COLS)
        acc = _dot(h, w_ref[:, c0:c1])
        for o_ref, s0, s1 in zip(out_refs, starts[:-1], starts[1:]):
            a, b = max(c0, int(s0)), min(c1, int(s1))
            if a < b:
                o_ref[:, a - int(s0):b - int(s0)] = acc[:, a - c0:b - c0].astype(o_ref.dtype)
        for _ in range(-(-n_tiles // len(chunk_starts))):
            next(tiles, None)
    for _ in tiles:
        pass


def _in_proj(x2d, norm_w, w_perm, conv_w, seq_len):
    t = x2d.shape[0]
    tm = min(TM_ROWS, seq_len)
    hb = tm // CONV_HALO
    last = t // CONV_HALO - 1
    gd = jax.ShapeDtypeStruct((t, GDN_WIDTH), F32)
    outs = (gd, gd, gd) + tuple(jax.ShapeDtypeStruct((t, w), dt) for w, dt in IN_SEGMENTS)
    row = lambda w: pl.BlockSpec((tm, w), lambda i: (i, 0))
    return pl.pallas_call(
        functools.partial(_in_proj_kernel, tm=tm, seq_len=seq_len),
        out_shape=outs,
        grid=(t // tm,),
        in_specs=[row(D_MODEL),
                  pl.BlockSpec((CONV_HALO, D_MODEL), lambda i: (jnp.maximum(i * hb - 1, 0), 0)),
                  pl.BlockSpec((CONV_HALO, D_MODEL), lambda i: (jnp.minimum((i + 1) * hb, last), 0)),
                  pl.BlockSpec((1, D_MODEL), lambda i: (0, 0)),
                  pl.BlockSpec((D_MODEL, IN_COLS), lambda i: (0, 0), pipeline_mode=pl.Buffered(1)),
                  pl.BlockSpec((CONV_K, CONV_COLS), lambda i: (0, 0))],
        out_specs=(row(GDN_WIDTH), row(GDN_WIDTH), row(GDN_WIDTH)) + tuple(row(w) for w, _ in IN_SEGMENTS),
        scratch_shapes=[pltpu.VMEM((tm + 2 * CONV_HALO, CONV_COLS), F32)],
        compiler_params=_params("parallel"),
        name="in_proj",
    )(x2d, x2d, x2d, norm_w.reshape(1, D_MODEL), w_perm, conv_w.astype(F32))


def _permute_w_in(w):
    parts = jnp.split(w, np.cumsum(IN_SIZES)[:-1], axis=-1)
    (na_q, na_k, na_v, ml_q, ml_k, ml_v, ml_o, ml_i, ml_f, gd_q, gd_k, gd_v, gd_z, gd_b, gd_a) = parts
    pad = jnp.zeros(w.shape[:-1] + (GATE_LANES - GATE_USED,), w.dtype)
    return jnp.concatenate([gd_q, gd_k, gd_v, na_q, na_k, na_v, ml_q, ml_k, ml_v, ml_o, gd_z,
                            gd_b, gd_a, ml_i, ml_f, pad], axis=-1).astype(BF16)


NA_BIAS_ROWS = 96


def _na_bias_kernel(rb_ref, o_ref):
    n = GRID_W * GRID_W
    dc = lax.broadcasted_iota(jnp.int32, (GATE_LANES, n), 0)
    col = lax.broadcasted_iota(jnp.int32, (GATE_LANES, n), 1)
    q, kc = col // GRID_W, col % GRID_W
    c0 = jnp.clip(q - NA_WIN_COLS // 2, 0, GRID_W - NA_WIN_COLS)
    valid = (kc >= c0) & (kc < c0 + NA_WIN_COLS)
    onehot = jnp.where(valid & (kc - q + (NA_WIN_COLS - 1) == dc), 1.0, 0.0).astype(BF16)
    o_ref[...] = jnp.where(valid[0:1, :], _exact_right(rb_ref[...], onehot), NEG)


def _na_bias_table(rel_bias):
    nh, ndr, ndc = rel_bias.shape
    rb = jnp.zeros((NA_BIAS_ROWS, GATE_LANES), F32).at[:nh * ndr, :ndc].set(
        rel_bias.astype(F32).reshape(nh * ndr, ndc))
    band = pl.pallas_call(
        _na_bias_kernel,
        out_shape=jax.ShapeDtypeStruct((NA_BIAS_ROWS, GRID_W * GRID_W), F32),
        name="na_bias_expand",
    )(rb)
    band = band[:nh * ndr].reshape(nh, ndr, GRID_W, GRID_W)
    first = np.clip(np.arange(ndr + 1) - 1, 0, ndr - 1)
    second = np.clip(np.arange(ndr + 1), 0, ndr - 1)
    tab = jnp.concatenate([band[:, first], band[:, second]], axis=-1)
    tab = tab.reshape(NA_HEADS // 2, 2, ndr + 1, GRID_W, PAIR)
    return jnp.moveaxis(tab, 1, 2).reshape(NA_HEADS // 2, ndr + 1, 2 * GRID_W, PAIR)


NA_ITEM_ROWS = 4
NA_KEY_ROWS = NA_ITEM_ROWS + NA_WIN_ROWS


NA_INNER_BLOCKS = NA_KEY_ROWS // 2 - 1


def _na_inner_bias(bias_tab):
    half = NA_WIN_ROWS // 2
    rows = []
    for i in range(NA_ITEM_ROWS):
        blocks = []
        for jp in range(i // 2, i // 2 + NA_INNER_BLOCKS):
            ok_a, ok_b = 0 <= 2 * jp - i < NA_WIN_ROWS, 0 <= 2 * jp + 1 - i < NA_WIN_ROWS
            entry = min(max(2 * jp - i + half, 0), 2 * NA_WIN_ROWS - 1)
            pen = np.concatenate([np.full(GRID_W, 0.0 if ok_a else NEG), np.full(GRID_W, 0.0 if ok_b else NEG)])
            blocks.append(bias_tab[:, entry] + jnp.asarray(pen, F32))
        rows.append(jnp.concatenate(blocks, axis=-1))
    return jnp.stack(rows, axis=1)


def _na_kernel(q_ref, k_ref, v_ref, bias_ref, inner_ref, o_ref, *, rows, rows_per_step):
    j = pl.program_id(1)
    nkeys = NA_KEY_ROWS * GRID_W
    half = NA_WIN_ROWS // 2
    lane = lax.broadcasted_iota(jnp.int32, (GRID_W, PAIR), 1)
    lane_blk = lax.broadcasted_iota(jnp.int32, (PAIR, PAIR), 1)

    def run(inner):
        items = []
        for g in range(rows_per_step // NA_ITEM_ROWS):
            r_first = j * rows_per_step + g * NA_ITEM_ROWS
            kr0 = jnp.clip(r_first - half, 0, rows - NA_KEY_ROWS)
            koff = pl.multiple_of(kr0 * GRID_W, GRID_W)
            entry, pen = {}, {}
            if not inner:
                for i in range(NA_ITEM_ROWS):
                    r = r_first + i
                    r0 = jnp.clip(r - half, 0, rows - NA_WIN_ROWS)
                    for jp in range(NA_KEY_ROWS // 2):
                        key_a = kr0 + 2 * jp
                        ok_a = (key_a >= r0) & (key_a < r0 + NA_WIN_ROWS)
                        ok_b = (key_a + 1 >= r0) & (key_a + 1 < r0 + NA_WIN_ROWS)
                        entry[i, jp] = jnp.clip(key_a - r + (NA_WIN_ROWS - 1), -1, 2 * NA_WIN_ROWS - 2) + 1
                        pen[i, jp] = jnp.where(lane_blk < GRID_W, jnp.where(ok_a, 0.0, NEG),
                                               jnp.where(ok_b, 0.0, NEG))
            for p in range(NA_HEADS // 2):
                items.append(dict(g=g, p=p, koff=koff, entry=entry, pen=pen, cols=slice(p * PAIR, (p + 1) * PAIR)))
        def logits(it):
            q0 = it["g"] * NA_ITEM_ROWS * GRID_W
            q_lhs = jnp.concatenate(
                [_stack_pair(q_ref[q0 + i * GRID_W:q0 + (i + 1) * GRID_W, it["cols"]])
                 for i in range(NA_ITEM_ROWS)], axis=0)
            q_lhs = (q_lhs.astype(F32) * (HEAD_DIM ** -0.5)).astype(BF16)
            it["s"] = _dot_nt(q_lhs, k_ref[pl.ds(it["koff"], nkeys), it["cols"]])

        def weights(it):
            s = it.pop("s")
            blocks = []
            for i in range(NA_ITEM_ROWS):
                if inner:
                    b0 = (i // 2) * PAIR
                    blocks.append(s[i * PAIR:(i + 1) * PAIR, b0:b0 + NA_INNER_BLOCKS * PAIR] + inner_ref[it["p"], i])
                else:
                    blocks.append(jnp.concatenate(
                        [s[i * PAIR:(i + 1) * PAIR, jp * PAIR:(jp + 1) * PAIR]
                         + bias_ref[it["p"], it["entry"][i, jp]] + it["pen"][i, jp]
                         for jp in range(NA_KEY_ROWS // 2)], axis=1))
            s = jnp.concatenate(blocks, axis=0)
            e = jnp.exp(s - jnp.max(s, axis=-1, keepdims=True))
            it["l"] = jnp.sum(e, axis=-1, keepdims=True)
            e = e.astype(BF16)
            if inner:
                zero = jnp.zeros((PAIR, PAIR), BF16)
                e = jnp.concatenate(
                    [jnp.concatenate(([zero] if i // 2 else []) + [e[i * PAIR:(i + 1) * PAIR]]
                                     + ([] if i // 2 else [zero]), axis=1) for i in range(NA_ITEM_ROWS)], axis=0)
            it["o"] = _dot(e, v_ref[pl.ds(it["koff"], nkeys), it["cols"]])

        def finish(it):
            o = it.pop("o") * (1.0 / it.pop("l"))
            q0 = it["g"] * NA_ITEM_ROWS * GRID_W
            for i in range(NA_ITEM_ROWS):
                blk = o[i * PAIR:(i + 1) * PAIR]
                o_ref[q0 + i * GRID_W:q0 + (i + 1) * GRID_W, it["cols"]] = jnp.where(
                    lane < HEAD_DIM, blk[:GRID_W], blk[GRID_W:]).astype(o_ref.dtype)

        n = len(items)
        logits(items[0])
        for k in range(n):
            if k + 1 < n:
                logits(items[k + 1])
            weights(items[k])
            if k >= 1:
                finish(items[k - 1])
        finish(items[n - 1])


    first_row = j * rows_per_step
    inner = (first_row >= half) & (first_row + rows_per_step - NA_ITEM_ROWS <= rows - NA_WIN_ROWS)

    @pl.when(inner)
    def _():
        run(True)

    @pl.when(jnp.logical_not(inner))
    def _():
        run(False)


def _neighbourhood_attention(q, k, v, bias_tab):
    b, s, _ = q.shape
    rows = s // GRID_W
    assert rows >= NA_KEY_ROWS
    rps = min(NA_ROWS_PER_STEP, rows)
    tq = rps * GRID_W
    inner_tab = _na_inner_bias(bias_tab)
    return pl.pallas_call(
        functools.partial(_na_kernel, rows=rows, rows_per_step=rps),
        out_shape=jax.ShapeDtypeStruct((b, s, NA_WIDTH), BF16),
        grid=(b, rows // rps),
        in_specs=[pl.BlockSpec((None, tq, NA_WIDTH), lambda bi, j: (bi, j, 0)),
                  pl.BlockSpec((None, s, NA_WIDTH), lambda bi, j: (bi, 0, 0)),
                  pl.BlockSpec((None, s, NA_WIDTH), lambda bi, j: (bi, 0, 0)),
                  pl.BlockSpec(bias_tab.shape, lambda bi, j: (0, 0, 0, 0)),
                  pl.BlockSpec(inner_tab.shape, lambda bi, j: (0, 0, 0, 0))],
        out_specs=pl.BlockSpec((None, tq, NA_WIDTH), lambda bi, j: (bi, j, 0)),
        compiler_params=_params("parallel", "arbitrary"),
        name="neighbourhood_attention",
    )(q, k, v, bias_tab, inner_tab)


def _gate_rows(gates):
    b, s, _ = gates.shape
    g = gates[:, :, :GATE_USED].reshape(b, s // CHUNK, CHUNK, GATE_USED // 2, 2)
    return jnp.transpose(g, (0, 3, 1, 4, 2)).reshape(b, GATE_USED // 2, s // CHUNK, PAIR)


def _gate_row_spec(lane0, heads, d, chunks, chunk_block):
    pairs = heads // 2
    unit_block = (lane0 // 2 + d * pairs) // pairs
    assert unit_block * pairs == lane0 // 2 + d * pairs
    return pl.BlockSpec((None, pairs, chunks, PAIR), lambda bi, j: (bi, unit_block, chunk_block(j), 0))


def _param_rows(p, heads):
    return jnp.repeat(p.astype(F32).reshape(N_DIR, heads // 2, 2), HEAD_DIM, axis=-1).reshape(
        N_DIR, heads // 2, 1, PAIR)


def _pair_cumsum_matrix(rev):
    r = lax.broadcasted_iota(jnp.int32, (PAIR, PAIR), 0)
    c = lax.broadcasted_iota(jnp.int32, (PAIR, PAIR), 1)
    same = (r // HEAD_DIM) == (c // HEAD_DIM)
    sp, s = r % HEAD_DIM, c % HEAD_DIM
    return jnp.where(same & ((sp >= s) if rev else (sp <= s)), 1.0, 0.0).astype(BF16)


ML_PAIRS = ML_HEADS // 2
ML_UNITS = N_DIR * ML_PAIRS
ML_STEPS_PER_TRIP = 8
OS_ROWS = 16


def _outer_sum_operands(col_term, row_term):
    ones = jnp.ones((3, PAIR), BF16)
    zeros = jnp.zeros((OS_ROWS - 6, PAIR), BF16)
    a = jnp.concatenate(list(_split3(col_term)) + [ones, zeros], axis=0)
    b = jnp.concatenate([ones] + list(_split3(row_term)) + [zeros], axis=0)
    return a, b


def _mlstm_kernel(qf_ref, kf_ref, vf_ref, if_ref, ff_ref,
                  qb_ref, kb_ref, vb_ref, ib_ref, fb_ref,
                  ibr_ref, fbr_ref,
                  hf_ref, hb_ref,
                  c_scr, m_scr, cs_scr, rt_scr, ai_scr, em_scr, we_scr, a_scr, g_scr, vt_scr, ut_scr, *, chunks):
    j = pl.program_id(1)

    @pl.when(j == 0)
    def _():
        c_scr[...] = jnp.zeros_like(c_scr)
        m_scr[...] = jnp.zeros_like(m_scr)

    lane = lax.broadcasted_iota(jnp.int32, (chunks, PAIR), 1)
    pos = lane % HEAD_DIM
    lo = lane < HEAD_DIM
    for d, (i_ref, f_ref) in enumerate(((if_ref, ff_ref), (ib_ref, fb_ref))):
        rev = bool(d)
        ucum = _pair_cumsum_matrix(rev)
        last = 0 if rev else CHUNK - 1
        for p in range(ML_PAIRS):
            u = d * ML_PAIRS + p
            logf = -_softplus(-(f_ref[p] + fbr_ref[d, p]))
            bcum = _exact_right(logf, ucum)
            cs = i_ref[p] + ibr_ref[d, p] - bcum
            bl = jnp.where(lo, bcum[:, last:last + 1], bcum[:, HEAD_DIM + last:HEAD_DIM + last + 1])
            cm = cs
            for k in (1, 2, 4, 8, 16, 32):
                if rev:
                    cm = jnp.where(pos < HEAD_DIM - k, jnp.maximum(cm, pltpu.roll(cm, PAIR - k, axis=1)), cm)
                else:
                    cm = jnp.where(pos >= k, jnp.maximum(cm, pltpu.roll(cm, k, axis=1)), cm)
            w = bl + cs
            m_loc = jnp.where(lo, jnp.max(jnp.where(lo, w, NEG), axis=-1, keepdims=True),
                              jnp.max(jnp.where(lo, NEG, w), axis=-1, keepdims=True))
            m = m_scr[u]
            m_prev, a_rows, g_rows = [None] * chunks, [None] * chunks, [None] * chunks
            for c in (range(chunks - 1, -1, -1) if rev else range(chunks)):
                m_prev[c] = m
                m_new = jnp.maximum(bl[c:c + 1] + m, m_loc[c:c + 1])
                a_rows[c] = jnp.exp(bl[c:c + 1] + m - m_new)
                g_rows[c] = jnp.exp(m_loc[c:c + 1] - m_new)
                m = m_new
            m_scr[u] = m
            inter = bcum + jnp.concatenate(m_prev, axis=0)
            m_t = jnp.maximum(bcum + cm, inter)
            cs_scr[u] = cs
            rt_scr[u] = bcum - m_t
            ai_scr[u] = jnp.exp(inter - m_t)
            em_scr[u] = jnp.exp(-m_t)
            we_scr[u] = jnp.exp(w - m_loc)
            a_scr[u] = jnp.concatenate(a_rows, axis=0)
            g_scr[u] = jnp.concatenate(g_rows, axis=0)

    dirs = ((qf_ref, kf_ref, vf_ref, hf_ref), (qb_ref, kb_ref, vb_ref, hb_ref))
    units = [(d, p) for d in range(N_DIR) for p in range(ML_PAIRS)]
    lane64 = lax.broadcasted_iota(jnp.int32, (CHUNK, PAIR), 1)
    one_hi = jnp.where(lane64 == HEAD_DIM, 1.0, 0.0).astype(BF16)
    one_lo = jnp.where(lane64 == 0, 1.0, 0.0).astype(BF16)
    lane_row = lax.broadcasted_iota(jnp.int32, (1, PAIR), 1)

    def key_tile(d, p, r0):
        k_pair = _stack_pair(dirs[d][1][pl.ds(r0, CHUNK), p * PAIR:(p + 1) * PAIR])
        return (k_pair.astype(F32) * (HEAD_DIM ** -0.5)).astype(BF16)

    def contributions(t):
        for k in range(ML_STEPS_PER_TRIP):
            step = t * ML_STEPS_PER_TRIP + k
            for u, (d, p) in enumerate(units):
                c = (chunks - 1 - step) if d else step
                r0 = _aligned(c * CHUNK, CHUNK)
                v2 = dirs[d][2][pl.ds(r0, CHUNK), p * PAIR:(p + 1) * PAIR]
                v_ext = jnp.concatenate([jnp.where(lane64 < HEAD_DIM, v2, one_hi),
                                         jnp.where(lane64 >= HEAD_DIM, v2, one_lo)], axis=0)
                v_t = v_ext.astype(F32).T
                vt_scr[u, c] = v_t.astype(BF16)
                ut_scr[u, c] = _dot((v_t * we_scr[u, pl.ds(c, 1), :]).astype(BF16), key_tile(d, p, r0))
            yield

    def outputs(t):
        masks = [_pair_masks(True)[0], _pair_masks(False)[0]]
        items = []
        for u, (d, p) in enumerate(units):
            ct = c_scr[u]
            for k in range(ML_STEPS_PER_TRIP):
                step = t * ML_STEPS_PER_TRIP + k
                c = (chunks - 1 - step) if d else step
                r0 = _aligned(c * CHUNK, CHUNK)
                row = pl.ds(c, 1)
                items.append(dict(u=u, d=d, p=p, c=c, r0=r0, row=row, ct=ct.astype(BF16)))
                ct = a_scr[u, row, :] * ct + g_scr[u, row, :] * ut_scr[u, c]
            c_scr[u] = ct
        yield
        for n, it in enumerate(items):
            u, d, p, r0, row = it["u"], it["d"], it["p"], it["r0"], it["row"]
            q_pair = _stack_pair(dirs[d][0][pl.ds(r0, CHUNK), p * PAIR:(p + 1) * PAIR])
            os_a, os_b = _outer_sum_operands(cs_scr[u, row, :], rt_scr[u, row, :])
            it["gram"] = _dot_nt(key_tile(d, p, r0), q_pair)
            it["osum"] = _dot_tn(os_a, os_b)
            it["st"] = _dot_nt(it.pop("ct"), q_pair)
            if n % ML_UNITS == ML_UNITS - 1:
                yield
        for n, it in enumerate(items):
            u, c = it["u"], it["c"]
            s_t = it.pop("gram") * jnp.exp(jnp.where(masks[it["d"]], it.pop("osum"), NEG))
            it["intra"] = jnp.sum(s_t, axis=0, keepdims=True)
            it["num"] = _dot(vt_scr[u, c], s_t.astype(BF16))
            if n % ML_UNITS == ML_UNITS - 1:
                yield
        for n, it in enumerate(items):
            u, d, p, row = it["u"], it["d"], it["p"], it["row"]
            a_inter = ai_scr[u, row, :]
            st = it.pop("st")
            den = a_inter * jnp.where(lane_row < HEAD_DIM, st[HEAD_DIM:HEAD_DIM + 1], st[0:1]) + it.pop("intra")
            out_t = (it.pop("num") + a_inter * st) * (1.0 / jnp.maximum(jnp.abs(den), em_scr[u, row, :]))
            out = out_t.T
            dirs[d][3][pl.ds(it["r0"], CHUNK), p * PAIR:(p + 1) * PAIR] = jnp.where(
                lane64 < HEAD_DIM, out[:CHUNK], out[CHUNK:])
            if n % ML_UNITS == ML_UNITS - 1:
                yield

    trips = chunks // ML_STEPS_PER_TRIP

    def fused(t, carry):
        filler = contributions(t)
        for _ in outputs(t - 1):
            next(filler, None)
        for _ in filler:
            pass
        return carry

    for _ in contributions(0):
        pass
    lax.fori_loop(1, trips, fused, 0)
    for _ in outputs(trips - 1):
        pass


def _mlstm(q, k, v, gate_rows, i_bias, f_bias):
    b, s, _ = q.shape
    tb = min(SEQ_BLOCK, s)
    nb = s // tb
    chunks = tb // CHUNK
    assert s % tb == 0 and chunks % ML_STEPS_PER_TRIP == 0
    ibr = _param_rows(i_bias, ML_HEADS)
    fbr = _param_rows(f_bias, ML_HEADS)

    def fwd(bi, j):
        return (bi, j, 0)

    def bwd(bi, j):
        return (bi, nb - 1 - j, 0)

    seq = lambda w, im: pl.BlockSpec((None, tb, w), im)
    rows = lambda lane0, d: _gate_row_spec(lane0, ML_HEADS, d, chunks, (lambda j: nb - 1 - j) if d else (lambda j: j))
    full = lambda a: pl.BlockSpec(a.shape, lambda bi, j: (0,) * a.ndim)
    out = jax.ShapeDtypeStruct((b, s, ML_WIDTH), F32)
    rows_scr = pltpu.VMEM((ML_UNITS, chunks, PAIR), F32)
    return pl.pallas_call(
        functools.partial(_mlstm_kernel, chunks=chunks),
        out_shape=(out, out),
        grid=(b, nb),
        in_specs=[seq(ML_WIDTH, fwd), seq(ML_WIDTH, fwd), seq(ML_WIDTH, fwd), rows(GATE_ML_I, 0), rows(GATE_ML_F, 0),
                  seq(ML_WIDTH, bwd), seq(ML_WIDTH, bwd), seq(ML_WIDTH, bwd), rows(GATE_ML_I, 1), rows(GATE_ML_F, 1),
                  full(ibr), full(fbr)],
        out_specs=(seq(ML_WIDTH, fwd), seq(ML_WIDTH, bwd)),
        scratch_shapes=[pltpu.VMEM((ML_UNITS, PAIR, PAIR), F32),
                        pltpu.VMEM((ML_UNITS, 1, PAIR), F32),
                        rows_scr, rows_scr, rows_scr, rows_scr, rows_scr, rows_scr, rows_scr,
                        pltpu.VMEM((ML_UNITS, chunks, PAIR, PAIR), BF16),
                        pltpu.VMEM((ML_UNITS, chunks, PAIR, PAIR), F32)],
        compiler_params=_params("parallel", "arbitrary"),
        name="mlstm",
    )(q, k, v, gate_rows, gate_rows, q, k, v, gate_rows, gate_rows, ibr, fbr)


GD_CHUNKS_PER_TRIP = 2
GD_INV_BLOCK = 16
GD_UNITS = N_DIR * GD_PAIRS


def _gdn_kernel(qf_ref, kf_ref, vf_ref, af_ref, bf_ref,
                qb_ref, kb_ref, vb_ref, ab_ref, bb_ref,
                alr_ref, dtr_ref,
                of_ref, ob_ref,
                s_scr, gc_scr, beta_scr, egc_scr, qsc_scr, fb_scr, eg_scr,
                u_scr, wq_scr, at_scr, ks_scr, *, chunks):
    j = pl.program_id(1)

    @pl.when(j == 0)
    def _():
        s_scr[...] = jnp.zeros_like(s_scr)

    lane_row = lax.broadcasted_iota(jnp.int32, (chunks, PAIR), 1)
    for d, (a_ref, b_ref) in enumerate(((af_ref, bf_ref), (ab_ref, bb_ref))):
        ucum = _pair_cumsum_matrix(rev=bool(d))
        last = 0 if d else CHUNK - 1
        for p in range(GD_PAIRS):
            u = d * GD_PAIRS + p
            g = -jnp.exp(alr_ref[d, p]) * _softplus(a_ref[p] + dtr_ref[d, p])
            gc = _exact_right(g, ucum)
            beta = _sigmoid(b_ref[p])
            g_last = jnp.where(lane_row < HEAD_DIM, gc[:, last:last + 1],
                               gc[:, HEAD_DIM + last:HEAD_DIM + last + 1])
            gc_scr[u] = gc
            beta_scr[u] = beta
            egc_scr[u] = jnp.exp(gc)
            qsc_scr[u] = jnp.exp(gc) * (HEAD_DIM ** -0.5)
            fb_scr[u] = jnp.exp(g_last - gc) * beta
            eg_scr[u] = jnp.exp(g_last)

    r = lax.broadcasted_iota(jnp.int32, (PAIR, PAIR), 0)
    cidx = lax.broadcasted_iota(jnp.int32, (PAIR, PAIR), 1)
    is_diag = r == cidx
    same16 = (r // GD_INV_BLOCK) == (cidx // GD_INV_BLOCK)
    eye_s = jnp.where(lax.broadcasted_iota(jnp.int32, (GD_INV_BLOCK, PAIR), 0)
                      == lax.broadcasted_iota(jnp.int32, (GD_INV_BLOCK, PAIR), 1) % GD_INV_BLOCK, 1.0, 0.0)
    dirs = ((qf_ref, kf_ref, vf_ref, of_ref), (qb_ref, kb_ref, vb_ref, ob_ref))
    units = [(d, p) for d in range(N_DIR) for p in range(GD_PAIRS)]

    def strip_of(bd):
        s = bd[0:GD_INV_BLOCK]
        for b in range(1, PAIR // GD_INV_BLOCK):
            s = s + bd[b * GD_INV_BLOCK:(b + 1) * GD_INV_BLOCK]
        return s

    def blockdiag_of(s):
        return jnp.where(same16, jnp.concatenate([s] * (PAIR // GD_INV_BLOCK), axis=0), jnp.zeros((), s.dtype))

    def precompute(t):
        masks = [_pair_masks(False), _pair_masks(True)]
        items = []
        for cc in range(GD_CHUNKS_PER_TRIP):
            step = t * GD_CHUNKS_PER_TRIP + cc
            for d in range(N_DIR):
                c = (chunks - 1 - step) if d else step
                for p in range(GD_PAIRS):
                    items.append(dict(d=d, p=p, c=c, r0=_aligned(c * CHUNK, CHUNK), row=pl.ds(c, 1),
                                      u=d * GD_PAIRS + p, cols=slice(p * PAIR, (p + 1) * PAIR)))
        for it in items:
            u, row = it["u"], it["row"]
            q_ref, k_ref = dirs[it["d"]][0], dirs[it["d"]][1]
            q_b = _stack_pair(q_ref[pl.ds(it["r0"], CHUNK), it["cols"]]).astype(BF16)
            k_b = _stack_pair(k_ref[pl.ds(it["r0"], CHUNK), it["cols"]]).astype(BF16)
            it["grams"] = _dot_nt(jnp.concatenate([q_b, k_b], axis=0), k_b)
            gc = gc_scr[u, row, :]
            it["osum"] = _dot_tn(*_outer_sum_operands(gc, -gc))
        yield
        for it in items:
            u, row = it["u"], it["row"]
            valid, strict = masks[it["d"]]
            decay = jnp.exp(jnp.where(valid, it.pop("osum"), NEG)) * beta_scr[u, row, :]
            grams = it.pop("grams")
            attn = (grams[:PAIR] * (HEAD_DIM ** -0.5) * decay).astype(BF16)
            q_scale = jnp.where(is_diag, qsc_scr[u, row, :], 0.0).astype(BF16)
            at_scr[u, it["c"]] = jnp.concatenate([attn, q_scale], axis=1)
            neg_a = jnp.where(strict, -(grams[PAIR:] * decay), 0.0)
            nd = jnp.where(same16, neg_a, 0.0)
            it["nl"] = (neg_a - nd).astype(BF16)
            nd_s = strip_of(nd)
            it["t_s"] = eye_s + nd_s
            it["pw_s"] = _dot(nd_s.astype(BF16), nd.astype(BF16))
        yield
        for _ in range(2):
            for it in items:
                pw_s = it["pw_s"].astype(BF16)
                both = _dot(jnp.concatenate([it["t_s"].astype(BF16), pw_s], axis=0), blockdiag_of(pw_s))
                it["t_s"] = it["t_s"] + both[:GD_INV_BLOCK]
                it["pw_s"] = both[GD_INV_BLOCK:]
        yield
        for it in items:
            pw_s = it.pop("pw_s").astype(BF16)
            t_s = it.pop("t_s")
            it["t_inv"] = blockdiag_of(t_s + _dot(t_s.astype(BF16), blockdiag_of(pw_s)))
        yield
        for it in items:
            it["x_b"] = it["t_inv"].astype(BF16)
            it["m"] = _dot(it["x_b"], it.pop("nl"))
        yield
        for it in items:
            m_b = it.pop("m").astype(BF16)
            both = _dot(m_b, jnp.concatenate([m_b, it.pop("x_b")], axis=1))
            it["m2"] = both[:, :PAIR]
            it["t_inv"] = it["t_inv"] + both[:, PAIR:]
        yield
        for it in items:
            it["t_inv"] = it["t_inv"] + _dot(it.pop("m2").astype(BF16), it["t_inv"].astype(BF16))
        yield
        for it in items:
            u, c, row = it["u"], it["c"], it["row"]
            q_ref, k_ref, v_ref = dirs[it["d"]][:3]
            rows = pl.ds(it["r0"], CHUNK)
            q_b = _stack_pair(q_ref[rows, it["cols"]]).astype(BF16)
            k_pair = _stack_pair(k_ref[rows, it["cols"]])
            v_b = _stack_pair(v_ref[rows, it["cols"]]).astype(BF16)
            t_inv = it.pop("t_inv")
            u_scr[u, c] = _dot(t_inv.astype(BF16), v_b)
            w = _dot((t_inv * egc_scr[u, row, :]).astype(BF16), k_pair.astype(BF16))
            wq_scr[u, c] = jnp.concatenate([w.astype(BF16), q_b], axis=0)
            ks_scr[u, c] = (k_pair.T * fb_scr[u, row, :]).astype(BF16)

    def scan(i):
        cs = (i, chunks - 1 - i)
        state = [s_scr[u] for u in range(len(units))]
        ws = [_dot(wq_scr[u, cs[d]], state[u].astype(BF16)) for u, (d, p) in enumerate(units)]
        yield
        v_qs = [jnp.concatenate([u_scr[u, cs[d]] - ws[u][:PAIR], ws[u][PAIR:]], axis=0).astype(BF16)
                for u, (d, p) in enumerate(units)]
        for u, (d, p) in enumerate(units):
            s_scr[u] = state[u] * eg_scr[u, pl.ds(cs[d], 1), :] + _dot(ks_scr[u, cs[d]], v_qs[u][:PAIR])
        for u, (d, p) in enumerate(units):
            o = _dot(at_scr[u, cs[d]], v_qs[u])
            r0 = _aligned(cs[d] * CHUNK, CHUNK)
            dirs[d][3][pl.ds(r0, CHUNK), p * PAIR:(p + 1) * PAIR] = o[:CHUNK] + o[CHUNK:]

    trips = chunks // GD_CHUNKS_PER_TRIP

    def scan_trip(t):
        for cc in range(GD_CHUNKS_PER_TRIP):
            yield from scan(t * GD_CHUNKS_PER_TRIP + cc)
            yield

    def fused(t, carry):
        filler = scan_trip(t - 1)
        for _ in precompute(t):
            next(filler, None)
        for _ in filler:
            pass
        return carry

    for _ in precompute(0):
        pass
    lax.fori_loop(1, trips, fused, 0)
    for _ in scan_trip(trips - 1):
        pass


def _gdn(q, k, v, gate_rows, a_log, dt_bias):
    b, s, _ = q.shape
    tb = min(GDN_SEQ_BLOCK, s)
    nb = s // tb
    chunks = tb // CHUNK
    assert s % tb == 0 and chunks % GD_CHUNKS_PER_TRIP == 0
    alr = _param_rows(a_log, GDN_HEADS)
    dtr = _param_rows(dt_bias, GDN_HEADS)

    def fwd(bi, j):
        return (bi, j, 0)

    def bwd(bi, j):
        return (bi, nb - 1 - j, 0)

    seq = lambda w, im: pl.BlockSpec((None, tb, w), im)
    rows = lambda lane0, d: _gate_row_spec(lane0, GDN_HEADS, d, chunks, (lambda j: nb - 1 - j) if d else (lambda j: j))
    full = lambda a: pl.BlockSpec(a.shape, lambda bi, j: (0,) * a.ndim)
    out = jax.ShapeDtypeStruct((b, s, GDN_WIDTH), F32)
    rows_scr = pltpu.VMEM((GD_UNITS, chunks, PAIR), F32)
    return pl.pallas_call(
        functools.partial(_gdn_kernel, chunks=chunks),
        out_shape=(out, out),
        grid=(b, nb),
        in_specs=[seq(GDN_WIDTH, fwd), seq(GDN_WIDTH, fwd), seq(GDN_WIDTH, fwd), rows(GATE_GD_A, 0), rows(GATE_GD_B, 0),
                  seq(GDN_WIDTH, bwd), seq(GDN_WIDTH, bwd), seq(GDN_WIDTH, bwd), rows(GATE_GD_A, 1), rows(GATE_GD_B, 1),
                  full(alr), full(dtr)],
        out_specs=(seq(GDN_WIDTH, fwd), seq(GDN_WIDTH, bwd)),
        scratch_shapes=[pltpu.VMEM((GD_UNITS, PAIR, PAIR), F32),
                        rows_scr, rows_scr, rows_scr, rows_scr, rows_scr, rows_scr,
                        pltpu.VMEM((GD_UNITS, chunks, PAIR, PAIR), F32),
                        pltpu.VMEM((GD_UNITS, chunks, 2 * PAIR, PAIR), BF16),
                        pltpu.VMEM((GD_UNITS, chunks, PAIR, 2 * PAIR), BF16),
                        pltpu.VMEM((GD_UNITS, chunks, PAIR, PAIR), BF16)],
        compiler_params=_params("parallel", "arbitrary"),
        name="gated_deltanet",
    )(q, k, v, gate_rows, gate_rows, q, k, v, gate_rows, gate_rows, alr, dtr)


def _out_proj_kernel(x_ref, na_ref, hf_ref, hb_ref, mo_ref, mw_ref, of_ref, ob_ref, gz_ref, gw_ref,
                     w_ref, o_ref):
    hs = hf_ref[...] + hb_ref[...]
    y_ml = hs * lax.rsqrt(_head_sumsq(hs) * (1.0 / HEAD_DIM) + EPS) * mw_ref[...] * _sigmoid(mo_ref[...])
    os_ = of_ref[...] + ob_ref[...]
    z = gz_ref[...]
    y_gd = os_ * lax.rsqrt(_head_sumsq(os_) * (1.0 / HEAD_DIM) + EPS) * gw_ref[...] * (z * _sigmoid(z))
    acc = x_ref[...] + _dot(na_ref[...], w_ref[0:NA_WIDTH, :])
    acc = acc + _dot(y_ml.astype(BF16), w_ref[NA_WIDTH:NA_WIDTH + ML_WIDTH, :])
    acc = acc + _dot(y_gd.astype(BF16), w_ref[NA_WIDTH + ML_WIDTH:, :])
    o_ref[...] = acc


def _out_proj(x2d, y_na, hf, hb, ml_o, ml_norm_w, of, ob, gd_z, gdn_norm_w, w_out):
    t = x2d.shape[0]
    tm = min(TM_ROWS, t)
    row = lambda w: pl.BlockSpec((tm, w), lambda i: (i, 0))
    const = lambda r, c: pl.BlockSpec((r, c), lambda i: (0, 0))
    return pl.pallas_call(
        _out_proj_kernel,
        out_shape=jax.ShapeDtypeStruct((t, D_MODEL), F32),
        grid=(t // tm,),
        in_specs=[row(D_MODEL), row(NA_WIDTH), row(ML_WIDTH), row(ML_WIDTH), row(ML_WIDTH), const(1, ML_WIDTH),
                  row(GDN_WIDTH), row(GDN_WIDTH), row(GDN_WIDTH), const(1, GDN_WIDTH),
                  const(D_MODEL, D_MODEL)],
        out_specs=row(D_MODEL),
        compiler_params=_params("parallel"),
        name="out_proj",
    )(x2d, y_na, hf, hb, ml_o, ml_norm_w.reshape(1, ML_WIDTH).astype(F32), of, ob, gd_z,
      gdn_norm_w.reshape(1, GDN_WIDTH).astype(F32), w_out.astype(BF16))


def _mem_kv_kernel(m_ref, nw_ref, w_ref, k_ref, v_ref):
    h = _rms(m_ref[...], nw_ref[...]).astype(BF16)
    k_ref[...] = _dot(h, w_ref[:, :D_MODEL]).astype(BF16)
    v_ref[...] = _dot(h, w_ref[:, D_MODEL:]).astype(BF16)


def _mem_kv(mem2d, norm_w, w_kv):
    t = mem2d.shape[0]
    tm = min(TM_ROWS, t)
    out = jax.ShapeDtypeStruct((t, D_MODEL), BF16)
    return pl.pallas_call(
        _mem_kv_kernel,
        out_shape=(out, out),
        grid=(t // tm,),
        in_specs=[pl.BlockSpec((tm, D_MODEL), lambda i: (i, 0)),
                  pl.BlockSpec((1, D_MODEL), lambda i: (0, 0)),
                  pl.BlockSpec((D_MODEL, 2 * D_MODEL), lambda i: (0, 0))],
        out_specs=(pl.BlockSpec((tm, D_MODEL), lambda i: (i, 0)), pl.BlockSpec((tm, D_MODEL), lambda i: (i, 0))),
        compiler_params=_params("parallel"),
        name="mem_kv",
    )(mem2d, norm_w.reshape(1, D_MODEL).astype(F32), w_kv.astype(BF16))


def _xattn_kernel(x_ref, nw_ref, wq_ref, k_ref, v_ref, wo_ref, o_ref):
    x = x_ref[...]
    q = _dot(_rms(x, nw_ref[...]).astype(BF16), wq_ref[...]).astype(BF16)
    heads = [slice(h * XA_HEAD_DIM, (h + 1) * XA_HEAD_DIM) for h in range(XA_HEADS)]
    s = [_dot_nt(q[:, cols], k_ref[:, cols]) * (XA_HEAD_DIM ** -0.5) for cols in heads]
    e = [jnp.exp(s_h - jnp.max(s_h, axis=-1, keepdims=True)) for s_h in s]
    inv_l = [1.0 / jnp.sum(e_h, axis=-1, keepdims=True) for e_h in e]
    o = [(_dot(e_h.astype(BF16), v_ref[:, cols]) * il).astype(BF16) for e_h, il, cols in zip(e, inv_l, heads)]
    o_ref[...] = x + _dot(jnp.concatenate(o, axis=1), wo_ref[...])


def _xattn(x3d, norm_w, w_q, k, v, w_o):
    b, s, _ = x3d.shape
    tm = min(TM_ROWS, s)
    n_mem = k.shape[1]
    return pl.pallas_call(
        _xattn_kernel,
        out_shape=jax.ShapeDtypeStruct(x3d.shape, F32),
        grid=(b, s // tm),
        in_specs=[pl.BlockSpec((None, tm, D_MODEL), lambda bi, i: (bi, i, 0)),
                  pl.BlockSpec((1, D_MODEL), lambda bi, i: (0, 0)),
                  pl.BlockSpec((D_MODEL, D_MODEL), lambda bi, i: (0, 0)),
                  pl.BlockSpec((None, n_mem, D_MODEL), lambda bi, i: (bi, 0, 0)),
                  pl.BlockSpec((None, n_mem, D_MODEL), lambda bi, i: (bi, 0, 0)),
                  pl.BlockSpec((D_MODEL, D_MODEL), lambda bi, i: (0, 0))],
        out_specs=pl.BlockSpec((None, tm, D_MODEL), lambda bi, i: (bi, i, 0)),
        compiler_params=_params("parallel", "parallel"),
        name="cross_attention",
    )(x3d, norm_w.reshape(1, D_MODEL).astype(F32), w_q.astype(BF16), k, v, w_o.astype(BF16))


FF_CHUNK = 512


def _ffn_kernel(x_ref, nw_ref, w1_ref, w2_ref, fw_ref, o_ref, *, final_norm):
    x = x_ref[...]
    h = _rms(x, nw_ref[...]).astype(BF16)
    acc = x
    for c0 in range(0, D_FF, FF_CHUNK):
        a = jnp.maximum(_dot(h, w1_ref[:, c0:c0 + FF_CHUNK]), 0.0)
        acc = acc + _dot((a * a).astype(BF16), w2_ref[c0:c0 + FF_CHUNK, :])
    o_ref[...] = _rms(acc, fw_ref[...]) if final_norm else acc


def _ffn(x2d, norm_w, w1, w2, final_w, final_norm):
    t = x2d.shape[0]
    tm = min(TM_ROWS, t)
    return pl.pallas_call(
        functools.partial(_ffn_kernel, final_norm=final_norm),
        out_shape=jax.ShapeDtypeStruct((t, D_MODEL), F32),
        grid=(t // tm,),
        in_specs=[pl.BlockSpec((tm, D_MODEL), lambda i: (i, 0)),
                  pl.BlockSpec((1, D_MODEL), lambda i: (0, 0)),
                  pl.BlockSpec((D_MODEL, D_FF), lambda i: (0, 0), pipeline_mode=pl.Buffered(1)),
                  pl.BlockSpec((D_FF, D_MODEL), lambda i: (0, 0), pipeline_mode=pl.Buffered(1)),
                  pl.BlockSpec((1, D_MODEL), lambda i: (0, 0))],
        out_specs=pl.BlockSpec((tm, D_MODEL), lambda i: (i, 0)),
        compiler_params=_params("parallel"),
        name="ffn",
    )(x2d, norm_w.reshape(1, D_MODEL).astype(F32), w1.astype(BF16), w2.astype(BF16),
      final_w.reshape(1, D_MODEL).astype(F32))


def kernel(x, mem, norm_mix_w, w_in, na_rel_bias, ml_i_bias, ml_f_bias, ml_norm_w, gdn_conv_w, gdn_a_log,
           gdn_dt_bias, gdn_norm_w, w_out, norm_xa_w, norm_mem_w, w_xq, w_xkv, w_xo, norm_ffn_w, w_ff1,
           w_ff2, norm_out_w):
    b, s, d = x.shape
    depth = w_in.shape[0]
    x2d = x.reshape(b * s, d).astype(F32)
    mem2d = mem.reshape(-1, d).astype(F32)
    w_in_p = _permute_w_in(w_in)
    w_out_b, w_xq_b, w_xkv_b, w_xo_b, w_ff1_b, w_ff2_b = (
        w.astype(BF16) for w in (w_out, w_xq, w_xkv, w_xo, w_ff1, w_ff2))
    seq = lambda a: a.reshape(b, s, a.shape[-1])
    flat = lambda a: a.reshape(b * s, a.shape[-1])
    n_mem = mem.shape[1]
    for l in range(depth):
        (gq, gk, gv, na_q, na_k, na_v, ml_q, ml_k, ml_v, ml_o, gd_z, gates) = _in_proj(
            x2d, norm_mix_w[l].astype(F32), w_in_p[l], gdn_conv_w[l], s)
        gate_rows = _gate_rows(seq(gates))
        y_na = _neighbourhood_attention(seq(na_q), seq(na_k), seq(na_v), _na_bias_table(na_rel_bias[l]))
        hf, hb = _mlstm(seq(ml_q), seq(ml_k), seq(ml_v), gate_rows, ml_i_bias[l], ml_f_bias[l])
        of, ob = _gdn(seq(gq), seq(gk), seq(gv), gate_rows, gdn_a_log[l], gdn_dt_bias[l])
        x2d = _out_proj(x2d, flat(y_na), flat(hf), flat(hb), ml_o, ml_norm_w[l], flat(of), flat(ob), gd_z,
                        gdn_norm_w[l], w_out_b[l])
        mk, mv = _mem_kv(mem2d, norm_mem_w[l], w_xkv_b[l])
        x2d = _xattn(x2d.reshape(b, s, d), norm_xa_w[l], w_xq_b[l], mk.reshape(b, n_mem, d),
                     mv.reshape(b, n_mem, d), w_xo_b[l]).reshape(b * s, d)
        x2d = _ffn(x2d, norm_ffn_w[l], w_ff1_b[l], w_ff2_b[l], norm_out_w, final_norm=(l == depth - 1))
    return x2d.reshape(b, s, d).astype(x.dtype)
```
